```python
import jax, jax.numpy as jnp
from jax import lax
import numpy as np

D_MODEL = 1024
BATCH = 4
SEQ = 4096
DEPTH = 2
DEC_BATCH = 128
DEC_SEQ = 4
PAST_LEN = 2048
PAGE_SIZE = 128

BRANCH_W = D_MODEL // 2
N_BRANCH = 4
N_PARTS = 13
IN_W = N_PARTS * BRANCH_W
CHUNK = 128
A_GROUPS = 4
A_GW = BRANCH_W // A_GROUPS
CONV_W = 3
C_HEADS = 8
C_HD = BRANCH_W // C_HEADS
ROPE_DIM = C_HD // 4
ROPE_THETA = 500000.0
MOBA_BLOCK = 256
MOBA_TOPK = 3
MOBA_Q_BLOCK = 64
N_MEM = 256
M_HEADS = 4
M_HD = BRANCH_W // M_HEADS
EPS = 1e-6

kernel_name = 'hybrid_gated_branch_decoder_step'


def _rmsnorm(x, g):
    xf = x.astype(jnp.float32)
    r = lax.rsqrt(jnp.mean(xf * xf, axis=-1, keepdims=True) + EPS)
    return (xf * r).astype(x.dtype) * g


def _layernorm(x, g):
    xf = x.astype(jnp.float32)
    mu = jnp.mean(xf, axis=-1, keepdims=True)
    xc = xf - mu
    r = lax.rsqrt(jnp.mean(xc * xc, axis=-1, keepdims=True) + EPS)
    return (xc * r).astype(x.dtype) * g


def _rope(x, pos):
    half = ROPE_DIM // 2
    inv = jnp.power(jnp.float32(ROPE_THETA), -jnp.arange(half, dtype=jnp.float32) * (2.0 / ROPE_DIM))
    ang = pos.astype(jnp.float32)[:, None] * inv[None, :]
    cos = jnp.cos(ang)[None, :, None, :]
    sin = jnp.sin(ang)[None, :, None, :]
    x1 = x[..., :half].astype(jnp.float32)
    x2 = x[..., half:ROPE_DIM].astype(jnp.float32)
    rot = jnp.concatenate([x1 * cos - x2 * sin, x2 * cos + x1 * sin], axis=-1).astype(x.dtype)
    return jnp.concatenate([rot, x[..., ROPE_DIM:]], axis=-1)


def _moba(q, k, v, q_pos):
    B, L, H, d = k.shape
    T = q.shape[1]
    nb = -(-L // MOBA_BLOCK)
    pad = nb * MOBA_BLOCK - L
    kb = jnp.pad(k, ((0, 0), (0, pad), (0, 0), (0, 0))).reshape(B, nb, MOBA_BLOCK, H, d)
    vb = jnp.pad(v, ((0, 0), (0, pad), (0, 0), (0, 0))).reshape(B, nb, MOBA_BLOCK, H, d)
    kmean = jnp.mean(kb.astype(jnp.float32), axis=2)
    kb = kb.transpose(0, 3, 1, 2, 4)
    vb = vb.transpose(0, 3, 1, 2, 4)
    n_sel = min(MOBA_TOPK, nb)
    qb = min(T, MOBA_Q_BLOCK)
    n_qb = T // qb
    q_blocks = q.reshape(B, n_qb, qb, H, d).transpose(1, 0, 2, 3, 4)
    pos_blocks = q_pos.reshape(n_qb, qb)
    b_ix = jnp.arange(B)[:, None, None, None]
    h_ix = jnp.arange(H)[None, None, :, None]
    offs = jnp.arange(MOBA_BLOCK, dtype=jnp.int32)
    blk_ids = jnp.arange(nb, dtype=jnp.int32)
    scale = d ** -0.5

    def one_block(args):
        qq, pp = args
        own = pp // MOBA_BLOCK
        gate = jnp.einsum('bqhd,bnhd->bqhn', qq.astype(jnp.float32), kmean)
        fully_past = blk_ids[None, :] < own[:, None]
        gate = jnp.where(fully_past[None, :, None, :], gate, -jnp.inf)
        _, top_i = lax.top_k(gate, n_sel)
        sel_ok = top_i < own[None, :, None, None]
        own_b = jnp.broadcast_to(own[None, :, None, None], (B, qb, H, 1))
        blocks = jnp.concatenate([top_i, own_b], axis=-1)
        valid = jnp.concatenate([sel_ok, jnp.ones((B, qb, H, 1), bool)], axis=-1)
        kg = kb[b_ix, h_ix, blocks]
        vg = vb[b_ix, h_ix, blocks]
        logits = jnp.einsum('bqhd,bqhnpd->bqhnp', qq, kg).astype(jnp.float32) * scale
        kpos = blocks[..., None] * MOBA_BLOCK + offs
        mask = valid[..., None] & (kpos <= pp[None, :, None, None, None])
        logits = jnp.where(mask, logits, -jnp.inf)
        probs = jax.nn.softmax(logits.reshape(B, qb, H, -1), axis=-1).reshape(logits.shape)
        return jnp.einsum('bqhnp,bqhnpd->bqhd', probs.astype(v.dtype), vg)

    out = lax.map(one_block, (q_blocks, pos_blocks))
    return out.transpose(1, 0, 2, 3, 4).reshape(B, T, H, d)


def _mem_attend(q, mk, mv):
    logits = jnp.einsum('bthd,bmhd->bhtm', q, mk).astype(jnp.float32) * (M_HD ** -0.5)
    p = jax.nn.softmax(logits, axis=-1).astype(mv.dtype)
    return jnp.einsum('bhtm,bmhd->bthd', p, mv)


def _mem_kv(mem, g_mem, w_mem_kv):
    B = mem.shape[0]
    mk, mv = jnp.split(_rmsnorm(mem, g_mem) @ w_mem_kv, 2, axis=-1)
    return mk.reshape(B, -1, M_HEADS, M_HD), mv.reshape(B, -1, M_HEADS, M_HD)


def _layer(x, pos, conv_prev, k_past, v_past, mk, mv,
           g_pre, g_post, w_in, ln_v_gain, w_spatial, b_spatial, conv_w,
           w_merge, b_merge, w_branch, w_out):
    B, T, _ = x.shape
    h = _rmsnorm(x, g_pre)
    (a_u, a_v, a_g, b_b, b_c, b_x, b_g,
     c_q, c_k, c_v, c_g, m_q, m_g) = jnp.split(h @ w_in, N_PARTS, axis=-1)
    lc = min(T, CHUNK)
    nc = T // lc
    vn = _layernorm(a_v, ln_v_gain)
    ws = jnp.where(jnp.tril(jnp.ones((lc, lc), bool))[None], w_spatial[:, :lc, :lc], 0.0)
    vc = vn.reshape(B, nc, lc, A_GROUPS, A_GW)
    sp = jnp.einsum('gts,bcsge->bctge', ws, vc) + b_spatial[:, :lc].T[None, None, :, :, None]
    out_a = a_u * sp.reshape(B, T, BRANCH_W)
    z = b_c * b_x
    zp = jnp.concatenate([conv_prev.astype(z.dtype), z], axis=1)
    conv = conv_w[0] * zp[:, 0:T]
    for j in range(1, CONV_W):
        conv = conv + conv_w[j] * zp[:, j:j + T]
    out_b = b_b * conv
    new_conv = zp[:, T:]
    q = _rope(c_q.reshape(B, T, C_HEADS, C_HD), pos)
    k = _rope(c_k.reshape(B, T, C_HEADS, C_HD), pos)
    v = c_v.reshape(B, T, C_HEADS, C_HD)
    if k_past is None:
        k_all, v_all = k, v
    else:
        k_all = jnp.concatenate([k_past.astype(k.dtype), k], axis=1)
        v_all = jnp.concatenate([v_past.astype(v.dtype), v], axis=1)
    out_c = _moba(q, k_all, v_all, pos).reshape(B, T, BRANCH_W)
    out_m = _mem_attend(m_q.reshape(B, T, M_HEADS, M_HD), mk, mv).reshape(B, T, BRANCH_W)
    branches = jnp.stack([jax.nn.silu(a_g) * out_a, jax.nn.silu(b_g) * out_b,
                          jax.nn.silu(c_g) * out_c, jax.nn.silu(m_g) * out_m], axis=2)
    gates = jax.nn.sigmoid(h @ w_merge + b_merge).reshape(B, T, N_BRANCH, D_MODEL)
    proj = jnp.einsum('btnw,nwd->btnd', branches, w_branch)
    y = jnp.einsum('btnd,btnd->btd', gates, proj) @ w_out
    return x + _rmsnorm(y, g_post), k, v, new_conv, vn


def setup_inputs(seed: int = 0) -> dict:
    key = jax.random.key(seed)
    ks = jax.random.split(key, 24)
    n_pages = PAST_LEN // PAGE_SIZE
    n_pool = (DEC_BATCH * n_pages * 5 + 3) // 4
    f32 = jnp.float32

    def nrm(k, shape, scale):
        return jax.random.normal(k, shape, f32) * scale

    page_table = jax.random.permutation(ks[9], n_pool)[:DEC_BATCH * n_pages]
    page_table = page_table.reshape(DEC_BATCH, n_pages).astype(jnp.int32)
    return {
        'x_prompt': nrm(ks[0], (BATCH, SEQ, D_MODEL), 1.0),
        'x_sample': nrm(ks[1], (DEC_BATCH, DEC_SEQ, D_MODEL), 1.0),
        'cache_k': nrm(ks[2], (DEPTH, n_pool, PAGE_SIZE, C_HEADS, C_HD), 1.0),
        'cache_v': nrm(ks[3], (DEPTH, n_pool, PAGE_SIZE, C_HEADS, C_HD), 1.0),
        'cache_mem_k': nrm(ks[4], (DEPTH, DEC_BATCH, N_MEM, M_HEADS, M_HD), 1.0),
        'cache_mem_v': nrm(ks[5], (DEPTH, DEC_BATCH, N_MEM, M_HEADS, M_HD), 1.0),
        'state_conv': nrm(ks[6], (DEPTH, DEC_BATCH, CONV_W - 1, BRANCH_W), 1.0),
        'page_table': page_table,
        'mem_prompt': nrm(ks[7], (BATCH, N_MEM, D_MODEL), 1.0),
        'g_pre': 1.0 + nrm(ks[10], (DEPTH, D_MODEL), 0.05),
        'g_post': 1.0 + nrm(ks[11], (DEPTH, D_MODEL), 0.05),
        'w_in': nrm(ks[12], (DEPTH, D_MODEL, IN_W), D_MODEL ** -0.5),
        'ln_v_gain': 1.0 + nrm(ks[13], (DEPTH, BRANCH_W), 0.05),
        'w_spatial': nrm(ks[14], (DEPTH, A_GROUPS, CHUNK, CHUNK), CHUNK ** -0.5),
        'b_spatial': 1.0 + nrm(ks[15], (DEPTH, A_GROUPS, CHUNK), 0.1),
        'conv_w': nrm(ks[16], (DEPTH, CONV_W, BRANCH_W), CONV_W ** -0.5),
        'g_mem': 1.0 + nrm(ks[17], (DEPTH, D_MODEL), 0.05),
        'w_mem_kv': nrm(ks[18], (DEPTH, D_MODEL, 2 * BRANCH_W), D_MODEL ** -0.5),
        'w_merge': nrm(ks[19], (DEPTH, D_MODEL, N_BRANCH * D_MODEL), D_MODEL ** -0.5),
        'b_merge': nrm(ks[20], (DEPTH, N_BRANCH * D_MODEL), 0.1),
        'w_branch': nrm(ks[21], (DEPTH, N_BRANCH, BRANCH_W, D_MODEL), BRANCH_W ** -0.5),
        'w_out': nrm(ks[22], (DEPTH, D_MODEL, D_MODEL), D_MODEL ** -0.5),
    }


def reference(x_prompt, x_sample, cache_k, cache_v, cache_mem_k, cache_mem_v, state_conv, page_table,
              mem_prompt, g_pre, g_post, w_in, ln_v_gain, w_spatial, b_spatial, conv_w, g_mem, w_mem_kv,
              w_merge, b_merge, w_branch, w_out):
    bp, tp = x_prompt.shape[0], x_prompt.shape[1]
    bs, ts = x_sample.shape[0], x_sample.shape[1]
    past_len = page_table.shape[1] * cache_k.shape[2]
    pos_p = jnp.arange(tp, dtype=jnp.int32)
    pos_s = past_len + jnp.arange(ts, dtype=jnp.int32)
    hp, hs = x_prompt, x_sample
    kp_l, vp_l, cp_l, mkp_l, mvp_l = [], [], [], [], []
    ks_l, vs_l, cs_l, vns_l = [], [], [], []
    for l in range(DEPTH):
        lw = (g_pre[l], g_post[l], w_in[l], ln_v_gain[l], w_spatial[l], b_spatial[l], conv_w[l],
              w_merge[l], b_merge[l], w_branch[l], w_out[l])
        mk_p, mv_p = _mem_kv(mem_prompt, g_mem[l], w_mem_kv[l])
        conv0 = jnp.zeros((bp, CONV_W - 1, BRANCH_W), hp.dtype)
        hp, kp, vp, cp, _ = _layer(hp, pos_p, conv0, None, None, mk_p, mv_p, *lw)
        k_past = cache_k[l, page_table].reshape(bs, past_len, C_HEADS, C_HD)
        v_past = cache_v[l, page_table].reshape(bs, past_len, C_HEADS, C_HD)
        hs, kn, vn_, cs, vns = _layer(hs, pos_s, state_conv[l], k_past, v_past,
                                      cache_mem_k[l], cache_mem_v[l], *lw)
        kp_l.append(kp); vp_l.append(vp); cp_l.append(cp); mkp_l.append(mk_p); mvp_l.append(mv_p)
        ks_l.append(kn); vs_l.append(vn_); cs_l.append(cs); vns_l.append(vns)
    return (hp, hs,
            jnp.stack(kp_l), jnp.stack(vp_l), jnp.stack(cp_l), jnp.stack(mkp_l), jnp.stack(mvp_l),
            jnp.stack(ks_l), jnp.stack(vs_l), jnp.stack(cs_l), jnp.stack(vns_l))
```

```python
import functools

import jax
import jax.numpy as jnp
from jax import lax
from jax.experimental import pallas as pl
from jax.experimental.pallas import tpu as pltpu

D_MODEL = 1024
BRANCH_W = 512
N_PARTS = 13
IN_W = N_PARTS * BRANCH_W
CHUNK = 128
A_GROUPS = 4
CONV_W = 3
C_HEADS = 8
C_HD = 64
ROPE_DIM = 16
ROPE_THETA = 500000.0
MOBA_BLOCK = 256
MOBA_TOPK = 3
M_HEADS = 4
M_HD = 128
N_MEM = 256
EPS = 1e-6

P_AU, P_AV, P_AG, P_BB, P_BC, P_BX, P_BG, P_CQ, P_CK, P_CV, P_CG, P_MQ, P_MG = range(13)

SAMPLE_ROWS = 8
NEG = -1e30
GATE_COLS = 128
VMEM_LIMIT = 56 * 1024 * 1024

F32 = jnp.float32
BF16 = jnp.bfloat16
NT_DIMS = (((1,), (1,)), ((), ()))


def _cparams(sem):
    return pltpu.CompilerParams(dimension_semantics=sem, vmem_limit_bytes=VMEM_LIMIT)


def _rms(x, g):
    r = lax.rsqrt(jnp.mean(x * x, axis=-1, keepdims=True) + EPS)
    return (x * r) * g


def _layernorm(x, g):
    mu = jnp.mean(x, axis=-1, keepdims=True)
    xc = x - mu
    r = lax.rsqrt(jnp.mean(xc * xc, axis=-1, keepdims=True) + EPS)
    return (xc * r) * g


def _sigmoid(x):
    return 1.0 / (1.0 + jnp.exp(-x))


def _silu(x):
    return x * _sigmoid(x)


def _rope(x, tab):
    c, s_lo, s_hi = tab[:, 0:128], tab[:, 128:256], tab[:, 256:384]
    half = ROPE_DIM // 2
    outs = []
    for g in range(BRANCH_W // 128):
        xg = x[:, g * 128:(g + 1) * 128]
        outs.append(xg * c + pltpu.roll(xg, 128 - half, axis=1) * s_lo + pltpu.roll(xg, half, axis=1) * s_hi)
    return jnp.concatenate(outs, axis=1)


def _rope_table(pos):
    half = ROPE_DIM // 2
    inv = jnp.power(jnp.float32(ROPE_THETA), -jnp.arange(half, dtype=F32) * (2.0 / ROPE_DIM))
    ang = pos.astype(F32)[:, None] * inv[None, :]
    cos, sin = jnp.cos(ang), jnp.sin(ang)
    d = jnp.arange(128) % C_HD
    idx = d % half
    cosl = jnp.where(d[None, :] < ROPE_DIM, cos[:, idx], 1.0)
    s_lo = jnp.where(d[None, :] < half, -sin[:, idx], 0.0)
    s_hi = jnp.where((d[None, :] >= half) & (d[None, :] < ROPE_DIM), sin[:, idx], 0.0)
    return jnp.concatenate([cosl, s_lo, s_hi], axis=1).astype(F32)


def _top_blocks(gate, valid, n_blocks):
    idx = lax.broadcasted_iota(jnp.int32, gate.shape, 1).astype(F32)
    g = jnp.where(valid, gate, -jnp.inf)
    sel = jnp.zeros(gate.shape, F32)
    for _ in range(MOBA_TOPK):
        m = jnp.max(g, axis=1, keepdims=True)
        cand = (g == m) & valid & (sel == 0.0)
        first = jnp.min(jnp.where(cand, idx, float(n_blocks)), axis=1, keepdims=True)
        pick = idx == first
        sel = jnp.where(pick, 1.0, sel)
        g = jnp.where(pick, -jnp.inf, g)
    return sel


def _inproj_kernel(x_ref, g_ref, w_ref, o_ref, h_ref):
    @pl.when(pl.program_id(1) == 0)
    def _():
        h_ref[...] = _rms(x_ref[...], g_ref[...]).astype(BF16)

    o_ref[...] = jnp.dot(h_ref[...], w_ref[...], preferred_element_type=F32)


def _inproj(x, g, w_bf16, tm, tn):
    n, d = x.shape
    n_out = w_bf16.shape[1]
    return pl.pallas_call(
        _inproj_kernel,
        grid=(n // tm, n_out // tn),
        in_specs=[
            pl.BlockSpec((tm, d), lambda i, j: (i, 0)),
            pl.BlockSpec((1, d), lambda i, j: (0, 0)),
            pl.BlockSpec((d, tn), lambda i, j: (0, j)),
        ],
        out_specs=pl.BlockSpec((tm, tn), lambda i, j: (i, j)),
        out_shape=jax.ShapeDtypeStruct((n, n_out), F32),
        scratch_shapes=[pltpu.VMEM((tm, d), BF16)],
        compiler_params=_cparams(("parallel", "arbitrary")),
        name="inproj",
    )(x, g.reshape(1, d), w_bf16)


KV_TM = 1024


def _kvpost_kernel(k_ref, v_ref, tab_ref, ko_ref, vo_ref, kb_ref, vb_ref, km_ref):
    k = _rope(k_ref[...], tab_ref[...])
    v = v_ref[...]
    ko_ref[...] = k
    vo_ref[...] = v
    kb_ref[...] = k.astype(BF16)
    vb_ref[...] = v.astype(BF16)
    for i in range(KV_TM // MOBA_BLOCK):
        blk = k[i * MOBA_BLOCK:(i + 1) * MOBA_BLOCK, :]
        km_ref[0, i:i + 1, :] = jnp.sum(blk, axis=0, keepdims=True) * (1.0 / MOBA_BLOCK)


def _kvpost(parts, tab, seq):
    n = parts.shape[0]
    tiles_per_seq = seq // KV_TM
    blk = lambda c: pl.BlockSpec((KV_TM, BRANCH_W), lambda i: (i, c))
    row = pl.BlockSpec((KV_TM, BRANCH_W), lambda i: (i, 0))
    return pl.pallas_call(
        _kvpost_kernel,
        grid=(n // KV_TM,),
        in_specs=[blk(P_CK), blk(P_CV),
                  pl.BlockSpec((KV_TM, 384), lambda i: (i % tiles_per_seq, 0))],
        out_specs=[row, row, row, row,
                   pl.BlockSpec((1, KV_TM // MOBA_BLOCK, BRANCH_W), lambda i: (i, 0, 0))],
        out_shape=[jax.ShapeDtypeStruct((n, BRANCH_W), F32), jax.ShapeDtypeStruct((n, BRANCH_W), F32),
                   jax.ShapeDtypeStruct((n, BRANCH_W), BF16), jax.ShapeDtypeStruct((n, BRANCH_W), BF16),
                   jax.ShapeDtypeStruct((n // KV_TM, KV_TM // MOBA_BLOCK, BRANCH_W), F32)],
        compiler_params=_cparams(("parallel",)),
        name="kvpost",
    )(parts, parts, tab)


def _moba_prompt_kernel(q_ref, tab_ref, kb_ref, vb_ref, km_ref, o_ref):
    j = pl.program_id(1)
    tq = MOBA_BLOCK
    nb = km_ref.shape[1]
    q = _rope(q_ref[...], tab_ref[...])
    km = jnp.concatenate([km_ref[0], jnp.zeros((GATE_COLS - nb, BRANCH_W), F32)], axis=0)
    lane = lax.broadcasted_iota(jnp.int32, (tq, 128), 1)
    row = lax.broadcasted_iota(jnp.int32, (tq, MOBA_BLOCK), 0)
    col = lax.broadcasted_iota(jnp.int32, (tq, MOBA_BLOCK), 1)
    causal = col <= row
    blk_id = lax.broadcasted_iota(jnp.int32, (tq, GATE_COLS), 1)
    past = blk_id < j
    own0 = pl.multiple_of(j * MOBA_BLOCK, MOBA_BLOCK)
    outs = []
    for g in range(C_HEADS // 2):
        lanes = slice(g * 128, (g + 1) * 128)
        qg = q[:, lanes]
        kmg = km[:, lanes]
        accs = []
        for hh in range(2):
            in_head = (lane >= hh * C_HD) & (lane < (hh + 1) * C_HD)
            qh = jnp.where(in_head, qg, 0.0)
            gate = lax.dot_general(qh, kmg, NT_DIMS, precision=lax.Precision.HIGHEST,
                                   preferred_element_type=F32)
            sel = _top_blocks(gate, past, GATE_COLS)
            qb = (qh * (C_HD ** -0.5)).astype(BF16)
            s = lax.dot_general(qb, kb_ref[pl.ds(own0, MOBA_BLOCK), lanes], NT_DIMS,
                                preferred_element_type=F32)
            s = jnp.where(causal, s, NEG)
            m = jnp.max(s, axis=1, keepdims=True)
            p = jnp.exp(s - m)
            l = jnp.sum(p, axis=1, keepdims=True)
            acc = jnp.dot(p.astype(BF16), vb_ref[pl.ds(own0, MOBA_BLOCK), lanes],
                          preferred_element_type=F32)

            def body(n, carry, qb=qb, sel=sel, lanes=lanes):
                m, l, acc = carry
                r0 = pl.multiple_of(n * MOBA_BLOCK, MOBA_BLOCK)
                s = lax.dot_general(qb, kb_ref[pl.ds(r0, MOBA_BLOCK), lanes], NT_DIMS,
                                    preferred_element_type=F32)
                sel_n = jnp.sum(jnp.where(blk_id == n, sel, 0.0), axis=1, keepdims=True)
                s = jnp.where(sel_n > 0.0, s, NEG)
                m_new = jnp.maximum(m, jnp.max(s, axis=1, keepdims=True))
                a = jnp.exp(m - m_new)
                p = jnp.exp(s - m_new)
                l = a * l + jnp.sum(p, axis=1, keepdims=True)
                acc = a * acc + jnp.dot(p.astype(BF16), vb_ref[pl.ds(r0, MOBA_BLOCK), lanes],
                                        preferred_element_type=F32)
                return m_new, l, acc

            m, l, acc = lax.fori_loop(0, j, body, (m, l, acc))
            accs.append(acc / l)
        outs.append(jnp.where(lane < C_HD, accs[0], accs[1]))
    o_ref[...] = jnp.concatenate(outs, axis=1)


def _moba_prompt(parts, tab, kb, vb, kmean, batch, seq):
    n = parts.shape[0]
    nb = seq // MOBA_BLOCK
    return pl.pallas_call(
        _moba_prompt_kernel,
        grid=(batch, nb),
        in_specs=[
            pl.BlockSpec((MOBA_BLOCK, BRANCH_W), lambda b, j: (b * nb + j, P_CQ)),
            pl.BlockSpec((MOBA_BLOCK, 384), lambda b, j: (j, 0)),
            pl.BlockSpec((seq, BRANCH_W), lambda b, j: (b, 0)),
            pl.BlockSpec((seq, BRANCH_W), lambda b, j: (b, 0)),
            pl.BlockSpec((1, nb, BRANCH_W), lambda b, j: (b, 0, 0)),
        ],
        out_specs=pl.BlockSpec((MOBA_BLOCK, BRANCH_W), lambda b, j: (b * nb + j, 0)),
        out_shape=jax.ShapeDtypeStruct((n, BRANCH_W), F32),
        compiler_params=_cparams(("parallel", "arbitrary")),
        name="moba_prompt",
    )(parts, tab, kb, vb, kmean)


def _tail(x, branches, gpre_ref, gpost_ref, wm_ref, bm_ref, wb_ref, wo_ref):
    h = _rms(x, gpre_ref[...]).astype(BF16)
    acc = None
    for n, br in enumerate(branches):
        cols = slice(n * D_MODEL, (n + 1) * D_MODEL)
        gate = _sigmoid(jnp.dot(h, wm_ref[:, cols], preferred_element_type=F32) + bm_ref[:, cols])
        proj = jnp.dot(br.astype(BF16), wb_ref[n], preferred_element_type=F32)
        acc = gate * proj if acc is None else acc + gate * proj
    y = jnp.dot(acc.astype(BF16), wo_ref[...], preferred_element_type=F32)
    return x + _rms(y, gpost_ref[...])


def _mem_attend(mq, mk, mv):
    outs = []
    for h in range(M_HEADS):
        lanes = slice(h * M_HD, (h + 1) * M_HD)
        s = lax.dot_general(mq[:, lanes].astype(BF16), mk[:, lanes], NT_DIMS,
                            preferred_element_type=F32) * (M_HD ** -0.5)
        m = jnp.max(s, axis=1, keepdims=True)
        p = jnp.exp(s - m)
        l = jnp.sum(p, axis=1, keepdims=True)
        outs.append(jnp.dot(p.astype(BF16), mv[:, lanes], preferred_element_type=F32) / l)
    return jnp.concatenate(outs, axis=1)


TAIL_TM = 256


def _prompt_tail_kernel(x_ref, au_ref, av_ref, ag_ref, bb_ref, bc_ref, bx_ref, bg_ref, cg_ref, mq_ref, mg_ref,
                        hc_ref, hx_ref, oc_ref, mk_ref, mv_ref,
                        gpre_ref, gpost_ref, lng_ref, ws_ref, bs_ref, cw_ref, wm_ref, bm_ref, wb_ref, wo_ref,
                        y_ref, zt_ref, *, tiles_per_seq):
    tm = TAIL_TM
    x = x_ref[...]
    vn = _layernorm(av_ref[...], lng_ref[...]).astype(BF16)
    t_i = lax.broadcasted_iota(jnp.int32, (CHUNK, CHUNK), 0)
    s_i = lax.broadcasted_iota(jnp.int32, (CHUNK, CHUNK), 1)
    sp_cols = []
    for g in range(A_GROUPS):
        lanes = slice(g * 128, (g + 1) * 128)
        ws = jnp.where(s_i <= t_i, ws_ref[g], 0.0).astype(BF16)
        bias = bs_ref[:, g:g + 1]
        rows = [jnp.dot(ws, vn[c * CHUNK:(c + 1) * CHUNK, lanes], preferred_element_type=F32) + bias
                for c in range(tm // CHUNK)]
        sp_cols.append(jnp.concatenate(rows, axis=0))
    br_a = _silu(ag_ref[...]) * (au_ref[...] * jnp.concatenate(sp_cols, axis=1))
    z = bc_ref[...] * bx_ref[...]
    first_of_seq = (pl.program_id(0) % tiles_per_seq) == 0
    halo = jnp.where(first_of_seq, 0.0, hc_ref[...] * hx_ref[...])
    rix = lax.broadcasted_iota(jnp.int32, (tm, BRANCH_W), 0)
    z1 = jnp.where(rix == 0, halo[7:8, :], pltpu.roll(z, 1, axis=0))
    z2 = pltpu.roll(z, 2, axis=0)
    z2 = jnp.where(rix == 0, halo[6:7, :], jnp.where(rix == 1, halo[7:8, :], z2))
    conv = cw_ref[0:1, :] * z2 + cw_ref[1:2, :] * z1 + cw_ref[2:3, :] * z
    br_b = _silu(bg_ref[...]) * (bb_ref[...] * conv)
    zt_ref[0] = z[tm - 8:tm, :]
    br_c = _silu(cg_ref[...]) * oc_ref[...]
    br_m = _silu(mg_ref[...]) * _mem_attend(mq_ref[...], mk_ref[...].astype(BF16), mv_ref[...].astype(BF16))
    y_ref[...] = _tail(x, (br_a, br_b, br_c, br_m), gpre_ref, gpost_ref, wm_ref, bm_ref, wb_ref, wo_ref)


def _const_spec(shape):
    zeros = (0,) * len(shape)
    return pl.BlockSpec(shape, lambda i: zeros, pipeline_mode=pl.Buffered(1))


def _prompt_tail(x, parts, out_c, mkv, lw, seq):
    n = x.shape[0]
    tm = TAIL_TM
    tiles_per_seq = seq // tm
    part = lambda c: pl.BlockSpec((tm, BRANCH_W), lambda i: (i, c))
    halo = lambda c: pl.BlockSpec((8, BRANCH_W), lambda i: (jnp.maximum(i * (tm // 8) - 1, 0), c))
    row512 = pl.BlockSpec((tm, BRANCH_W), lambda i: (i, 0))
    in_specs = [
        pl.BlockSpec((tm, D_MODEL), lambda i: (i, 0)),
        part(P_AU), part(P_AV), part(P_AG), part(P_BB), part(P_BC), part(P_BX), part(P_BG),
        part(P_CG), part(P_MQ), part(P_MG),
        halo(P_BC), halo(P_BX),
        row512,
        pl.BlockSpec((N_MEM, BRANCH_W), lambda i: (i // tiles_per_seq, 0)),
        pl.BlockSpec((N_MEM, BRANCH_W), lambda i: (i // tiles_per_seq, 1)),
        _const_spec((1, D_MODEL)), _const_spec((1, D_MODEL)), _const_spec((1, BRANCH_W)),
        _const_spec((A_GROUPS, CHUNK, CHUNK)), _const_spec((CHUNK, A_GROUPS)), _const_spec((CONV_W, BRANCH_W)),
        _const_spec((D_MODEL, 4 * D_MODEL)), _const_spec((1, 4 * D_MODEL)),
        _const_spec((4, BRANCH_W, D_MODEL)), _const_spec((D_MODEL, D_MODEL)),
    ]
    return pl.pallas_call(
        functools.partial(_prompt_tail_kernel, tiles_per_seq=tiles_per_seq),
        grid=(n // tm,),
        in_specs=in_specs,
        out_specs=[pl.BlockSpec((tm, D_MODEL), lambda i: (i, 0)),
                   pl.BlockSpec((1, 8, BRANCH_W), lambda i: (i, 0, 0))],
        out_shape=[jax.ShapeDtypeStruct((n, D_MODEL), F32),
                   jax.ShapeDtypeStruct((n // tm, 8, BRANCH_W), F32)],
        compiler_params=_cparams(("parallel",)),
        name="prompt_tail",
    )(x, *([parts] * 10), parts, parts, out_c, mkv, mkv,
      lw["g_pre"], lw["g_post"], lw["ln_v_gain"], lw["w_spatial"], lw["b_spatial_t"], lw["conv_w"],
      lw["w_merge"], lw["b_merge"], lw["w_branch"], lw["w_out"])


N_PAGES = 16
PAGE = 128
PAGES_PER_BLOCK = MOBA_BLOCK // PAGE


def _sample_branch_kernel(pt_ref, parts_ref, tab_ref, conv_ref, mk_ref, mv_ref, *rest, dec_seq):
    k_pages = rest[0:N_PAGES]
    v_pages = rest[N_PAGES:2 * N_PAGES]
    lng_ref, wsx_ref, bsx_ref, cw_ref = rest[2 * N_PAGES:2 * N_PAGES + 4]
    ko_ref, vo_ref, co_ref, vn_ref, br_ref = rest[2 * N_PAGES + 4:]
    del pt_ref
    r8 = SAMPLE_ROWS
    part = lambda c: parts_ref[0, :, c * BRANCH_W:(c + 1) * BRANCH_W]
    rix = lax.broadcasted_iota(jnp.int32, (r8, BRANCH_W), 0)

    vn = _layernorm(part(P_AV), lng_ref[...])
    vn_ref[0] = vn
    sp = bsx_ref[...]
    for s in range(dec_seq):
        sp = sp + jnp.where(rix >= s, wsx_ref[s], 0.0) * vn[s:s + 1, :]
    br_ref[0, :, 0:BRANCH_W] = _silu(part(P_AG)) * (part(P_AU) * sp)

    z = part(P_BC) * part(P_BX)
    prev = conv_ref[0, 0]
    zrow = lambda i: prev[i:i + 1, :] if i < CONV_W - 1 else z[i - (CONV_W - 1):i - (CONV_W - 2), :]
    conv = jnp.zeros((r8, BRANCH_W), F32)
    for t in range(dec_seq):
        c_t = cw_ref[0:1, :] * zrow(t) + cw_ref[1:2, :] * zrow(t + 1) + cw_ref[2:3, :] * zrow(t + 2)
        conv = jnp.where(rix == t, c_t, conv)
    br_ref[0, :, BRANCH_W:2 * BRANCH_W] = _silu(part(P_BG)) * (part(P_BB) * conv)
    co_ref[0, 0] = z[dec_seq - (CONV_W - 1):dec_seq, :]

    tab = tab_ref[...]
    q = _rope(part(P_CQ), tab)
    k = _rope(part(P_CK), tab)
    v = part(P_CV)
    ko_ref[0] = k
    vo_ref[0] = v
    lane = lax.broadcasted_iota(jnp.int32, (r8, BRANCH_W), 1)
    qbd = jnp.concatenate([jnp.where((lane >= h * C_HD) & (lane < (h + 1) * C_HD), q, 0.0)
                           for h in range(C_HEADS)], axis=0)
    qbd_b = (qbd * (C_HD ** -0.5)).astype(BF16)
    n_past = N_PAGES // PAGES_PER_BLOCK
    s_blocks, v_blocks, kmeans = [], [], []
    for n in range(n_past):
        kn = jnp.concatenate([k_pages[n * PAGES_PER_BLOCK + i][0, 0] for i in range(PAGES_PER_BLOCK)], axis=0)
        vb = jnp.concatenate([v_pages[n * PAGES_PER_BLOCK + i][0, 0] for i in range(PAGES_PER_BLOCK)], axis=0)
        kmeans.append(jnp.sum(kn, axis=0, keepdims=True) * (1.0 / MOBA_BLOCK))
        s_blocks.append(lax.dot_general(qbd_b, kn.astype(BF16), NT_DIMS, preferred_element_type=F32))
        v_blocks.append(vb.astype(BF16))
    kmean = jnp.concatenate(kmeans + [jnp.zeros((GATE_COLS - n_past, BRANCH_W), F32)], axis=0)
    gate = lax.dot_general(qbd, kmean, NT_DIMS, precision=lax.Precision.HIGHEST,
                           preferred_element_type=F32)
    past = lax.broadcasted_iota(jnp.int32, gate.shape, 1) < n_past
    sel = _top_blocks(gate, past, GATE_COLS)
    s_blocks = [jnp.where(sel[:, n:n + 1] > 0.0, s_blocks[n], NEG) for n in range(n_past)]
    t_q = lax.broadcasted_iota(jnp.int32, (C_HEADS * r8, 1), 0) % r8
    s_own = []
    for c in range(dec_seq):
        prod = (qbd * (C_HD ** -0.5)) * k[c:c + 1, :]
        s_c = jnp.sum(prod, axis=1, keepdims=True)
        s_own.append(jnp.where(t_q >= c, s_c, NEG))
    m = s_own[0]
    for s_c in s_own[1:]:
        m = jnp.maximum(m, s_c)
    for s_n in s_blocks:
        m = jnp.maximum(m, jnp.max(s_n, axis=1, keepdims=True))
    l = jnp.zeros_like(m)
    acc = jnp.zeros((C_HEADS * r8, BRANCH_W), F32)
    for n in range(n_past):
        p = jnp.exp(s_blocks[n] - m)
        l = l + jnp.sum(p, axis=1, keepdims=True)
        acc = acc + jnp.dot(p.astype(BF16), v_blocks[n], preferred_element_type=F32)
    for c in range(dec_seq):
        p = jnp.exp(s_own[c] - m)
        l = l + p
        acc = acc + p * v[c:c + 1, :]
    acc = acc / l
    out_c = jnp.zeros((r8, BRANCH_W), F32)
    for h in range(C_HEADS):
        out_c = jnp.where((lane >= h * C_HD) & (lane < (h + 1) * C_HD), acc[h * r8:(h + 1) * r8, :], out_c)
    br_ref[0, :, 2 * BRANCH_W:3 * BRANCH_W] = _silu(part(P_CG)) * out_c

    out_m = _mem_attend(part(P_MQ), mk_ref[0, 0].astype(BF16), mv_ref[0, 0].astype(BF16))
    br_ref[0, :, 3 * BRANCH_W:4 * BRANCH_W] = _silu(part(P_MG)) * out_m


def _sample_branch(layer, parts3, tab, state_conv, cache_k, cache_v, cache_mem_k, cache_mem_v, page_table, lw,
                   dec_seq):
    bs = parts3.shape[0]
    r8 = SAMPLE_ROWS
    page_spec = lambda p: pl.BlockSpec((1, 1, PAGE, BRANCH_W), lambda b, pt: (layer, pt[b, p], 0, 0))
    const = lambda shape: pl.BlockSpec(shape, lambda b, pt: (0,) * len(shape))
    seq_spec = lambda rows, w: pl.BlockSpec((1, rows, w), lambda b, pt: (b, 0, 0))
    in_specs = [
        seq_spec(r8, IN_W),
        const((r8, 384)),
        pl.BlockSpec((1, 1, CONV_W - 1, BRANCH_W), lambda b, pt: (layer, b, 0, 0)),
        pl.BlockSpec((1, 1, N_MEM, BRANCH_W), lambda b, pt: (layer, b, 0, 0)),
        pl.BlockSpec((1, 1, N_MEM, BRANCH_W), lambda b, pt: (layer, b, 0, 0)),
    ] + [page_spec(p) for p in range(N_PAGES)] + [page_spec(p) for p in range(N_PAGES)] + [
        const((1, BRANCH_W)), const((dec_seq, r8, BRANCH_W)), const((r8, BRANCH_W)), const((CONV_W, BRANCH_W)),
    ]
    out_specs = [seq_spec(r8, BRANCH_W), seq_spec(r8, BRANCH_W),
                 pl.BlockSpec((1, 1, CONV_W - 1, BRANCH_W), lambda b, pt: (b, 0, 0, 0)),
                 seq_spec(r8, BRANCH_W), seq_spec(r8, 4 * BRANCH_W)]
    out_shape = [jax.ShapeDtypeStruct((bs, r8, BRANCH_W), F32), jax.ShapeDtypeStruct((bs, r8, BRANCH_W), F32),
                 jax.ShapeDtypeStruct((bs, 1, CONV_W - 1, BRANCH_W), F32),
                 jax.ShapeDtypeStruct((bs, r8, BRANCH_W), F32), jax.ShapeDtypeStruct((bs, r8, 4 * BRANCH_W), F32)]
    grid_spec = pltpu.PrefetchScalarGridSpec(
        num_scalar_prefetch=1, grid=(bs,), in_specs=in_specs, out_specs=out_specs)
    return pl.pallas_call(
        functools.partial(_sample_branch_kernel, dec_seq=dec_seq),
        grid_spec=grid_spec,
        out_shape=out_shape,
        compiler_params=_cparams(("arbitrary",)),
        name="sample_branch",
    )(page_table, parts3, tab, state_conv, cache_mem_k, cache_mem_v,
      *([cache_k] * N_PAGES), *([cache_v] * N_PAGES),
      lw["ln_v_gain"], lw["w_spatial_x"], lw["b_spatial_x"], lw["conv_w"])


def _sample_tail_kernel(x_ref, br_ref, gpre_ref, gpost_ref, wm_ref, bm_ref, wb_ref, wo_ref, y_ref):
    branches = tuple(br_ref[:, n * BRANCH_W:(n + 1) * BRANCH_W] for n in range(4))
    y_ref[...] = _tail(x_ref[...], branches, gpre_ref, gpost_ref, wm_ref, bm_ref, wb_ref, wo_ref)


def _sample_tail(x, branches, lw):
    n = x.shape[0]
    tm = TAIL_TM
    return pl.pallas_call(
        _sample_tail_kernel,
        grid=(n // tm,),
        in_specs=[
            pl.BlockSpec((tm, D_MODEL), lambda i: (i, 0)),
            pl.BlockSpec((tm, 4 * BRANCH_W), lambda i: (i, 0)),
            _const_spec((1, D_MODEL)), _const_spec((1, D_MODEL)),
            _const_spec((D_MODEL, 4 * D_MODEL)), _const_spec((1, 4 * D_MODEL)),
            _const_spec((4, BRANCH_W, D_MODEL)), _const_spec((D_MODEL, D_MODEL)),
        ],
        out_specs=pl.BlockSpec((tm, D_MODEL), lambda i: (i, 0)),
        out_shape=jax.ShapeDtypeStruct((n, D_MODEL), F32),
        compiler_params=_cparams(("parallel",)),
        name="sample_tail",
    )(x, branches, lw["g_pre"], lw["g_post"], lw["w_merge"], lw["b_merge"], lw["w_branch"], lw["w_out"])


def kernel(x_prompt, x_sample, cache_k, cache_v, cache_mem_k, cache_mem_v, state_conv, page_table, mem_prompt,
           g_pre, g_post, w_in, ln_v_gain, w_spatial, b_spatial, conv_w, g_mem, w_mem_kv, w_merge, b_merge,
           w_branch, w_out):
    bp, tp, d = x_prompt.shape
    bs, ts, _ = x_sample.shape
    depth = w_in.shape[0]
    n_pool, page = cache_k.shape[1], cache_k.shape[2]
    past_len = page_table.shape[1] * page
    assert (d, page, page_table.shape[1]) == (D_MODEL, PAGE, N_PAGES) and ts <= SAMPLE_ROWS
    assert past_len % MOBA_BLOCK == 0 and tp % KV_TM == 0 and ts >= CONV_W - 1
    r8 = SAMPLE_ROWS

    tab_p = _rope_table(jnp.arange(tp, dtype=jnp.int32))
    tab_s = _rope_table(past_len + jnp.arange(r8, dtype=jnp.int32))
    xp = x_prompt.reshape(bp * tp, d)
    xs = jnp.pad(x_sample, ((0, 0), (0, r8 - ts), (0, 0))).reshape(bs * r8, d)
    mem = mem_prompt.reshape(bp * N_MEM, d)
    ck = cache_k.reshape(depth, n_pool, page, BRANCH_W)
    cv = cache_v.reshape(depth, n_pool, page, BRANCH_W)
    cmk = cache_mem_k.reshape(depth, bs, N_MEM, BRANCH_W)
    cmv = cache_mem_v.reshape(depth, bs, N_MEM, BRANCH_W)
    lane_group = jnp.arange(BRANCH_W) // (BRANCH_W // A_GROUPS)

    outs = {name: [] for name in ("kp", "vp", "cp", "mkp", "mvp", "ks", "vs", "cs", "vns")}
    for l in range(depth):
        wsx = jnp.transpose(w_spatial[l][:, :r8, :ts], (2, 1, 0))[:, :, lane_group]
        bsx = jnp.transpose(b_spatial[l][:, :r8], (1, 0))[:, lane_group]
        lw = {
            "g_pre": g_pre[l].reshape(1, d), "g_post": g_post[l].reshape(1, d),
            "ln_v_gain": ln_v_gain[l].reshape(1, BRANCH_W),
            "w_spatial": w_spatial[l], "b_spatial_t": b_spatial[l].T,
            "w_spatial_x": wsx, "b_spatial_x": bsx,
            "conv_w": conv_w[l],
            "w_merge": w_merge[l].astype(BF16), "b_merge": b_merge[l].reshape(1, 4 * d),
            "w_branch": w_branch[l].astype(BF16), "w_out": w_out[l].astype(BF16),
        }
        w_in_b = w_in[l].astype(BF16)

        mkv = _inproj(mem, g_mem[l], w_mem_kv[l].astype(BF16), 1024, 1024)
        parts = _inproj(xp, g_pre[l], w_in_b, 1024, 1664)
        k_p, v_p, kb, vb, kmean = _kvpost(parts, tab_p, tp)
        kmean = kmean.reshape(bp, tp // MOBA_BLOCK, BRANCH_W)
        out_c = _moba_prompt(parts, tab_p, kb, vb, kmean, bp, tp)
        xp, ztail = _prompt_tail(xp, parts, out_c, mkv, lw, tp)
        outs["kp"].append(k_p.reshape(bp, tp, C_HEADS, C_HD))
        outs["vp"].append(v_p.reshape(bp, tp, C_HEADS, C_HD))
        outs["cp"].append(ztail.reshape(bp, tp // TAIL_TM, 8, BRANCH_W)[:, -1, 8 - (CONV_W - 1):, :])
        outs["mkp"].append(mkv[:, :BRANCH_W].reshape(bp, N_MEM, M_HEADS, M_HD))
        outs["mvp"].append(mkv[:, BRANCH_W:].reshape(bp, N_MEM, M_HEADS, M_HD))

        parts_s = _inproj(xs, g_pre[l], w_in_b, bs * r8, 1664).reshape(bs, r8, IN_W)
        k_s, v_s, c_s, vn_s, br_s = _sample_branch(l, parts_s, tab_s, state_conv, ck, cv, cmk, cmv, page_table,
                                                   lw, ts)
        xs = _sample_tail(xs, br_s.reshape(bs * r8, 4 * BRANCH_W), lw)
        outs["ks"].append(k_s[:, :ts].reshape(bs, ts, C_HEADS, C_HD))
        outs["vs"].append(v_s[:, :ts].reshape(bs, ts, C_HEADS, C_HD))
        outs["cs"].append(c_s.reshape(bs, CONV_W - 1, BRANCH_W))
        outs["vns"].append(vn_s[:, :ts])

    st = lambda name: jnp.stack(outs[name])
    return (xp.reshape(bp, tp, d), xs.reshape(bs, r8, d)[:, :ts],
            st("kp"), st("vp"), st("cp"), st("mkp"), st("mvp"),
            st("ks"), st("vs"), st("cs"), st("vns"))
```

```python
import functools

import jax
import jax.numpy as jnp
from jax import lax
from jax.experimental import pallas as pl
from jax.experimental.pallas import tpu as pltpu

D_MODEL = 1024
BRANCH_W = 512
N_PARTS = 13
IN_W = N_PARTS * BRANCH_W
CHUNK = 128
A_GROUPS = 4
CONV_W = 3
C_HEADS = 8
C_HD = 64
ROPE_DIM = 16
ROPE_THETA = 500000.0
MOBA_BLOCK = 256
MOBA_TOPK = 3
M_HEADS = 4
M_HD = 128
N_MEM = 256
EPS = 1e-6

P_AU, P_AV, P_AG, P_BB, P_BC, P_BX, P_BG, P_CQ, P_CK, P_CV, P_CG, P_MQ, P_MG = range(13)

SAMPLE_ROWS = 8
NEG = -1e30
GATE_COLS = 128
VMEM_LIMIT = 56 * 1024 * 1024

F32 = jnp.float32
BF16 = jnp.bfloat16
NT_DIMS = (((1,), (1,)), ((), ()))


def _cparams(sem):
    return pltpu.CompilerParams(dimension_semantics=sem, vmem_limit_bytes=VMEM_LIMIT)


def _rms(x, g):
    r = lax.rsqrt(jnp.mean(x * x, axis=-1, keepdims=True) + EPS)
    return (x * r) * g


def _layernorm(x, g):
    mu = jnp.mean(x, axis=-1, keepdims=True)
    xc = x - mu
    r = lax.rsqrt(jnp.mean(xc * xc, axis=-1, keepdims=True) + EPS)
    return (xc * r) * g


def _sigmoid(x):
    return 1.0 / (1.0 + jnp.exp(-x))


def _silu(x):
    return x * _sigmoid(x)


def _rope(x, tab):
    c, s_lo, s_hi = tab[:, 0:128], tab[:, 128:256], tab[:, 256:384]
    half = ROPE_DIM // 2
    outs = []
    for g in range(BRANCH_W // 128):
        xg = x[:, g * 128:(g + 1) * 128]
        outs.append(xg * c + pltpu.roll(xg, 128 - half, axis=1) * s_lo + pltpu.roll(xg, half, axis=1) * s_hi)
    return jnp.concatenate(outs, axis=1)


def _rope_table(pos):
    half = ROPE_DIM // 2
    inv = jnp.power(jnp.float32(ROPE_THETA), -jnp.arange(half, dtype=F32) * (2.0 / ROPE_DIM))
    ang = pos.astype(F32)[:, None] * inv[None, :]
    cos, sin = jnp.cos(ang), jnp.sin(ang)
    d = jnp.arange(128) % C_HD
    idx = d % half
    cosl = jnp.where(d[None, :] < ROPE_DIM, cos[:, idx], 1.0)
    s_lo = jnp.where(d[None, :] < half, -sin[:, idx], 0.0)
    s_hi = jnp.where((d[None, :] >= half) & (d[None, :] < ROPE_DIM), sin[:, idx], 0.0)
    return jnp.concatenate([cosl, s_lo, s_hi], axis=1).astype(F32)


def _top_blocks(gate, valid, axis):
    idx = lax.broadcasted_iota(jnp.int32, gate.shape, axis).astype(F32)
    g = jnp.where(valid, gate, -jnp.inf)
    sel = jnp.zeros(gate.shape, F32)
    for _ in range(MOBA_TOPK):
        m = jnp.max(g, axis=axis, keepdims=True)
        cand = (g == m) & valid & (sel == 0.0)
        first = jnp.min(jnp.where(cand, idx, float(gate.shape[axis])), axis=axis, keepdims=True)
        pick = idx == first
        sel = jnp.where(pick, 1.0, sel)
        g = jnp.where(pick, -jnp.inf, g)
    return sel


def _inproj_kernel(x_ref, g_ref, w_ref, o_ref, h_ref):
    @pl.when(pl.program_id(1) == 0)
    def _():
        h_ref[...] = _rms(x_ref[...], g_ref[...]).astype(BF16)

    o_ref[...] = jnp.dot(h_ref[...], w_ref[...], preferred_element_type=F32)


def _inproj(x, g, w_bf16, tm, tn):
    n, d = x.shape
    n_out = w_bf16.shape[1]
    return pl.pallas_call(
        _inproj_kernel,
        grid=(n // tm, n_out // tn),
        in_specs=[
            pl.BlockSpec((tm, d), lambda i, j: (i, 0)),
            pl.BlockSpec((1, d), lambda i, j: (0, 0)),
            pl.BlockSpec((d, tn), lambda i, j: (0, j)),
        ],
        out_specs=pl.BlockSpec((tm, tn), lambda i, j: (i, j)),
        out_shape=jax.ShapeDtypeStruct((n, n_out), F32),
        scratch_shapes=[pltpu.VMEM((tm, d), BF16)],
        compiler_params=_cparams(("parallel", "arbitrary")),
        name="inproj",
    )(x, g.reshape(1, d), w_bf16)


KV_TM = 1024


def _kvpost_kernel(k_ref, v_ref, tab_ref, ko_ref, vo_ref, kb_ref, vt_ref, km_ref):
    k = _rope(k_ref[...], tab_ref[...])
    v = v_ref[...]
    ko_ref[...] = k
    vo_ref[...] = v
    kb_ref[...] = k.astype(BF16)
    for i in range(KV_TM // MOBA_BLOCK):
        rows = slice(i * MOBA_BLOCK, (i + 1) * MOBA_BLOCK)
        vt_ref[i] = v[rows, :].T.astype(BF16)
        km_ref[0, i:i + 1, :] = jnp.sum(k[rows, :], axis=0, keepdims=True) * (1.0 / MOBA_BLOCK)


def _kvpost(parts, tab, seq):
    n = parts.shape[0]
    tiles_per_seq = seq // KV_TM
    blocks_per_tile = KV_TM // MOBA_BLOCK
    blk = lambda c: pl.BlockSpec((KV_TM, BRANCH_W), lambda i: (i, c))
    row = pl.BlockSpec((KV_TM, BRANCH_W), lambda i: (i, 0))
    return pl.pallas_call(
        _kvpost_kernel,
        grid=(n // KV_TM,),
        in_specs=[blk(P_CK), blk(P_CV),
                  pl.BlockSpec((KV_TM, 384), lambda i: (i % tiles_per_seq, 0))],
        out_specs=[row, row, row,
                   pl.BlockSpec((blocks_per_tile, BRANCH_W, MOBA_BLOCK), lambda i: (i, 0, 0)),
                   pl.BlockSpec((1, blocks_per_tile, BRANCH_W), lambda i: (i, 0, 0))],
        out_shape=[jax.ShapeDtypeStruct((n, BRANCH_W), F32), jax.ShapeDtypeStruct((n, BRANCH_W), F32),
                   jax.ShapeDtypeStruct((n, BRANCH_W), BF16),
                   jax.ShapeDtypeStruct((n // MOBA_BLOCK, BRANCH_W, MOBA_BLOCK), BF16),
                   jax.ShapeDtypeStruct((n // KV_TM, blocks_per_tile, BRANCH_W), F32)],
        compiler_params=_cparams(("parallel",)),
        name="kvpost",
    )(parts, parts, tab)


MOBA_HEADS_PER_LOOP = 4


def _moba_prompt_kernel(q_ref, tab_ref, kb_ref, vt_ref, km_ref, o_ref, pen_ref, qt_ref, m_ref, l_ref, acc_ref):
    j = pl.program_id(1)
    tq = MOBA_BLOCK
    nb = km_ref.shape[1]
    q = _rope(q_ref[...], tab_ref[...])
    km = km_ref[0]
    past = lax.broadcasted_iota(jnp.int32, (nb, tq), 0) < j
    dim_i = lax.broadcasted_iota(jnp.int32, (128, tq), 0)
    for g in range(C_HEADS // 2):
        lanes = slice(g * 128, (g + 1) * 128)
        qt = q[:, lanes].T
        for hh in range(2):
            h = 2 * g + hh
            qh = jnp.where((dim_i >= hh * C_HD) & (dim_i < (hh + 1) * C_HD), qt, 0.0)
            gate = jnp.dot(km[:, lanes], qh, precision=lax.Precision.HIGHEST, preferred_element_type=F32)
            pen_ref[h] = jnp.where(_top_blocks(gate, past, 0) > 0.0, 0.0, NEG)
            qt_ref[h] = (qh * (C_HD ** -0.5)).astype(BF16)
    m_ref[...] = jnp.full(m_ref.shape, NEG, F32)
    l_ref[...] = jnp.zeros(l_ref.shape, F32)
    acc_ref[...] = jnp.zeros(acc_ref.shape, F32)

    def scores(heads, n, mask_fn):
        r0 = pl.multiple_of(n * MOBA_BLOCK, MOBA_BLOCK)
        out = []
        for h in heads:
            kblk = kb_ref[pl.ds(r0, MOBA_BLOCK), (h // 2) * 128:(h // 2 + 1) * 128]
            out.append(mask_fn(h, jnp.dot(kblk, qt_ref[h], preferred_element_type=F32)))
        return tuple(out)

    def accumulate(heads, n, s_list):
        for h, s in zip(heads, s_list):
            rows = slice(h * C_HD, (h + 1) * C_HD)
            m = m_ref[h:h + 1, :]
            m_new = jnp.maximum(m, jnp.max(s, axis=0, keepdims=True))
            a = jnp.exp(m - m_new)
            p = jnp.exp(s - m_new)
            l_ref[h:h + 1, :] = a * l_ref[h:h + 1, :] + jnp.sum(p, axis=0, keepdims=True)
            pv = jnp.dot(vt_ref[n, rows, :], p.astype(BF16), preferred_element_type=F32)
            acc_ref[rows, :] = a * acc_ref[rows, :] + pv
            m_ref[h:h + 1, :] = m_new

    key_i = lax.broadcasted_iota(jnp.int32, (MOBA_BLOCK, tq), 0)
    qry_i = lax.broadcasted_iota(jnp.int32, (MOBA_BLOCK, tq), 1)
    causal = key_i <= qry_i
    for h0 in range(0, C_HEADS, MOBA_HEADS_PER_LOOP):
        heads = tuple(range(h0, h0 + MOBA_HEADS_PER_LOOP))
        s_own = scores(heads, j, lambda h, s: jnp.where(causal, s, NEG))

        def body(n, s_prev, heads=heads):
            s_next = scores(heads, n, lambda h, s: s + pen_ref[h, pl.ds(n, 1), :])
            accumulate(heads, jnp.where(n == 0, j, n - 1), s_prev)
            return s_next

        s_last = lax.fori_loop(0, j, body, s_own)
        accumulate(heads, jnp.maximum(j - 1, 0), s_last)
    for h in range(C_HEADS):
        rows = slice(h * C_HD, (h + 1) * C_HD)
        acc_ref[rows, :] = acc_ref[rows, :] / l_ref[h:h + 1, :]
    o_ref[...] = acc_ref[...].T


def _moba_prompt(parts, tab, kb, vt, kmean, batch, seq):
    n = parts.shape[0]
    nb = seq // MOBA_BLOCK
    return pl.pallas_call(
        _moba_prompt_kernel,
        grid=(batch, nb),
        in_specs=[
            pl.BlockSpec((MOBA_BLOCK, BRANCH_W), lambda b, j: (b * nb + j, P_CQ)),
            pl.BlockSpec((MOBA_BLOCK, 384), lambda b, j: (j, 0)),
            pl.BlockSpec((seq, BRANCH_W), lambda b, j: (b, 0)),
            pl.BlockSpec((nb, BRANCH_W, MOBA_BLOCK), lambda b, j: (b, 0, 0)),
            pl.BlockSpec((1, nb, BRANCH_W), lambda b, j: (b, 0, 0)),
        ],
        out_specs=pl.BlockSpec((MOBA_BLOCK, BRANCH_W), lambda b, j: (b * nb + j, 0)),
        out_shape=jax.ShapeDtypeStruct((n, BRANCH_W), F32),
        scratch_shapes=[pltpu.VMEM((C_HEADS, nb, MOBA_BLOCK), F32),
                        pltpu.VMEM((C_HEADS, 128, MOBA_BLOCK), BF16),
                        pltpu.VMEM((C_HEADS, MOBA_BLOCK), F32),
                        pltpu.VMEM((C_HEADS, MOBA_BLOCK), F32),
                        pltpu.VMEM((BRANCH_W, MOBA_BLOCK), F32)],
        compiler_params=_cparams(("parallel", "arbitrary")),
        name="moba_prompt",
    )(parts, tab, kb, vt, kmean)


def _tail(x, branches, gpre_ref, gpost_ref, wm_ref, bm_ref, wb_ref, wo_ref):
    h = _rms(x, gpre_ref[...]).astype(BF16)
    acc = None
    for n, br in enumerate(branches):
        cols = slice(n * D_MODEL, (n + 1) * D_MODEL)
        gate = _sigmoid(jnp.dot(h, wm_ref[:, cols], preferred_element_type=F32) + bm_ref[:, cols])
        proj = jnp.dot(br.astype(BF16), wb_ref[n], preferred_element_type=F32)
        acc = gate * proj if acc is None else acc + gate * proj
    y = jnp.dot(acc.astype(BF16), wo_ref[...], preferred_element_type=F32)
    return x + _rms(y, gpost_ref[...])


def _mem_attend(mq, mk_head, mv_head):
    outs = []
    for h in range(M_HEADS):
        lanes = slice(h * M_HD, (h + 1) * M_HD)
        s = lax.dot_general(mq[:, lanes].astype(BF16), mk_head(h).astype(BF16), NT_DIMS,
                            preferred_element_type=F32) * (M_HD ** -0.5)
        m = jnp.max(s, axis=1, keepdims=True)
        p = jnp.exp(s - m)
        l = jnp.sum(p, axis=1, keepdims=True)
        outs.append(jnp.dot(p.astype(BF16), mv_head(h).astype(BF16), preferred_element_type=F32) / l)
    return jnp.concatenate(outs, axis=1)


TAIL_TM = 256


def _prompt_tail_kernel(x_ref, au_ref, av_ref, ag_ref, bb_ref, bc_ref, bx_ref, bg_ref, cg_ref, mq_ref, mg_ref,
                        hc_ref, hx_ref, oc_ref, mk_ref, mv_ref,
                        gpre_ref, gpost_ref, lng_ref, ws_ref, bs_ref, cw_ref, wm_ref, bm_ref, wb_ref, wo_ref,
                        y_ref, zt_ref, *, tiles_per_seq):
    tm = TAIL_TM
    x = x_ref[...]
    vn = _layernorm(av_ref[...], lng_ref[...]).astype(BF16)
    t_i = lax.broadcasted_iota(jnp.int32, (CHUNK, CHUNK), 0)
    s_i = lax.broadcasted_iota(jnp.int32, (CHUNK, CHUNK), 1)
    sp_cols = []
    for g in range(A_GROUPS):
        lanes = slice(g * 128, (g + 1) * 128)
        ws = jnp.where(s_i <= t_i, ws_ref[g], 0.0).astype(BF16)
        bias = bs_ref[:, g:g + 1]
        rows = [jnp.dot(ws, vn[c * CHUNK:(c + 1) * CHUNK, lanes], preferred_element_type=F32) + bias
                for c in range(tm // CHUNK)]
        sp_cols.append(jnp.concatenate(rows, axis=0))
    br_a = _silu(ag_ref[...]) * (au_ref[...] * jnp.concatenate(sp_cols, axis=1))
    z = bc_ref[...] * bx_ref[...]
    first_of_seq = (pl.program_id(0) % tiles_per_seq) == 0
    halo = jnp.where(first_of_seq, 0.0, hc_ref[...] * hx_ref[...])
    rix = lax.broadcasted_iota(jnp.int32, (tm, BRANCH_W), 0)
    z1 = jnp.where(rix == 0, halo[7:8, :], pltpu.roll(z, 1, axis=0))
    z2 = pltpu.roll(z, 2, axis=0)
    z2 = jnp.where(rix == 0, halo[6:7, :], jnp.where(rix == 1, halo[7:8, :], z2))
    conv = cw_ref[0:1, :] * z2 + cw_ref[1:2, :] * z1 + cw_ref[2:3, :] * z
    br_b = _silu(bg_ref[...]) * (bb_ref[...] * conv)
    zt_ref[0] = z[tm - 8:tm, :]
    br_c = _silu(cg_ref[...]) * oc_ref[...]
    out_m = _mem_attend(mq_ref[...], lambda h: mk_ref[:, h * M_HD:(h + 1) * M_HD],
                        lambda h: mv_ref[:, h * M_HD:(h + 1) * M_HD])
    br_m = _silu(mg_ref[...]) * out_m
    y_ref[...] = _tail(x, (br_a, br_b, br_c, br_m), gpre_ref, gpost_ref, wm_ref, bm_ref, wb_ref, wo_ref)


def _const_spec(shape):
    zeros = (0,) * len(shape)
    return pl.BlockSpec(shape, lambda i: zeros, pipeline_mode=pl.Buffered(1))


def _prompt_tail(x, parts, out_c, mkv, lw, seq):
    n = x.shape[0]
    tm = TAIL_TM
    tiles_per_seq = seq // tm
    part = lambda c: pl.BlockSpec((tm, BRANCH_W), lambda i: (i, c))
    halo = lambda c: pl.BlockSpec((8, BRANCH_W), lambda i: (jnp.maximum(i * (tm // 8) - 1, 0), c))
    row512 = pl.BlockSpec((tm, BRANCH_W), lambda i: (i, 0))
    in_specs = [
        pl.BlockSpec((tm, D_MODEL), lambda i: (i, 0)),
        part(P_AU), part(P_AV), part(P_AG), part(P_BB), part(P_BC), part(P_BX), part(P_BG),
        part(P_CG), part(P_MQ), part(P_MG),
        halo(P_BC), halo(P_BX),
        row512,
        pl.BlockSpec((N_MEM, BRANCH_W), lambda i: (i // tiles_per_seq, 0)),
        pl.BlockSpec((N_MEM, BRANCH_W), lambda i: (i // tiles_per_seq, 1)),
        _const_spec((1, D_MODEL)), _const_spec((1, D_MODEL)), _const_spec((1, BRANCH_W)),
        _const_spec((A_GROUPS, CHUNK, CHUNK)), _const_spec((CHUNK, A_GROUPS)), _const_spec((CONV_W, BRANCH_W)),
        _const_spec((D_MODEL, 4 * D_MODEL)), _const_spec((1, 4 * D_MODEL)),
        _const_spec((4, BRANCH_W, D_MODEL)), _const_spec((D_MODEL, D_MODEL)),
    ]
    return pl.pallas_call(
        functools.partial(_prompt_tail_kernel, tiles_per_seq=tiles_per_seq),
        grid=(n // tm,),
        in_specs=in_specs,
        out_specs=[pl.BlockSpec((tm, D_MODEL), lambda i: (i, 0)),
                   pl.BlockSpec((1, 8, BRANCH_W), lambda i: (i, 0, 0))],
        out_shape=[jax.ShapeDtypeStruct((n, D_MODEL), F32),
                   jax.ShapeDtypeStruct((n // tm, 8, BRANCH_W), F32)],
        compiler_params=_cparams(("parallel",)),
        name="prompt_tail",
    )(x, *([parts] * 10), parts, parts, out_c, mkv, mkv,
      lw["g_pre"], lw["g_post"], lw["ln_v_gain"], lw["w_spatial"], lw["b_spatial_t"], lw["conv_w"],
      lw["w_merge"], lw["b_merge"], lw["w_branch"], lw["w_out"])


N_PAGES = 16
PAGE = 128
PAGES_PER_BLOCK = MOBA_BLOCK // PAGE


def _sample_branch_kernel(pt_ref, parts_ref, tab_ref, conv_ref, mk_ref, mv_ref, *rest, dec_seq):
    k_pages = rest[0:N_PAGES]
    v_pages = rest[N_PAGES:2 * N_PAGES]
    lng_ref, wsx_ref, bsx_ref, cw_ref = rest[2 * N_PAGES:2 * N_PAGES + 4]
    ko_ref, vo_ref, co_ref, vn_ref, br_ref = rest[2 * N_PAGES + 4:]
    del pt_ref
    r8 = SAMPLE_ROWS
    part = lambda c: parts_ref[0, :, c * BRANCH_W:(c + 1) * BRANCH_W]
    rix = lax.broadcasted_iota(jnp.int32, (r8, BRANCH_W), 0)

    vn = _layernorm(part(P_AV), lng_ref[...])
    vn_ref[0] = vn
    sp = bsx_ref[...]
    for s in range(dec_seq):
        sp = sp + jnp.where(rix >= s, wsx_ref[s], 0.0) * vn[s:s + 1, :]
    br_ref[0, :, 0:BRANCH_W] = _silu(part(P_AG)) * (part(P_AU) * sp)

    z = part(P_BC) * part(P_BX)
    prev = conv_ref[0, 0]
    zrow = lambda i: prev[i:i + 1, :] if i < CONV_W - 1 else z[i - (CONV_W - 1):i - (CONV_W - 2), :]
    conv = jnp.zeros((r8, BRANCH_W), F32)
    for t in range(dec_seq):
        c_t = cw_ref[0:1, :] * zrow(t) + cw_ref[1:2, :] * zrow(t + 1) + cw_ref[2:3, :] * zrow(t + 2)
        conv = jnp.where(rix == t, c_t, conv)
    br_ref[0, :, BRANCH_W:2 * BRANCH_W] = _silu(part(P_BG)) * (part(P_BB) * conv)
    co_ref[0, 0] = z[dec_seq - (CONV_W - 1):dec_seq, :]

    tab = tab_ref[...]
    q = _rope(part(P_CQ), tab)
    k = _rope(part(P_CK), tab)
    v = part(P_CV)
    ko_ref[0] = k
    vo_ref[0] = v
    n_past = N_PAGES // PAGES_PER_BLOCK
    flat = MOBA_BLOCK * C_HEADS
    scale = C_HD ** -0.5
    hr = C_HEADS * r8
    head_rows = lambda x: jnp.concatenate([x[:, h * C_HD:(h + 1) * C_HD] for h in range(C_HEADS)], axis=0)
    q_rows = head_rows(q)
    q_rows_b = (q_rows * scale).astype(BF16)
    row_head = lambda width: lax.broadcasted_iota(jnp.int32, (hr, width), 0) // r8
    col_head = lambda width: lax.broadcasted_iota(jnp.int32, (hr, width), 1) % C_HEADS
    t_q = lax.broadcasted_iota(jnp.int32, (hr, 1), 0) % r8

    def block_rows(pages, n):
        return jnp.concatenate([pages[n * PAGES_PER_BLOCK + i][0, 0] for i in range(PAGES_PER_BLOCK)], axis=0)

    m = jnp.full((hr, 1), NEG, F32)
    s_own = []
    for c in range(dec_seq):
        k_c = jnp.concatenate([jnp.broadcast_to(k[c:c + 1, h * C_HD:(h + 1) * C_HD], (r8, C_HD))
                               for h in range(C_HEADS)], axis=0)
        s_c = jnp.where(t_q >= c, jnp.sum((q_rows * scale) * k_c, axis=1, keepdims=True), NEG)
        s_own.append(s_c)
        m = jnp.maximum(m, s_c)
    l = jnp.zeros((hr, 1), F32)
    acc = jnp.zeros((hr, C_HD), F32)
    for c in range(dec_seq):
        v_c = jnp.concatenate([jnp.broadcast_to(v[c:c + 1, h * C_HD:(h + 1) * C_HD], (r8, C_HD))
                               for h in range(C_HEADS)], axis=0)
        p = jnp.exp(s_own[c] - m)
        l = l + p
        acc = acc + p * v_c

    k_flat = [block_rows(k_pages, n) for n in range(n_past)]
    kmean = jnp.concatenate(
        [jnp.sum(kf.reshape(MOBA_BLOCK, C_HEADS, C_HD), axis=0) * (1.0 / MOBA_BLOCK) for kf in k_flat]
        + [jnp.zeros((GATE_COLS - n_past * C_HEADS, C_HD), F32)], axis=0)
    gate = lax.dot_general(q_rows, kmean, NT_DIMS, precision=lax.Precision.HIGHEST,
                           preferred_element_type=F32)
    gate_blk = lax.broadcasted_iota(jnp.int32, (hr, GATE_COLS), 1) // C_HEADS
    same_head = row_head(GATE_COLS) == col_head(GATE_COLS)
    sel = _top_blocks(gate, same_head & (gate_blk < n_past), 1)
    same_head_flat = row_head(flat) == col_head(flat)
    for n in range(n_past):
        s = lax.dot_general(q_rows_b, k_flat[n].astype(BF16), NT_DIMS, preferred_element_type=F32)
        sel_n = jnp.sum(jnp.where(gate_blk == n, sel, 0.0), axis=1, keepdims=True) > 0.0
        s = jnp.where(same_head_flat & sel_n, s, NEG)
        m_new = jnp.maximum(m, jnp.max(s, axis=1, keepdims=True))
        a = jnp.exp(m - m_new)
        p = jnp.exp(s - m_new)
        l = a * l + jnp.sum(p, axis=1, keepdims=True)
        acc = a * acc + jnp.dot(p.astype(BF16), block_rows(v_pages, n).astype(BF16), preferred_element_type=F32)
        m = m_new
    acc = acc / l
    out_c = jnp.concatenate([acc[h * r8:(h + 1) * r8, :] for h in range(C_HEADS)], axis=1)
    br_ref[0, :, 2 * BRANCH_W:3 * BRANCH_W] = _silu(part(P_CG)) * out_c

    out_m = _mem_attend(part(P_MQ), lambda h: mk_ref[0, 0, pl.ds(h, N_MEM, stride=M_HEADS), :],
                        lambda h: mv_ref[0, 0, pl.ds(h, N_MEM, stride=M_HEADS), :])
    br_ref[0, :, 3 * BRANCH_W:4 * BRANCH_W] = _silu(part(P_MG)) * out_m


def _sample_branch(layer, parts3, tab, state_conv, cache_k, cache_v, cache_mem_k, cache_mem_v, page_table, lw,
                   dec_seq):
    bs = parts3.shape[0]
    r8 = SAMPLE_ROWS
    page_spec = lambda p: pl.BlockSpec((1, 1, PAGE * C_HEADS, C_HD), lambda b, pt: (layer, pt[b, p], 0, 0))
    const = lambda shape: pl.BlockSpec(shape, lambda b, pt: (0,) * len(shape))
    seq_spec = lambda rows, w: pl.BlockSpec((1, rows, w), lambda b, pt: (b, 0, 0))
    in_specs = [
        seq_spec(r8, IN_W),
        const((r8, 384)),
        pl.BlockSpec((1, 1, CONV_W - 1, BRANCH_W), lambda b, pt: (layer, b, 0, 0)),
        pl.BlockSpec((1, 1, N_MEM * M_HEADS, M_HD), lambda b, pt: (layer, b, 0, 0)),
        pl.BlockSpec((1, 1, N_MEM * M_HEADS, M_HD), lambda b, pt: (layer, b, 0, 0)),
    ] + [page_spec(p) for p in range(N_PAGES)] + [page_spec(p) for p in range(N_PAGES)] + [
        const((1, BRANCH_W)), const((dec_seq, r8, BRANCH_W)), const((r8, BRANCH_W)), const((CONV_W, BRANCH_W)),
    ]
    out_specs = [seq_spec(r8, BRANCH_W), seq_spec(r8, BRANCH_W),
                 pl.BlockSpec((1, 1, CONV_W - 1, BRANCH_W), lambda b, pt: (b, 0, 0, 0)),
                 seq_spec(r8, BRANCH_W), seq_spec(r8, 4 * BRANCH_W)]
    out_shape = [jax.ShapeDtypeStruct((bs, r8, BRANCH_W), F32), jax.ShapeDtypeStruct((bs, r8, BRANCH_W), F32),
                 jax.ShapeDtypeStruct((bs, 1, CONV_W - 1, BRANCH_W), F32),
                 jax.ShapeDtypeStruct((bs, r8, BRANCH_W), F32), jax.ShapeDtypeStruct((bs, r8, 4 * BRANCH_W), F32)]
    grid_spec = pltpu.PrefetchScalarGridSpec(
        num_scalar_prefetch=1, grid=(bs,), in_specs=in_specs, out_specs=out_specs)
    return pl.pallas_call(
        functools.partial(_sample_branch_kernel, dec_seq=dec_seq),
        grid_spec=grid_spec,
        out_shape=out_shape,
        compiler_params=_cparams(("arbitrary",)),
        name="sample_branch",
    )(page_table, parts3, tab, state_conv, cache_mem_k, cache_mem_v,
      *([cache_k] * N_PAGES), *([cache_v] * N_PAGES),
      lw["ln_v_gain"], lw["w_spatial_x"], lw["b_spatial_x"], lw["conv_w"])


def _sample_tail_kernel(x_ref, br_ref, gpre_ref, gpost_ref, wm_ref, bm_ref, wb_ref, wo_ref, y_ref):
    branches = tuple(br_ref[:, n * BRANCH_W:(n + 1) * BRANCH_W] for n in range(4))
    y_ref[...] = _tail(x_ref[...], branches, gpre_ref, gpost_ref, wm_ref, bm_ref, wb_ref, wo_ref)


def _sample_tail(x, branches, lw):
    n = x.shape[0]
    tm = TAIL_TM
    return pl.pallas_call(
        _sample_tail_kernel,
        grid=(n // tm,),
        in_specs=[
            pl.BlockSpec((tm, D_MODEL), lambda i: (i, 0)),
            pl.BlockSpec((tm, 4 * BRANCH_W), lambda i: (i, 0)),
            _const_spec((1, D_MODEL)), _const_spec((1, D_MODEL)),
            _const_spec((D_MODEL, 4 * D_MODEL)), _const_spec((1, 4 * D_MODEL)),
            _const_spec((4, BRANCH_W, D_MODEL)), _const_spec((D_MODEL, D_MODEL)),
        ],
        out_specs=pl.BlockSpec((tm, D_MODEL), lambda i: (i, 0)),
        out_shape=jax.ShapeDtypeStruct((n, D_MODEL), F32),
        compiler_params=_cparams(("parallel",)),
        name="sample_tail",
    )(x, branches, lw["g_pre"], lw["g_post"], lw["w_merge"], lw["b_merge"], lw["w_branch"], lw["w_out"])


def kernel(x_prompt, x_sample, cache_k, cache_v, cache_mem_k, cache_mem_v, state_conv, page_table, mem_prompt,
           g_pre, g_post, w_in, ln_v_gain, w_spatial, b_spatial, conv_w, g_mem, w_mem_kv, w_merge, b_merge,
           w_branch, w_out):
    bp, tp, d = x_prompt.shape
    bs, ts, _ = x_sample.shape
    depth = w_in.shape[0]
    page = cache_k.shape[2]
    past_len = page_table.shape[1] * page
    assert (d, page, page_table.shape[1]) == (D_MODEL, PAGE, N_PAGES) and ts <= SAMPLE_ROWS
    assert past_len % MOBA_BLOCK == 0 and tp % KV_TM == 0 and ts >= CONV_W - 1
    r8 = SAMPLE_ROWS

    tab_p = _rope_table(jnp.arange(tp, dtype=jnp.int32))
    tab_s = _rope_table(past_len + jnp.arange(r8, dtype=jnp.int32))
    xp = x_prompt.reshape(bp * tp, d)
    xs = jnp.pad(x_sample, ((0, 0), (0, r8 - ts), (0, 0))).reshape(bs * r8, d)
    mem = mem_prompt.reshape(bp * N_MEM, d)
    n_pool = cache_k.shape[1]
    ck = cache_k.reshape(depth, n_pool, page * C_HEADS, C_HD)
    cv = cache_v.reshape(depth, n_pool, page * C_HEADS, C_HD)
    cmk = cache_mem_k.reshape(depth, bs, N_MEM * M_HEADS, M_HD)
    cmv = cache_mem_v.reshape(depth, bs, N_MEM * M_HEADS, M_HD)
    lane_group = jnp.arange(BRANCH_W) // (BRANCH_W // A_GROUPS)

    outs = {name: [] for name in ("kp", "vp", "cp", "mkp", "mvp", "ks", "vs", "cs", "vns")}
    for l in range(depth):
        wsx = jnp.transpose(w_spatial[l][:, :r8, :ts], (2, 1, 0))[:, :, lane_group]
        bsx = jnp.transpose(b_spatial[l][:, :r8], (1, 0))[:, lane_group]
        lw = {
            "g_pre": g_pre[l].reshape(1, d), "g_post": g_post[l].reshape(1, d),
            "ln_v_gain": ln_v_gain[l].reshape(1, BRANCH_W),
            "w_spatial": w_spatial[l], "b_spatial_t": b_spatial[l].T,
            "w_spatial_x": wsx, "b_spatial_x": bsx,
            "conv_w": conv_w[l],
            "w_merge": w_merge[l].astype(BF16), "b_merge": b_merge[l].reshape(1, 4 * d),
            "w_branch": w_branch[l].astype(BF16), "w_out": w_out[l].astype(BF16),
        }
        w_in_b = w_in[l].astype(BF16)

        mkv = _inproj(mem, g_mem[l], w_mem_kv[l].astype(BF16), 1024, 1024)
        parts = _inproj(xp, g_pre[l], w_in_b, 1024, 1664)
        k_p, v_p, kb, vb, kmean = _kvpost(parts, tab_p, tp)
        kmean = kmean.reshape(bp, tp // MOBA_BLOCK, BRANCH_W)
        out_c = _moba_prompt(parts, tab_p, kb, vb, kmean, bp, tp)
        xp, ztail = _prompt_tail(xp, parts, out_c, mkv, lw, tp)
        outs["kp"].append(k_p.reshape(bp, tp, C_HEADS, C_HD))
        outs["vp"].append(v_p.reshape(bp, tp, C_HEADS, C_HD))
        outs["cp"].append(ztail.reshape(bp, tp // TAIL_TM, 8, BRANCH_W)[:, -1, 8 - (CONV_W - 1):, :])
        outs["mkp"].append(mkv[:, :BRANCH_W].reshape(bp, N_MEM, M_HEADS, M_HD))
        outs["mvp"].append(mkv[:, BRANCH_W:].reshape(bp, N_MEM, M_HEADS, M_HD))

        parts_s = _inproj(xs, g_pre[l], w_in_b, bs * r8, 1664).reshape(bs, r8, IN_W)
        k_s, v_s, c_s, vn_s, br_s = _sample_branch(l, parts_s, tab_s, state_conv, ck, cv, cmk, cmv, page_table,
                                                   lw, ts)
        xs = _sample_tail(xs, br_s.reshape(bs * r8, 4 * BRANCH_W), lw)
        outs["ks"].append(k_s[:, :ts].reshape(bs, ts, C_HEADS, C_HD))
        outs["vs"].append(v_s[:, :ts].reshape(bs, ts, C_HEADS, C_HD))
        outs["cs"].append(c_s.reshape(bs, CONV_W - 1, BRANCH_W))
        outs["vns"].append(vn_s[:, :ts])

    st = lambda name: jnp.stack(outs[name])
    return (xp.reshape(bp, tp, d), xs.reshape(bs, r8, d)[:, :ts],
            st("kp"), st("vp"), st("cp"), st("mkp"), st("mvp"),
            st("ks"), st("vs"), st("cs"), st("vns"))
```

```python
import functools

import jax
import jax.numpy as jnp
from jax import lax
from jax.experimental import pallas as pl
from jax.experimental.pallas import tpu as pltpu

D_MODEL = 1024
BRANCH_W = 512
N_PARTS = 13
IN_W = N_PARTS * BRANCH_W
CHUNK = 128
A_GROUPS = 4
CONV_W = 3
C_HEADS = 8
C_HD = 64
ROPE_DIM = 16
ROPE_THETA = 500000.0
MOBA_BLOCK = 256
MOBA_TOPK = 3
M_HEADS = 4
M_HD = 128
N_MEM = 256
EPS = 1e-6

P_AU, P_AV, P_AG, P_BB, P_BC, P_BX, P_BG, P_CQ, P_CK, P_CV, P_CG, P_MQ, P_MG = range(13)

SAMPLE_ROWS = 8
NEG = -1e30
GATE_COLS = 128
VMEM_LIMIT = 56 * 1024 * 1024

F32 = jnp.float32
BF16 = jnp.bfloat16
NT_DIMS = (((1,), (1,)), ((), ()))


def _cparams(sem):
    return pltpu.CompilerParams(dimension_semantics=sem, vmem_limit_bytes=VMEM_LIMIT)


def _rms(x, g):
    r = lax.rsqrt(jnp.mean(x * x, axis=-1, keepdims=True) + EPS)
    return (x * r) * g


def _layernorm(x, g):
    mu = jnp.mean(x, axis=-1, keepdims=True)
    xc = x - mu
    r = lax.rsqrt(jnp.mean(xc * xc, axis=-1, keepdims=True) + EPS)
    return (xc * r) * g


def _sigmoid(x):
    return 1.0 / (1.0 + jnp.exp(-x))


def _silu(x):
    return x * _sigmoid(x)


def _rope(x, tab):
    c, s_lo, s_hi = tab[:, 0:128], tab[:, 128:256], tab[:, 256:384]
    half = ROPE_DIM // 2
    outs = []
    for g in range(BRANCH_W // 128):
        xg = x[:, g * 128:(g + 1) * 128]
        outs.append(xg * c + pltpu.roll(xg, 128 - half, axis=1) * s_lo + pltpu.roll(xg, half, axis=1) * s_hi)
    return jnp.concatenate(outs, axis=1)


def _rope_table(pos):
    half = ROPE_DIM // 2
    inv = jnp.power(jnp.float32(ROPE_THETA), -jnp.arange(half, dtype=F32) * (2.0 / ROPE_DIM))
    ang = pos.astype(F32)[:, None] * inv[None, :]
    cos, sin = jnp.cos(ang), jnp.sin(ang)
    d = jnp.arange(128) % C_HD
    idx = d % half
    cosl = jnp.where(d[None, :] < ROPE_DIM, cos[:, idx], 1.0)
    s_lo = jnp.where(d[None, :] < half, -sin[:, idx], 0.0)
    s_hi = jnp.where((d[None, :] >= half) & (d[None, :] < ROPE_DIM), sin[:, idx], 0.0)
    return jnp.concatenate([cosl, s_lo, s_hi], axis=1).astype(F32)


def _top_blocks(gate, valid, axis):
    idx = lax.broadcasted_iota(jnp.int32, gate.shape, axis).astype(F32)
    g = jnp.where(valid, gate, -jnp.inf)
    sel = jnp.zeros(gate.shape, F32)
    for _ in range(MOBA_TOPK):
        m = jnp.max(g, axis=axis, keepdims=True)
        cand = (g == m) & valid & (sel == 0.0)
        first = jnp.min(jnp.where(cand, idx, float(gate.shape[axis])), axis=axis, keepdims=True)
        pick = idx == first
        sel = jnp.where(pick, 1.0, sel)
        g = jnp.where(pick, -jnp.inf, g)
    return sel


def _inproj_kernel(x_ref, g_ref, w_ref, o_ref, h_ref):
    @pl.when(pl.program_id(1) == 0)
    def _():
        h_ref[...] = _rms(x_ref[...], g_ref[...]).astype(BF16)

    o_ref[...] = jnp.dot(h_ref[...], w_ref[...], preferred_element_type=F32)


def _inproj(x, g, w_bf16, tm, tn):
    n, d = x.shape
    n_out = w_bf16.shape[1]
    return pl.pallas_call(
        _inproj_kernel,
        grid=(n // tm, n_out // tn),
        in_specs=[
            pl.BlockSpec((tm, d), lambda i, j: (i, 0)),
            pl.BlockSpec((1, d), lambda i, j: (0, 0)),
            pl.BlockSpec((d, tn), lambda i, j: (0, j)),
        ],
        out_specs=pl.BlockSpec((tm, tn), lambda i, j: (i, j)),
        out_shape=jax.ShapeDtypeStruct((n, n_out), F32),
        scratch_shapes=[pltpu.VMEM((tm, d), BF16)],
        compiler_params=_cparams(("parallel", "arbitrary")),
        name="inproj",
    )(x, g.reshape(1, d), w_bf16)


KV_TM = 1024


def _kvpost_kernel(k_ref, v_ref, tab_ref, kt_ref, vt_ref, kb_ref, vtb_ref, km_ref):
    k = _rope(k_ref[...], tab_ref[...])
    k_t = k.T
    v_t = v_ref[...].T
    kt_ref[0] = k_t.reshape(C_HEADS, C_HD, KV_TM)
    vt_ref[0] = v_t.reshape(C_HEADS, C_HD, KV_TM)
    kb_ref[...] = k.astype(BF16)
    for i in range(KV_TM // MOBA_BLOCK):
        rows = slice(i * MOBA_BLOCK, (i + 1) * MOBA_BLOCK)
        vtb_ref[i] = v_t[:, rows].astype(BF16)
        km_ref[0, i:i + 1, :] = jnp.sum(k[rows, :], axis=0, keepdims=True) * (1.0 / MOBA_BLOCK)


def _kvpost(parts, tab, seq):
    n = parts.shape[0]
    tiles_per_seq = seq // KV_TM
    blocks_per_tile = KV_TM // MOBA_BLOCK
    blk = lambda c: pl.BlockSpec((KV_TM, BRANCH_W), lambda i: (i, c))
    row = pl.BlockSpec((KV_TM, BRANCH_W), lambda i: (i, 0))
    slab = pl.BlockSpec((1, C_HEADS, C_HD, KV_TM), lambda i: (i // tiles_per_seq, 0, 0, i % tiles_per_seq))
    return pl.pallas_call(
        _kvpost_kernel,
        grid=(n // KV_TM,),
        in_specs=[blk(P_CK), blk(P_CV),
                  pl.BlockSpec((KV_TM, 384), lambda i: (i % tiles_per_seq, 0))],
        out_specs=[slab, slab, row,
                   pl.BlockSpec((blocks_per_tile, BRANCH_W, MOBA_BLOCK), lambda i: (i, 0, 0)),
                   pl.BlockSpec((1, blocks_per_tile, BRANCH_W), lambda i: (i, 0, 0))],
        out_shape=[jax.ShapeDtypeStruct((n // seq, C_HEADS, C_HD, seq), F32),
                   jax.ShapeDtypeStruct((n // seq, C_HEADS, C_HD, seq), F32),
                   jax.ShapeDtypeStruct((n, BRANCH_W), BF16),
                   jax.ShapeDtypeStruct((n // MOBA_BLOCK, BRANCH_W, MOBA_BLOCK), BF16),
                   jax.ShapeDtypeStruct((n // KV_TM, blocks_per_tile, BRANCH_W), F32)],
        compiler_params=_cparams(("parallel",)),
        name="kvpost",
    )(parts, parts, tab)


MOBA_HEADS_PER_LOOP = 4


def _moba_prompt_kernel(q_ref, tab_ref, kb_ref, vt_ref, km_ref, o_ref, pen_ref, qt_ref, m_ref, l_ref, acc_ref):
    j = pl.program_id(1)
    tq = MOBA_BLOCK
    nb = km_ref.shape[1]
    q = _rope(q_ref[...], tab_ref[...])
    km = km_ref[0]
    past = lax.broadcasted_iota(jnp.int32, (nb, tq), 0) < j
    dim_i = lax.broadcasted_iota(jnp.int32, (128, tq), 0)
    for g in range(C_HEADS // 2):
        lanes = slice(g * 128, (g + 1) * 128)
        qt = q[:, lanes].T
        for hh in range(2):
            h = 2 * g + hh
            qh = jnp.where((dim_i >= hh * C_HD) & (dim_i < (hh + 1) * C_HD), qt, 0.0)
            gate = jnp.dot(km[:, lanes], qh, precision=lax.Precision.HIGHEST, preferred_element_type=F32)
            pen_ref[h] = jnp.where(_top_blocks(gate, past, 0) > 0.0, 0.0, NEG)
            qt_ref[h] = (qh * (C_HD ** -0.5)).astype(BF16)
    m_ref[...] = jnp.full(m_ref.shape, NEG, F32)
    l_ref[...] = jnp.zeros(l_ref.shape, F32)
    acc_ref[...] = jnp.zeros(acc_ref.shape, F32)

    def scores(heads, n, mask_fn):
        r0 = pl.multiple_of(n * MOBA_BLOCK, MOBA_BLOCK)
        out = []
        for h in heads:
            kblk = kb_ref[pl.ds(r0, MOBA_BLOCK), (h // 2) * 128:(h // 2 + 1) * 128]
            out.append(mask_fn(h, jnp.dot(kblk, qt_ref[h], preferred_element_type=F32)))
        return tuple(out)

    def accumulate(heads, n, s_list):
        for h, s in zip(heads, s_list):
            rows = slice(h * C_HD, (h + 1) * C_HD)
            m = m_ref[h:h + 1, :]
            m_new = jnp.maximum(m, jnp.max(s, axis=0, keepdims=True))
            a = jnp.exp(m - m_new)
            p = jnp.exp(s - m_new)
            l_ref[h:h + 1, :] = a * l_ref[h:h + 1, :] + jnp.sum(p, axis=0, keepdims=True)
            pv = jnp.dot(vt_ref[n, rows, :], p.astype(BF16), preferred_element_type=F32)
            acc_ref[rows, :] = a * acc_ref[rows, :] + pv
            m_ref[h:h + 1, :] = m_new

    key_i = lax.broadcasted_iota(jnp.int32, (MOBA_BLOCK, tq), 0)
    qry_i = lax.broadcasted_iota(jnp.int32, (MOBA_BLOCK, tq), 1)
    causal = key_i <= qry_i
    for h0 in range(0, C_HEADS, MOBA_HEADS_PER_LOOP):
        heads = tuple(range(h0, h0 + MOBA_HEADS_PER_LOOP))
        s_own = scores(heads, j, lambda h, s: jnp.where(causal, s, NEG))

        def body(n, s_prev, heads=heads):
            s_next = scores(heads, n, lambda h, s: s + pen_ref[h, pl.ds(n, 1), :])
            accumulate(heads, jnp.where(n == 0, j, n - 1), s_prev)
            return s_next

        s_last = lax.fori_loop(0, j, body, s_own)
        accumulate(heads, jnp.maximum(j - 1, 0), s_last)
    for h in range(C_HEADS):
        rows = slice(h * C_HD, (h + 1) * C_HD)
        acc_ref[rows, :] = acc_ref[rows, :] / l_ref[h:h + 1, :]
    o_ref[...] = acc_ref[...].T


def _moba_prompt(parts, tab, kb, vt, kmean, batch, seq):
    n = parts.shape[0]
    nb = seq // MOBA_BLOCK
    return pl.pallas_call(
        _moba_prompt_kernel,
        grid=(batch, nb),
        in_specs=[
            pl.BlockSpec((MOBA_BLOCK, BRANCH_W), lambda b, j: (b * nb + j, P_CQ)),
            pl.BlockSpec((MOBA_BLOCK, 384), lambda b, j: (j, 0)),
            pl.BlockSpec((seq, BRANCH_W), lambda b, j: (b, 0)),
            pl.BlockSpec((nb, BRANCH_W, MOBA_BLOCK), lambda b, j: (b, 0, 0)),
            pl.BlockSpec((1, nb, BRANCH_W), lambda b, j: (b, 0, 0)),
        ],
        out_specs=pl.BlockSpec((MOBA_BLOCK, BRANCH_W), lambda b, j: (b * nb + j, 0)),
        out_shape=jax.ShapeDtypeStruct((n, BRANCH_W), F32),
        scratch_shapes=[pltpu.VMEM((C_HEADS, nb, MOBA_BLOCK), F32),
                        pltpu.VMEM((C_HEADS, 128, MOBA_BLOCK), BF16),
                        pltpu.VMEM((C_HEADS, MOBA_BLOCK), F32),
                        pltpu.VMEM((C_HEADS, MOBA_BLOCK), F32),
                        pltpu.VMEM((BRANCH_W, MOBA_BLOCK), F32)],
        compiler_params=_cparams(("parallel", "arbitrary")),
        name="moba_prompt",
    )(parts, tab, kb, vt, kmean)


def _tail(x, branches, gpre_ref, gpost_ref, wm_ref, bm_ref, wb_ref, wo_ref):
    h = _rms(x, gpre_ref[...]).astype(BF16)
    acc = None
    for n, br in enumerate(branches):
        cols = slice(n * D_MODEL, (n + 1) * D_MODEL)
        gate = _sigmoid(jnp.dot(h, wm_ref[:, cols], preferred_element_type=F32) + bm_ref[:, cols])
        proj = jnp.dot(br.astype(BF16), wb_ref[n], preferred_element_type=F32)
        acc = gate * proj if acc is None else acc + gate * proj
    y = jnp.dot(acc.astype(BF16), wo_ref[...], preferred_element_type=F32)
    return x + _rms(y, gpost_ref[...])


def _mem_attend(mq, mk_head, mv_head):
    outs = []
    for h in range(M_HEADS):
        lanes = slice(h * M_HD, (h + 1) * M_HD)
        s = lax.dot_general(mq[:, lanes].astype(BF16), mk_head(h).astype(BF16), NT_DIMS,
                            preferred_element_type=F32) * (M_HD ** -0.5)
        m = jnp.max(s, axis=1, keepdims=True)
        p = jnp.exp(s - m)
        l = jnp.sum(p, axis=1, keepdims=True)
        outs.append(jnp.dot(p.astype(BF16), mv_head(h).astype(BF16), preferred_element_type=F32) / l)
    return jnp.concatenate(outs, axis=1)


TAIL_TM = 256


def _prompt_tail_kernel(x_ref, au_ref, av_ref, ag_ref, bb_ref, bc_ref, bx_ref, bg_ref, cg_ref, mq_ref, mg_ref,
                        hc_ref, hx_ref, oc_ref, mk_ref, mv_ref,
                        gpre_ref, gpost_ref, lng_ref, ws_ref, bs_ref, cw_ref, wm_ref, bm_ref, wb_ref, wo_ref,
                        y_ref, zt_ref, *, tiles_per_seq):
    tm = TAIL_TM
    x = x_ref[...]
    vn = _layernorm(av_ref[...], lng_ref[...]).astype(BF16)
    t_i = lax.broadcasted_iota(jnp.int32, (CHUNK, CHUNK), 0)
    s_i = lax.broadcasted_iota(jnp.int32, (CHUNK, CHUNK), 1)
    sp_cols = []
    for g in range(A_GROUPS):
        lanes = slice(g * 128, (g + 1) * 128)
        ws = jnp.where(s_i <= t_i, ws_ref[g], 0.0).astype(BF16)
        bias = bs_ref[:, g:g + 1]
        rows = [jnp.dot(ws, vn[c * CHUNK:(c + 1) * CHUNK, lanes], preferred_element_type=F32) + bias
                for c in range(tm // CHUNK)]
        sp_cols.append(jnp.concatenate(rows, axis=0))
    br_a = _silu(ag_ref[...]) * (au_ref[...] * jnp.concatenate(sp_cols, axis=1))
    z = bc_ref[...] * bx_ref[...]
    first_of_seq = (pl.program_id(0) % tiles_per_seq) == 0
    halo = jnp.where(first_of_seq, 0.0, hc_ref[...] * hx_ref[...])
    rix = lax.broadcasted_iota(jnp.int32, (tm, BRANCH_W), 0)
    z1 = jnp.where(rix == 0, halo[7:8, :], pltpu.roll(z, 1, axis=0))
    z2 = pltpu.roll(z, 2, axis=0)
    z2 = jnp.where(rix == 0, halo[6:7, :], jnp.where(rix == 1, halo[7:8, :], z2))
    conv = cw_ref[0:1, :] * z2 + cw_ref[1:2, :] * z1 + cw_ref[2:3, :] * z
    br_b = _silu(bg_ref[...]) * (bb_ref[...] * conv)
    zt_ref[0] = z[tm - 8:tm, :]
    br_c = _silu(cg_ref[...]) * oc_ref[...]
    out_m = _mem_attend(mq_ref[...], lambda h: mk_ref[:, h * M_HD:(h + 1) * M_HD],
                        lambda h: mv_ref[:, h * M_HD:(h + 1) * M_HD])
    br_m = _silu(mg_ref[...]) * out_m
    y_ref[...] = _tail(x, (br_a, br_b, br_c, br_m), gpre_ref, gpost_ref, wm_ref, bm_ref, wb_ref, wo_ref)


def _const_spec(shape):
    zeros = (0,) * len(shape)
    return pl.BlockSpec(shape, lambda i: zeros, pipeline_mode=pl.Buffered(1))


def _prompt_tail(x, parts, out_c, mkv, lw, seq):
    n = x.shape[0]
    tm = TAIL_TM
    tiles_per_seq = seq // tm
    part = lambda c: pl.BlockSpec((tm, BRANCH_W), lambda i: (i, c))
    halo = lambda c: pl.BlockSpec((8, BRANCH_W), lambda i: (jnp.maximum(i * (tm // 8) - 1, 0), c))
    row512 = pl.BlockSpec((tm, BRANCH_W), lambda i: (i, 0))
    in_specs = [
        pl.BlockSpec((tm, D_MODEL), lambda i: (i, 0)),
        part(P_AU), part(P_AV), part(P_AG), part(P_BB), part(P_BC), part(P_BX), part(P_BG),
        part(P_CG), part(P_MQ), part(P_MG),
        halo(P_BC), halo(P_BX),
        row512,
        pl.BlockSpec((N_MEM, BRANCH_W), lambda i: (i // tiles_per_seq, 0)),
        pl.BlockSpec((N_MEM, BRANCH_W), lambda i: (i // tiles_per_seq, 1)),
        _const_spec((1, D_MODEL)), _const_spec((1, D_MODEL)), _const_spec((1, BRANCH_W)),
        _const_spec((A_GROUPS, CHUNK, CHUNK)), _const_spec((CHUNK, A_GROUPS)), _const_spec((CONV_W, BRANCH_W)),
        _const_spec((D_MODEL, 4 * D_MODEL)), _const_spec((1, 4 * D_MODEL)),
        _const_spec((4, BRANCH_W, D_MODEL)), _const_spec((D_MODEL, D_MODEL)),
    ]
    return pl.pallas_call(
        functools.partial(_prompt_tail_kernel, tiles_per_seq=tiles_per_seq),
        grid=(n // tm,),
        in_specs=in_specs,
        out_specs=[pl.BlockSpec((tm, D_MODEL), lambda i: (i, 0)),
                   pl.BlockSpec((1, 8, BRANCH_W), lambda i: (i, 0, 0))],
        out_shape=[jax.ShapeDtypeStruct((n, D_MODEL), F32),
                   jax.ShapeDtypeStruct((n // tm, 8, BRANCH_W), F32)],
        compiler_params=_cparams(("parallel",)),
        name="prompt_tail",
    )(x, *([parts] * 10), parts, parts, out_c, mkv, mkv,
      lw["g_pre"], lw["g_post"], lw["ln_v_gain"], lw["w_spatial"], lw["b_spatial_t"], lw["conv_w"],
      lw["w_merge"], lw["b_merge"], lw["w_branch"], lw["w_out"])


N_PAGES = 16
PAGE = 128
PAGES_PER_BLOCK = MOBA_BLOCK // PAGE


def _sample_branch_kernel(pt_ref, parts_ref, tab_ref, conv_ref, mk_ref, mv_ref, *rest, dec_seq):
    k_pages = rest[0:N_PAGES]
    v_pages = rest[N_PAGES:2 * N_PAGES]
    lng_ref, wsx_ref, bsx_ref, cw_ref = rest[2 * N_PAGES:2 * N_PAGES + 4]
    ko_ref, vo_ref, co_ref, vn_ref, br_ref = rest[2 * N_PAGES + 4:]
    del pt_ref
    r8 = SAMPLE_ROWS
    part = lambda c: parts_ref[0, :, c * BRANCH_W:(c + 1) * BRANCH_W]
    rix = lax.broadcasted_iota(jnp.int32, (r8, BRANCH_W), 0)

    vn = _layernorm(part(P_AV), lng_ref[...])
    vn_ref[0] = vn
    sp = bsx_ref[...]
    for s in range(dec_seq):
        sp = sp + jnp.where(rix >= s, wsx_ref[s], 0.0) * vn[s:s + 1, :]
    br_ref[0, :, 0:BRANCH_W] = _silu(part(P_AG)) * (part(P_AU) * sp)

    z = part(P_BC) * part(P_BX)
    prev = conv_ref[0, 0]
    zrow = lambda i: prev[i:i + 1, :] if i < CONV_W - 1 else z[i - (CONV_W - 1):i - (CONV_W - 2), :]
    conv = jnp.zeros((r8, BRANCH_W), F32)
    for t in range(dec_seq):
        c_t = cw_ref[0:1, :] * zrow(t) + cw_ref[1:2, :] * zrow(t + 1) + cw_ref[2:3, :] * zrow(t + 2)
        conv = jnp.where(rix == t, c_t, conv)
    br_ref[0, :, BRANCH_W:2 * BRANCH_W] = _silu(part(P_BG)) * (part(P_BB) * conv)
    co_ref[0, 0] = z[dec_seq - (CONV_W - 1):dec_seq, :]

    tab = tab_ref[...]
    q = _rope(part(P_CQ), tab)
    k = _rope(part(P_CK), tab)
    v = part(P_CV)
    ko_ref[0] = k
    vo_ref[0] = v
    n_past = N_PAGES // PAGES_PER_BLOCK
    scale = C_HD ** -0.5
    hr = C_HEADS * r8
    head_rows = lambda x: jnp.concatenate([x[:, h * C_HD:(h + 1) * C_HD] for h in range(C_HEADS)], axis=0)
    q_rows = head_rows(q) * scale
    t_q = lax.broadcasted_iota(jnp.int32, (hr, 1), 0) % r8

    def head_block_t(pages, n, h):
        return jnp.concatenate([pages[n * PAGES_PER_BLOCK + i][0, 0, h] for i in range(PAGES_PER_BLOCK)], axis=1)

    s_own = []
    for c in range(dec_seq):
        k_c = jnp.concatenate([jnp.broadcast_to(k[c:c + 1, h * C_HD:(h + 1) * C_HD], (r8, C_HD))
                               for h in range(C_HEADS)], axis=0)
        s_own.append(jnp.where(t_q >= c, jnp.sum(q_rows * k_c, axis=1, keepdims=True), NEG))
    s_past = []
    gate = jnp.zeros((hr, GATE_COLS), F32)
    gate_col = lax.broadcasted_iota(jnp.int32, (hr, GATE_COLS), 1)
    for n in range(n_past):
        s_n = jnp.concatenate(
            [jnp.dot(q_rows[h * r8:(h + 1) * r8, :], head_block_t(k_pages, n, h),
                     precision=lax.Precision.HIGHEST, preferred_element_type=F32) for h in range(C_HEADS)],
            axis=0)
        s_past.append(s_n)
        gate = jnp.where(gate_col == n, jnp.sum(s_n, axis=1, keepdims=True) * (1.0 / MOBA_BLOCK), gate)
    sel = _top_blocks(gate, gate_col < n_past, 1)
    s_past = [jnp.where(sel[:, n:n + 1] > 0.0, s_past[n], NEG) for n in range(n_past)]
    m = s_own[0]
    for s_c in s_own[1:]:
        m = jnp.maximum(m, s_c)
    for s_n in s_past:
        m = jnp.maximum(m, jnp.max(s_n, axis=1, keepdims=True))
    l = jnp.zeros((hr, 1), F32)
    acc = jnp.zeros((hr, C_HD), F32)
    for c in range(dec_seq):
        v_c = jnp.concatenate([jnp.broadcast_to(v[c:c + 1, h * C_HD:(h + 1) * C_HD], (r8, C_HD))
                               for h in range(C_HEADS)], axis=0)
        p = jnp.exp(s_own[c] - m)
        l = l + p
        acc = acc + p * v_c
    for n in range(n_past):
        p = jnp.exp(s_past[n] - m)
        l = l + jnp.sum(p, axis=1, keepdims=True)
        p = p.astype(BF16)
        acc = acc + jnp.concatenate(
            [lax.dot_general(p[h * r8:(h + 1) * r8, :], head_block_t(v_pages, n, h).astype(BF16), NT_DIMS,
                             preferred_element_type=F32) for h in range(C_HEADS)], axis=0)
    acc = acc / l
    out_c = jnp.concatenate([acc[h * r8:(h + 1) * r8, :] for h in range(C_HEADS)], axis=1)
    br_ref[0, :, 2 * BRANCH_W:3 * BRANCH_W] = _silu(part(P_CG)) * out_c

    out_m = _mem_attend(part(P_MQ), lambda h: mk_ref[0, 0, pl.ds(h, N_MEM, stride=M_HEADS), :],
                        lambda h: mv_ref[0, 0, pl.ds(h, N_MEM, stride=M_HEADS), :])
    br_ref[0, :, 3 * BRANCH_W:4 * BRANCH_W] = _silu(part(P_MG)) * out_m


def _sample_branch(layer, parts3, tab, state_conv, cache_k, cache_v, cache_mem_k, cache_mem_v, page_table, lw,
                   dec_seq):
    bs = parts3.shape[0]
    r8 = SAMPLE_ROWS
    page_spec = lambda p: pl.BlockSpec((1, 1, C_HEADS, C_HD, PAGE), lambda b, pt: (layer, pt[b, p], 0, 0, 0))
    const = lambda shape: pl.BlockSpec(shape, lambda b, pt: (0,) * len(shape))
    seq_spec = lambda rows, w: pl.BlockSpec((1, rows, w), lambda b, pt: (b, 0, 0))
    in_specs = [
        seq_spec(r8, IN_W),
        const((r8, 384)),
        pl.BlockSpec((1, 1, CONV_W - 1, BRANCH_W), lambda b, pt: (layer, b, 0, 0)),
        pl.BlockSpec((1, 1, N_MEM * M_HEADS, M_HD), lambda b, pt: (layer, b, 0, 0)),
        pl.BlockSpec((1, 1, N_MEM * M_HEADS, M_HD), lambda b, pt: (layer, b, 0, 0)),
    ] + [page_spec(p) for p in range(N_PAGES)] + [page_spec(p) for p in range(N_PAGES)] + [
        const((1, BRANCH_W)), const((dec_seq, r8, BRANCH_W)), const((r8, BRANCH_W)), const((CONV_W, BRANCH_W)),
    ]
    out_specs = [seq_spec(r8, BRANCH_W), seq_spec(r8, BRANCH_W),
                 pl.BlockSpec((1, 1, CONV_W - 1, BRANCH_W), lambda b, pt: (b, 0, 0, 0)),
                 seq_spec(r8, BRANCH_W), seq_spec(r8, 4 * BRANCH_W)]
    out_shape = [jax.ShapeDtypeStruct((bs, r8, BRANCH_W), F32), jax.ShapeDtypeStruct((bs, r8, BRANCH_W), F32),
                 jax.ShapeDtypeStruct((bs, 1, CONV_W - 1, BRANCH_W), F32),
                 jax.ShapeDtypeStruct((bs, r8, BRANCH_W), F32), jax.ShapeDtypeStruct((bs, r8, 4 * BRANCH_W), F32)]
    grid_spec = pltpu.PrefetchScalarGridSpec(
        num_scalar_prefetch=1, grid=(bs,), in_specs=in_specs, out_specs=out_specs)
    return pl.pallas_call(
        functools.partial(_sample_branch_kernel, dec_seq=dec_seq),
        grid_spec=grid_spec,
        out_shape=out_shape,
        compiler_params=_cparams(("arbitrary",)),
        name="sample_branch",
    )(page_table, parts3, tab, state_conv, cache_mem_k, cache_mem_v,
      *([cache_k] * N_PAGES), *([cache_v] * N_PAGES),
      lw["ln_v_gain"], lw["w_spatial_x"], lw["b_spatial_x"], lw["conv_w"])


def _sample_tail_kernel(x_ref, br_ref, gpre_ref, gpost_ref, wm_ref, bm_ref, wb_ref, wo_ref, y_ref):
    branches = tuple(br_ref[:, n * BRANCH_W:(n + 1) * BRANCH_W] for n in range(4))
    y_ref[...] = _tail(x_ref[...], branches, gpre_ref, gpost_ref, wm_ref, bm_ref, wb_ref, wo_ref)


def _sample_tail(x, branches, lw):
    n = x.shape[0]
    tm = TAIL_TM
    return pl.pallas_call(
        _sample_tail_kernel,
        grid=(n // tm,),
        in_specs=[
            pl.BlockSpec((tm, D_MODEL), lambda i: (i, 0)),
            pl.BlockSpec((tm, 4 * BRANCH_W), lambda i: (i, 0)),
            _const_spec((1, D_MODEL)), _const_spec((1, D_MODEL)),
            _const_spec((D_MODEL, 4 * D_MODEL)), _const_spec((1, 4 * D_MODEL)),
            _const_spec((4, BRANCH_W, D_MODEL)), _const_spec((D_MODEL, D_MODEL)),
        ],
        out_specs=pl.BlockSpec((tm, D_MODEL), lambda i: (i, 0)),
        out_shape=jax.ShapeDtypeStruct((n, D_MODEL), F32),
        compiler_params=_cparams(("parallel",)),
        name="sample_tail",
    )(x, branches, lw["g_pre"], lw["g_post"], lw["w_merge"], lw["b_merge"], lw["w_branch"], lw["w_out"])


def kernel(x_prompt, x_sample, cache_k, cache_v, cache_mem_k, cache_mem_v, state_conv, page_table, mem_prompt,
           g_pre, g_post, w_in, ln_v_gain, w_spatial, b_spatial, conv_w, g_mem, w_mem_kv, w_merge, b_merge,
           w_branch, w_out):
    bp, tp, d = x_prompt.shape
    bs, ts, _ = x_sample.shape
    depth = w_in.shape[0]
    page = cache_k.shape[2]
    past_len = page_table.shape[1] * page
    assert (d, page, page_table.shape[1]) == (D_MODEL, PAGE, N_PAGES) and ts <= SAMPLE_ROWS
    assert past_len % MOBA_BLOCK == 0 and tp % KV_TM == 0 and ts >= CONV_W - 1
    r8 = SAMPLE_ROWS

    tab_p = _rope_table(jnp.arange(tp, dtype=jnp.int32))
    tab_s = _rope_table(past_len + jnp.arange(r8, dtype=jnp.int32))
    xp = x_prompt.reshape(bp * tp, d)
    xs = jnp.pad(x_sample, ((0, 0), (0, r8 - ts), (0, 0))).reshape(bs * r8, d)
    mem = mem_prompt.reshape(bp * N_MEM, d)
    ck = jnp.transpose(cache_k, (0, 1, 3, 4, 2))
    cv = jnp.transpose(cache_v, (0, 1, 3, 4, 2))
    cmk = cache_mem_k.reshape(depth, bs, N_MEM * M_HEADS, M_HD)
    cmv = cache_mem_v.reshape(depth, bs, N_MEM * M_HEADS, M_HD)
    lane_group = jnp.arange(BRANCH_W) // (BRANCH_W // A_GROUPS)

    outs = {name: [] for name in ("kp", "vp", "cp", "mkp", "mvp", "ks", "vs", "cs", "vns")}
    for l in range(depth):
        wsx = jnp.transpose(w_spatial[l][:, :r8, :ts], (2, 1, 0))[:, :, lane_group]
        bsx = jnp.transpose(b_spatial[l][:, :r8], (1, 0))[:, lane_group]
        lw = {
            "g_pre": g_pre[l].reshape(1, d), "g_post": g_post[l].reshape(1, d),
            "ln_v_gain": ln_v_gain[l].reshape(1, BRANCH_W),
            "w_spatial": w_spatial[l], "b_spatial_t": b_spatial[l].T,
            "w_spatial_x": wsx, "b_spatial_x": bsx,
            "conv_w": conv_w[l],
            "w_merge": w_merge[l].astype(BF16), "b_merge": b_merge[l].reshape(1, 4 * d),
            "w_branch": w_branch[l].astype(BF16), "w_out": w_out[l].astype(BF16),
        }
        w_in_b = w_in[l].astype(BF16)

        mkv = _inproj(mem, g_mem[l], w_mem_kv[l].astype(BF16), 1024, 1024)
        parts = _inproj(xp, g_pre[l], w_in_b, 1024, 1664)
        k_p, v_p, kb, vb, kmean = _kvpost(parts, tab_p, tp)
        kmean = kmean.reshape(bp, tp // MOBA_BLOCK, BRANCH_W)
        out_c = _moba_prompt(parts, tab_p, kb, vb, kmean, bp, tp)
        xp, ztail = _prompt_tail(xp, parts, out_c, mkv, lw, tp)
        outs["kp"].append(jnp.transpose(k_p, (0, 3, 1, 2)))
        outs["vp"].append(jnp.transpose(v_p, (0, 3, 1, 2)))
        outs["cp"].append(ztail.reshape(bp, tp // TAIL_TM, 8, BRANCH_W)[:, -1, 8 - (CONV_W - 1):, :])
        outs["mkp"].append(mkv[:, :BRANCH_W].reshape(bp, N_MEM, M_HEADS, M_HD))
        outs["mvp"].append(mkv[:, BRANCH_W:].reshape(bp, N_MEM, M_HEADS, M_HD))

        parts_s = _inproj(xs, g_pre[l], w_in_b, bs * r8, 1664).reshape(bs, r8, IN_W)
        k_s, v_s, c_s, vn_s, br_s = _sample_branch(l, parts_s, tab_s, state_conv, ck, cv, cmk, cmv, page_table,
                                                   lw, ts)
        xs = _sample_tail(xs, br_s.reshape(bs * r8, 4 * BRANCH_W), lw)
        outs["ks"].append(k_s[:, :ts].reshape(bs, ts, C_HEADS, C_HD))
        outs["vs"].append(v_s[:, :ts].reshape(bs, ts, C_HEADS, C_HD))
        outs["cs"].append(c_s.reshape(bs, CONV_W - 1, BRANCH_W))
        outs["vns"].append(vn_s[:, :ts])

    st = lambda name: jnp.stack(outs[name])
    return (xp.reshape(bp, tp, d), xs.reshape(bs, r8, d)[:, :ts],
            st("kp"), st("vp"), st("cp"), st("mkp"), st("mvp"),
            st("ks"), st("vs"), st("cs"), st("vns"))
```

```python
import functools

import jax
import jax.numpy as jnp
from jax import lax
from jax.experimental import pallas as pl
from jax.experimental.pallas import tpu as pltpu

D_MODEL = 1024
BRANCH_W = 512
N_PARTS = 13
IN_W = N_PARTS * BRANCH_W
CHUNK = 128
A_GROUPS = 4
CONV_W = 3
C_HEADS = 8
C_HD = 64
ROPE_DIM = 16
ROPE_THETA = 500000.0
MOBA_BLOCK = 256
MOBA_TOPK = 3
M_HEADS = 4
M_HD = 128
N_MEM = 256
EPS = 1e-6

P_AU, P_AV, P_AG, P_BB, P_BC, P_BX, P_BG, P_CQ, P_CK, P_CV, P_CG, P_MQ, P_MG = range(13)

SAMPLE_ROWS = 8
NEG = -1e30
GATE_COLS = 128
VMEM_LIMIT = 56 * 1024 * 1024

F32 = jnp.float32
BF16 = jnp.bfloat16
NT_DIMS = (((1,), (1,)), ((), ()))


def _cparams(sem):
    return pltpu.CompilerParams(dimension_semantics=sem, vmem_limit_bytes=VMEM_LIMIT)


def _rms(x, g):
    r = lax.rsqrt(jnp.mean(x * x, axis=-1, keepdims=True) + EPS)
    return (x * r) * g


def _layernorm(x, g):
    mu = jnp.mean(x, axis=-1, keepdims=True)
    xc = x - mu
    r = lax.rsqrt(jnp.mean(xc * xc, axis=-1, keepdims=True) + EPS)
    return (xc * r) * g


def _sigmoid(x):
    return 1.0 / (1.0 + jnp.exp(-x))


def _silu(x):
    return x * _sigmoid(x)


def _rope(x, tab):
    c, s_lo, s_hi = tab[:, 0:128], tab[:, 128:256], tab[:, 256:384]
    half = ROPE_DIM // 2
    outs = []
    for g in range(BRANCH_W // 128):
        xg = x[:, g * 128:(g + 1) * 128]
        outs.append(xg * c + pltpu.roll(xg, 128 - half, axis=1) * s_lo + pltpu.roll(xg, half, axis=1) * s_hi)
    return jnp.concatenate(outs, axis=1)


def _rope_table(pos):
    half = ROPE_DIM // 2
    inv = jnp.power(jnp.float32(ROPE_THETA), -jnp.arange(half, dtype=F32) * (2.0 / ROPE_DIM))
    ang = pos.astype(F32)[:, None] * inv[None, :]
    cos, sin = jnp.cos(ang), jnp.sin(ang)
    d = jnp.arange(128) % C_HD
    idx = d % half
    cosl = jnp.where(d[None, :] < ROPE_DIM, cos[:, idx], 1.0)
    s_lo = jnp.where(d[None, :] < half, -sin[:, idx], 0.0)
    s_hi = jnp.where((d[None, :] >= half) & (d[None, :] < ROPE_DIM), sin[:, idx], 0.0)
    return jnp.concatenate([cosl, s_lo, s_hi], axis=1).astype(F32)


def _top_blocks(gate, valid, axis):
    idx = lax.broadcasted_iota(jnp.int32, gate.shape, axis).astype(F32)
    g = jnp.where(valid, gate, -jnp.inf)
    sel = jnp.zeros(gate.shape, F32)
    for _ in range(MOBA_TOPK):
        m = jnp.max(g, axis=axis, keepdims=True)
        cand = (g == m) & valid & (sel == 0.0)
        first = jnp.min(jnp.where(cand, idx, float(gate.shape[axis])), axis=axis, keepdims=True)
        pick = idx == first
        sel = jnp.where(pick, 1.0, sel)
        g = jnp.where(pick, -jnp.inf, g)
    return sel


def _inproj_kernel(x_ref, g_ref, w_ref, o_ref, h_ref):
    @pl.when(pl.program_id(1) == 0)
    def _():
        h_ref[...] = _rms(x_ref[...], g_ref[...]).astype(BF16)

    o_ref[...] = jnp.dot(h_ref[...], w_ref[...], preferred_element_type=F32)


def _inproj(x, g, w_bf16, tm, tn):
    n, d = x.shape
    n_out = w_bf16.shape[1]
    return pl.pallas_call(
        _inproj_kernel,
        grid=(n // tm, n_out // tn),
        in_specs=[
            pl.BlockSpec((tm, d), lambda i, j: (i, 0)),
            pl.BlockSpec((1, d), lambda i, j: (0, 0)),
            pl.BlockSpec((d, tn), lambda i, j: (0, j)),
        ],
        out_specs=pl.BlockSpec((tm, tn), lambda i, j: (i, j)),
        out_shape=jax.ShapeDtypeStruct((n, n_out), F32),
        scratch_shapes=[pltpu.VMEM((tm, d), BF16)],
        compiler_params=_cparams(("parallel", "arbitrary")),
        name="inproj",
    )(x, g.reshape(1, d), w_bf16)


KV_TM = 1024


V_ROWS = 80
LOG2E = 1.4426950408889634


def _kvpost_kernel(k_ref, v_ref, tab_ref, kt_ref, vt_ref, kb_ref, vtb_ref, km_ref, *, tiles_per_seq):
    k = _rope(k_ref[...], tab_ref[...])
    k_t = k.T
    v_t = v_ref[...].T
    kt_ref[0] = k_t.reshape(C_HEADS, C_HD, KV_TM)
    vt_ref[0] = v_t.reshape(C_HEADS, C_HD, KV_TM)
    lane = lax.broadcasted_iota(jnp.int32, (KV_TM, BRANCH_W), 1)
    row = lax.broadcasted_iota(jnp.int32, (KV_TM, BRANCH_W), 0)
    blk = (pl.program_id(0) % tiles_per_seq) * (KV_TM // MOBA_BLOCK) + row // MOBA_BLOCK
    onehot = jnp.where(lane % C_HD == blk, 1.0, 0.0)
    odd_head = (lane // C_HD) % 2 == 1
    kb_ref[0] = jnp.where(odd_head, onehot, k).astype(BF16)
    kb_ref[1] = jnp.where(odd_head, k, onehot).astype(BF16)
    extra = jnp.where(lax.broadcasted_iota(jnp.int32, (V_ROWS - C_HD, MOBA_BLOCK), 0) == 0, 1.0, 0.0)
    for i in range(KV_TM // MOBA_BLOCK):
        rows = slice(i * MOBA_BLOCK, (i + 1) * MOBA_BLOCK)
        pieces = []
        for h in range(C_HEADS):
            pieces += [v_t[h * C_HD:(h + 1) * C_HD, rows], extra]
        vtb_ref[i] = jnp.concatenate(pieces, axis=0).astype(BF16)
        km_ref[0, i:i + 1, :] = jnp.sum(k[rows, :], axis=0, keepdims=True) * (1.0 / MOBA_BLOCK)


def _kvpost(parts, tab, seq):
    n = parts.shape[0]
    tiles_per_seq = seq // KV_TM
    blocks_per_tile = KV_TM // MOBA_BLOCK
    assert seq // MOBA_BLOCK <= C_HD
    blk = lambda c: pl.BlockSpec((KV_TM, BRANCH_W), lambda i: (i, c))
    slab = pl.BlockSpec((1, C_HEADS, C_HD, KV_TM), lambda i: (i // tiles_per_seq, 0, 0, i % tiles_per_seq))
    return pl.pallas_call(
        functools.partial(_kvpost_kernel, tiles_per_seq=tiles_per_seq),
        grid=(n // KV_TM,),
        in_specs=[blk(P_CK), blk(P_CV),
                  pl.BlockSpec((KV_TM, 384), lambda i: (i % tiles_per_seq, 0))],
        out_specs=[slab, slab,
                   pl.BlockSpec((2, KV_TM, BRANCH_W), lambda i: (0, i, 0)),
                   pl.BlockSpec((blocks_per_tile, C_HEADS * V_ROWS, MOBA_BLOCK), lambda i: (i, 0, 0)),
                   pl.BlockSpec((1, blocks_per_tile, BRANCH_W), lambda i: (i, 0, 0))],
        out_shape=[jax.ShapeDtypeStruct((n // seq, C_HEADS, C_HD, seq), F32),
                   jax.ShapeDtypeStruct((n // seq, C_HEADS, C_HD, seq), F32),
                   jax.ShapeDtypeStruct((2, n, BRANCH_W), BF16),
                   jax.ShapeDtypeStruct((n // MOBA_BLOCK, C_HEADS * V_ROWS, MOBA_BLOCK), BF16),
                   jax.ShapeDtypeStruct((n // KV_TM, blocks_per_tile, BRANCH_W), F32)],
        compiler_params=_cparams(("parallel",)),
        name="kvpost",
    )(parts, parts, tab)


MOBA_HEADS_PER_LOOP = 4


def _moba_prompt_kernel(q_ref, tab_ref, kb_ref, vtb_ref, km_ref, o_ref, qt_ref, m_ref, acc_ref):
    j = pl.program_id(1)
    tq = MOBA_BLOCK
    nb = km_ref.shape[1]
    q = _rope(q_ref[...], tab_ref[...])
    km = km_ref[0]
    blk_i = lax.broadcasted_iota(jnp.int32, (nb, tq), 0)
    dim_i = lax.broadcasted_iota(jnp.int32, (128, tq), 0)
    for g in range(C_HEADS // 2):
        lanes = slice(g * 128, (g + 1) * 128)
        qt = q[:, lanes].T
        for hh in range(2):
            h = 2 * g + hh
            qh = jnp.where((dim_i >= hh * C_HD) & (dim_i < (hh + 1) * C_HD), qt, 0.0)
            gate = jnp.dot(km[:, lanes], qh, precision=lax.Precision.HIGHEST, preferred_element_type=F32)
            sel = _top_blocks(gate, blk_i < j, 0)
            pen = jnp.where((sel > 0.0) | (blk_i == j), 0.0, NEG)
            pen = jnp.concatenate([pen, jnp.zeros((C_HD - nb, tq), F32)], axis=0)
            qs = qh * (C_HD ** -0.5 * LOG2E)
            full = (jnp.concatenate([qs[0:C_HD, :], pen], axis=0) if hh == 0
                    else jnp.concatenate([pen, qs[C_HD:2 * C_HD, :]], axis=0))
            qt_ref[h] = full.astype(BF16)
    m_ref[...] = jnp.full(m_ref.shape, NEG, F32)
    acc_ref[...] = jnp.zeros(acc_ref.shape, F32)

    def scores(heads, n):
        r0 = pl.multiple_of(n * MOBA_BLOCK, MOBA_BLOCK)
        out = []
        for h in heads:
            kblk = kb_ref[h % 2, pl.ds(r0, MOBA_BLOCK), (h // 2) * 128:(h // 2 + 1) * 128]
            out.append(jnp.dot(kblk, qt_ref[h], preferred_element_type=F32))
        return tuple(out)

    def accumulate(heads, n, s_list):
        for h, s in zip(heads, s_list):
            rows = slice(h * V_ROWS, (h + 1) * V_ROWS)
            m = m_ref[h:h + 1, :]
            m_new = jnp.maximum(m, jnp.max(s, axis=0, keepdims=True))
            a = jnp.exp2(m - m_new)
            p = jnp.exp2(s - m_new).astype(BF16)
            pv = jnp.dot(vtb_ref[n, rows, :], p, preferred_element_type=F32)
            acc_ref[rows, :] = a * acc_ref[rows, :] + pv
            m_ref[h:h + 1, :] = m_new

    key_i = lax.broadcasted_iota(jnp.int32, (MOBA_BLOCK, tq), 0)
    qry_i = lax.broadcasted_iota(jnp.int32, (MOBA_BLOCK, tq), 1)
    causal = key_i <= qry_i
    for h0 in range(0, C_HEADS, MOBA_HEADS_PER_LOOP):
        heads = tuple(range(h0, h0 + MOBA_HEADS_PER_LOOP))
        s_own = tuple(jnp.where(causal, s, NEG) for s in scores(heads, j))

        def body(n, s_prev, heads=heads):
            s_next = scores(heads, n)
            accumulate(heads, jnp.where(n == 0, j, n - 1), s_prev)
            return s_next

        s_last = lax.fori_loop(0, j, body, s_own)
        accumulate(heads, jnp.maximum(j - 1, 0), s_last)
    out_t = jnp.concatenate(
        [acc_ref[h * V_ROWS:h * V_ROWS + C_HD, :] / acc_ref[h * V_ROWS + C_HD:h * V_ROWS + C_HD + 1, :]
         for h in range(C_HEADS)], axis=0)
    o_ref[...] = out_t.T


def _moba_prompt(parts, tab, kb, vtb, kmean, batch, seq):
    n = parts.shape[0]
    nb = seq // MOBA_BLOCK
    return pl.pallas_call(
        _moba_prompt_kernel,
        grid=(batch, nb),
        in_specs=[
            pl.BlockSpec((MOBA_BLOCK, BRANCH_W), lambda b, j: (b * nb + j, P_CQ)),
            pl.BlockSpec((MOBA_BLOCK, 384), lambda b, j: (j, 0)),
            pl.BlockSpec((2, seq, BRANCH_W), lambda b, j: (0, b, 0)),
            pl.BlockSpec((nb, C_HEADS * V_ROWS, MOBA_BLOCK), lambda b, j: (b, 0, 0)),
            pl.BlockSpec((1, nb, BRANCH_W), lambda b, j: (b, 0, 0)),
        ],
        out_specs=pl.BlockSpec((MOBA_BLOCK, BRANCH_W), lambda b, j: (b * nb + j, 0)),
        out_shape=jax.ShapeDtypeStruct((n, BRANCH_W), F32),
        scratch_shapes=[pltpu.VMEM((C_HEADS, 128, MOBA_BLOCK), BF16),
                        pltpu.VMEM((C_HEADS, MOBA_BLOCK), F32),
                        pltpu.VMEM((C_HEADS * V_ROWS, MOBA_BLOCK), F32)],
        compiler_params=_cparams(("parallel", "arbitrary")),
        name="moba_prompt",
    )(parts, tab, kb, vtb, kmean)


def _tail(x, branches, gpre_ref, gpost_ref, wm_ref, bm_ref, wb_ref, wo_ref):
    h = _rms(x, gpre_ref[...]).astype(BF16)
    acc = None
    for n, br in enumerate(branches):
        cols = slice(n * D_MODEL, (n + 1) * D_MODEL)
        gate = _sigmoid(jnp.dot(h, wm_ref[:, cols], preferred_element_type=F32) + bm_ref[:, cols])
        proj = jnp.dot(br.astype(BF16), wb_ref[n], preferred_element_type=F32)
        acc = gate * proj if acc is None else acc + gate * proj
    y = jnp.dot(acc.astype(BF16), wo_ref[...], preferred_element_type=F32)
    return x + _rms(y, gpost_ref[...])


def _mem_attend(problems):
    lanes = lambda h: slice(h * M_HD, (h + 1) * M_HD)
    scores = [[lax.dot_general(mq[:, lanes(h)].astype(BF16), mk_head(h).astype(BF16), NT_DIMS,
                               preferred_element_type=F32) * (M_HD ** -0.5) for h in range(M_HEADS)]
              for mq, mk_head, _ in problems]
    probs = [[jnp.exp(s - jnp.max(s, axis=1, keepdims=True)) for s in per_head] for per_head in scores]
    return [jnp.concatenate(
        [jnp.dot(p.astype(BF16), mv_head(h).astype(BF16), preferred_element_type=F32)
         / jnp.sum(p, axis=1, keepdims=True) for h, p in enumerate(per_head)], axis=1)
        for (_, _, mv_head), per_head in zip(problems, probs)]


TAIL_TM = 256


def _prompt_tail_kernel(x_ref, au_ref, av_ref, ag_ref, bb_ref, bc_ref, bx_ref, bg_ref, cg_ref, mq_ref, mg_ref,
                        hc_ref, hx_ref, oc_ref, mk_ref, mv_ref,
                        gpre_ref, gpost_ref, lng_ref, ws_ref, bs_ref, cw_ref, wm_ref, bm_ref, wb_ref, wo_ref,
                        y_ref, zt_ref, *, tiles_per_seq):
    tm = TAIL_TM
    x = x_ref[...]
    vn = _layernorm(av_ref[...], lng_ref[...]).astype(BF16)
    t_i = lax.broadcasted_iota(jnp.int32, (CHUNK, CHUNK), 0)
    s_i = lax.broadcasted_iota(jnp.int32, (CHUNK, CHUNK), 1)
    sp_cols = []
    for g in range(A_GROUPS):
        lanes = slice(g * 128, (g + 1) * 128)
        ws = jnp.where(s_i <= t_i, ws_ref[g], 0.0).astype(BF16)
        bias = bs_ref[:, g:g + 1]
        rows = [jnp.dot(ws, vn[c * CHUNK:(c + 1) * CHUNK, lanes], preferred_element_type=F32) + bias
                for c in range(tm // CHUNK)]
        sp_cols.append(jnp.concatenate(rows, axis=0))
    br_a = _silu(ag_ref[...]) * (au_ref[...] * jnp.concatenate(sp_cols, axis=1))
    z = bc_ref[...] * bx_ref[...]
    first_of_seq = (pl.program_id(0) % tiles_per_seq) == 0
    halo = jnp.where(first_of_seq, 0.0, hc_ref[...] * hx_ref[...])
    rix = lax.broadcasted_iota(jnp.int32, (tm, BRANCH_W), 0)
    z1 = jnp.where(rix == 0, halo[7:8, :], pltpu.roll(z, 1, axis=0))
    z2 = pltpu.roll(z, 2, axis=0)
    z2 = jnp.where(rix == 0, halo[6:7, :], jnp.where(rix == 1, halo[7:8, :], z2))
    conv = cw_ref[0:1, :] * z2 + cw_ref[1:2, :] * z1 + cw_ref[2:3, :] * z
    br_b = _silu(bg_ref[...]) * (bb_ref[...] * conv)
    zt_ref[0] = z[tm - 8:tm, :]
    br_c = _silu(cg_ref[...]) * oc_ref[...]
    out_m, = _mem_attend([(mq_ref[...], lambda h: mk_ref[:, h * M_HD:(h + 1) * M_HD],
                           lambda h: mv_ref[:, h * M_HD:(h + 1) * M_HD])])
    br_m = _silu(mg_ref[...]) * out_m
    y_ref[...] = _tail(x, (br_a, br_b, br_c, br_m), gpre_ref, gpost_ref, wm_ref, bm_ref, wb_ref, wo_ref)


def _const_spec(shape):
    zeros = (0,) * len(shape)
    return pl.BlockSpec(shape, lambda i: zeros, pipeline_mode=pl.Buffered(1))


def _prompt_tail(x, parts, out_c, mkv, lw, seq):
    n = x.shape[0]
    tm = TAIL_TM
    tiles_per_seq = seq // tm
    part = lambda c: pl.BlockSpec((tm, BRANCH_W), lambda i: (i, c))
    halo = lambda c: pl.BlockSpec((8, BRANCH_W), lambda i: (jnp.maximum(i * (tm // 8) - 1, 0), c))
    row512 = pl.BlockSpec((tm, BRANCH_W), lambda i: (i, 0))
    in_specs = [
        pl.BlockSpec((tm, D_MODEL), lambda i: (i, 0)),
        part(P_AU), part(P_AV), part(P_AG), part(P_BB), part(P_BC), part(P_BX), part(P_BG),
        part(P_CG), part(P_MQ), part(P_MG),
        halo(P_BC), halo(P_BX),
        row512,
        pl.BlockSpec((N_MEM, BRANCH_W), lambda i: (i // tiles_per_seq, 0)),
        pl.BlockSpec((N_MEM, BRANCH_W), lambda i: (i // tiles_per_seq, 1)),
        _const_spec((1, D_MODEL)), _const_spec((1, D_MODEL)), _const_spec((1, BRANCH_W)),
        _const_spec((A_GROUPS, CHUNK, CHUNK)), _const_spec((CHUNK, A_GROUPS)), _const_spec((CONV_W, BRANCH_W)),
        _const_spec((D_MODEL, 4 * D_MODEL)), _const_spec((1, 4 * D_MODEL)),
        _const_spec((4, BRANCH_W, D_MODEL)), _const_spec((D_MODEL, D_MODEL)),
    ]
    return pl.pallas_call(
        functools.partial(_prompt_tail_kernel, tiles_per_seq=tiles_per_seq),
        grid=(n // tm,),
        in_specs=in_specs,
        out_specs=[pl.BlockSpec((tm, D_MODEL), lambda i: (i, 0)),
                   pl.BlockSpec((1, 8, BRANCH_W), lambda i: (i, 0, 0))],
        out_shape=[jax.ShapeDtypeStruct((n, D_MODEL), F32),
                   jax.ShapeDtypeStruct((n // tm, 8, BRANCH_W), F32)],
        compiler_params=_cparams(("parallel",)),
        name="prompt_tail",
    )(x, *([parts] * 10), parts, parts, out_c, mkv, mkv,
      lw["g_pre"], lw["g_post"], lw["ln_v_gain"], lw["w_spatial"], lw["b_spatial_t"], lw["conv_w"],
      lw["w_merge"], lw["b_merge"], lw["w_branch"], lw["w_out"])


N_PAGES = 16
PAGE = 128
PAGES_PER_BLOCK = MOBA_BLOCK // PAGE


SAMPLE_SEQS = 2


def _sample_branch_kernel(pt_ref, parts_ref, tab_ref, conv_ref, mk_ref, mv_ref, *rest, dec_seq):
    ns = SAMPLE_SEQS
    k_pages = [rest[s * N_PAGES:(s + 1) * N_PAGES] for s in range(ns)]
    v_pages = [rest[(ns + s) * N_PAGES:(ns + s + 1) * N_PAGES] for s in range(ns)]
    lng_ref, wsx_ref, bsx_ref, cw_ref, ko_ref, vo_ref, co_ref, vn_ref, br_ref = rest[2 * ns * N_PAGES:]
    del pt_ref
    r8 = SAMPLE_ROWS
    part = lambda s, c: parts_ref[s, :, c * BRANCH_W:(c + 1) * BRANCH_W]
    rix = lax.broadcasted_iota(jnp.int32, (r8, BRANCH_W), 0)

    for s in range(ns):
        vn = _layernorm(part(s, P_AV), lng_ref[...])
        vn_ref[s] = vn
        sp = bsx_ref[...]
        for i in range(dec_seq):
            sp = sp + jnp.where(rix >= i, wsx_ref[i], 0.0) * vn[i:i + 1, :]
        br_ref[s, :, 0:BRANCH_W] = _silu(part(s, P_AG)) * (part(s, P_AU) * sp)

        z = part(s, P_BC) * part(s, P_BX)
        prev = conv_ref[0, s]
        zrow = lambda i, prev=prev, z=z: (prev[i:i + 1, :] if i < CONV_W - 1
                                          else z[i - (CONV_W - 1):i - (CONV_W - 2), :])
        conv = jnp.zeros((r8, BRANCH_W), F32)
        for t in range(dec_seq):
            c_t = cw_ref[0:1, :] * zrow(t) + cw_ref[1:2, :] * zrow(t + 1) + cw_ref[2:3, :] * zrow(t + 2)
            conv = jnp.where(rix == t, c_t, conv)
        br_ref[s, :, BRANCH_W:2 * BRANCH_W] = _silu(part(s, P_BG)) * (part(s, P_BB) * conv)
        co_ref[s, 0] = z[dec_seq - (CONV_W - 1):dec_seq, :]

    tab = tab_ref[...]
    q = [_rope(part(s, P_CQ), tab) for s in range(ns)]
    k = [_rope(part(s, P_CK), tab) for s in range(ns)]
    v = [part(s, P_CV) for s in range(ns)]
    for s in range(ns):
        ko_ref[s] = k[s]
        vo_ref[s] = v[s]
    n_past = N_PAGES // PAGES_PER_BLOCK
    scale = C_HD ** -0.5
    hr = ns * C_HEADS * r8
    stack = lambda f: jnp.concatenate([f(s, h) for s in range(ns) for h in range(C_HEADS)], axis=0)
    head_lanes = lambda h: slice(h * C_HD, (h + 1) * C_HD)
    group = lambda x, s, h: x[(s * C_HEADS + h) * r8:(s * C_HEADS + h + 1) * r8, :]
    q_rows = stack(lambda s, h: q[s][:, head_lanes(h)]) * scale
    t_q = lax.broadcasted_iota(jnp.int32, (hr, 1), 0) % r8

    def head_block_t(pages, n, h):
        return jnp.concatenate([pages[n * PAGES_PER_BLOCK + i][0, 0, h] for i in range(PAGES_PER_BLOCK)], axis=1)

    s_own = []
    for c in range(dec_seq):
        k_c = stack(lambda s, h: jnp.broadcast_to(k[s][c:c + 1, head_lanes(h)], (r8, C_HD)))
        s_own.append(jnp.where(t_q >= c, jnp.sum(q_rows * k_c, axis=1, keepdims=True), NEG))
    s_past = []
    gate = jnp.zeros((hr, GATE_COLS), F32)
    gate_col = lax.broadcasted_iota(jnp.int32, (hr, GATE_COLS), 1)
    for n in range(n_past):
        s_n = stack(lambda s, h: jnp.dot(group(q_rows, s, h), head_block_t(k_pages[s], n, h),
                                         precision=lax.Precision.HIGHEST, preferred_element_type=F32))
        s_past.append(s_n)
        gate = jnp.where(gate_col == n, jnp.sum(s_n, axis=1, keepdims=True) * (1.0 / MOBA_BLOCK), gate)
    sel = _top_blocks(gate, gate_col < n_past, 1)
    s_past = [jnp.where(sel[:, n:n + 1] > 0.0, s_past[n], NEG) for n in range(n_past)]
    m = s_own[0]
    for s_c in s_own[1:]:
        m = jnp.maximum(m, s_c)
    for s_n in s_past:
        m = jnp.maximum(m, jnp.max(s_n, axis=1, keepdims=True))
    l = jnp.zeros((hr, 1), F32)
    acc = jnp.zeros((hr, C_HD), F32)
    for c in range(dec_seq):
        v_c = stack(lambda s, h: jnp.broadcast_to(v[s][c:c + 1, head_lanes(h)], (r8, C_HD)))
        p = jnp.exp(s_own[c] - m)
        l = l + p
        acc = acc + p * v_c
    for n in range(n_past):
        p = jnp.exp(s_past[n] - m)
        l = l + jnp.sum(p, axis=1, keepdims=True)
        p = p.astype(BF16)
        acc = acc + stack(lambda s, h: lax.dot_general(group(p, s, h), head_block_t(v_pages[s], n, h).astype(BF16),
                                                       NT_DIMS, preferred_element_type=F32))
    acc = acc / l

    out_m = _mem_attend([(part(s, P_MQ),
                          lambda h, s=s: mk_ref[0, s, pl.ds(h, N_MEM, stride=M_HEADS), :],
                          lambda h, s=s: mv_ref[0, s, pl.ds(h, N_MEM, stride=M_HEADS), :]) for s in range(ns)])
    for s in range(ns):
        out_c = jnp.concatenate([group(acc, s, h) for h in range(C_HEADS)], axis=1)
        br_ref[s, :, 2 * BRANCH_W:3 * BRANCH_W] = _silu(part(s, P_CG)) * out_c
        br_ref[s, :, 3 * BRANCH_W:4 * BRANCH_W] = _silu(part(s, P_MG)) * out_m[s]


def _sample_branch(layer, parts3, tab, state_conv, cache_k, cache_v, cache_mem_k, cache_mem_v, page_table, lw,
                   dec_seq):
    bs = parts3.shape[0]
    r8 = SAMPLE_ROWS
    ns = SAMPLE_SEQS
    page_spec = lambda s, p: pl.BlockSpec((1, 1, C_HEADS, C_HD, PAGE),
                                          lambda b, pt: (layer, pt[ns * b + s, p], 0, 0, 0))
    page_specs = [page_spec(s, p) for s in range(ns) for p in range(N_PAGES)]
    const = lambda shape: pl.BlockSpec(shape, lambda b, pt: (0,) * len(shape))
    seq_spec = lambda rows, w: pl.BlockSpec((ns, rows, w), lambda b, pt: (b, 0, 0))
    in_specs = [
        seq_spec(r8, IN_W),
        const((r8, 384)),
        pl.BlockSpec((1, ns, CONV_W - 1, BRANCH_W), lambda b, pt: (layer, b, 0, 0)),
        pl.BlockSpec((1, ns, N_MEM * M_HEADS, M_HD), lambda b, pt: (layer, b, 0, 0)),
        pl.BlockSpec((1, ns, N_MEM * M_HEADS, M_HD), lambda b, pt: (layer, b, 0, 0)),
    ] + page_specs + page_specs + [
        const((1, BRANCH_W)), const((dec_seq, r8, BRANCH_W)), const((r8, BRANCH_W)), const((CONV_W, BRANCH_W)),
    ]
    out_specs = [seq_spec(r8, BRANCH_W), seq_spec(r8, BRANCH_W),
                 pl.BlockSpec((ns, 1, CONV_W - 1, BRANCH_W), lambda b, pt: (b, 0, 0, 0)),
                 seq_spec(r8, BRANCH_W), seq_spec(r8, 4 * BRANCH_W)]
    out_shape = [jax.ShapeDtypeStruct((bs, r8, BRANCH_W), F32), jax.ShapeDtypeStruct((bs, r8, BRANCH_W), F32),
                 jax.ShapeDtypeStruct((bs, 1, CONV_W - 1, BRANCH_W), F32),
                 jax.ShapeDtypeStruct((bs, r8, BRANCH_W), F32), jax.ShapeDtypeStruct((bs, r8, 4 * BRANCH_W), F32)]
    grid_spec = pltpu.PrefetchScalarGridSpec(
        num_scalar_prefetch=1, grid=(bs // ns,), in_specs=in_specs, out_specs=out_specs)
    return pl.pallas_call(
        functools.partial(_sample_branch_kernel, dec_seq=dec_seq),
        grid_spec=grid_spec,
        out_shape=out_shape,
        compiler_params=_cparams(("arbitrary",)),
        name="sample_branch",
    )(page_table, parts3, tab, state_conv, cache_mem_k, cache_mem_v,
      *([cache_k] * (ns * N_PAGES)), *([cache_v] * (ns * N_PAGES)),
      lw["ln_v_gain"], lw["w_spatial_x"], lw["b_spatial_x"], lw["conv_w"])


def _sample_tail_kernel(x_ref, br_ref, gpre_ref, gpost_ref, wm_ref, bm_ref, wb_ref, wo_ref, y_ref):
    branches = tuple(br_ref[:, n * BRANCH_W:(n + 1) * BRANCH_W] for n in range(4))
    y_ref[...] = _tail(x_ref[...], branches, gpre_ref, gpost_ref, wm_ref, bm_ref, wb_ref, wo_ref)


def _sample_tail(x, branches, lw):
    n = x.shape[0]
    tm = TAIL_TM
    return pl.pallas_call(
        _sample_tail_kernel,
        grid=(n // tm,),
        in_specs=[
            pl.BlockSpec((tm, D_MODEL), lambda i: (i, 0)),
            pl.BlockSpec((tm, 4 * BRANCH_W), lambda i: (i, 0)),
            _const_spec((1, D_MODEL)), _const_spec((1, D_MODEL)),
            _const_spec((D_MODEL, 4 * D_MODEL)), _const_spec((1, 4 * D_MODEL)),
            _const_spec((4, BRANCH_W, D_MODEL)), _const_spec((D_MODEL, D_MODEL)),
        ],
        out_specs=pl.BlockSpec((tm, D_MODEL), lambda i: (i, 0)),
        out_shape=jax.ShapeDtypeStruct((n, D_MODEL), F32),
        compiler_params=_cparams(("parallel",)),
        name="sample_tail",
    )(x, branches, lw["g_pre"], lw["g_post"], lw["w_merge"], lw["b_merge"], lw["w_branch"], lw["w_out"])


def kernel(x_prompt, x_sample, cache_k, cache_v, cache_mem_k, cache_mem_v, state_conv, page_table, mem_prompt,
           g_pre, g_post, w_in, ln_v_gain, w_spatial, b_spatial, conv_w, g_mem, w_mem_kv, w_merge, b_merge,
           w_branch, w_out):
    bp, tp, d = x_prompt.shape
    bs, ts, _ = x_sample.shape
    depth = w_in.shape[0]
    page = cache_k.shape[2]
    past_len = page_table.shape[1] * page
    assert (d, page, page_table.shape[1]) == (D_MODEL, PAGE, N_PAGES) and ts <= SAMPLE_ROWS
    assert past_len % MOBA_BLOCK == 0 and tp % KV_TM == 0 and ts >= CONV_W - 1
    r8 = SAMPLE_ROWS

    tab_p = _rope_table(jnp.arange(tp, dtype=jnp.int32))
    tab_s = _rope_table(past_len + jnp.arange(r8, dtype=jnp.int32))
    xp = x_prompt.reshape(bp * tp, d)
    xs = jnp.pad(x_sample, ((0, 0), (0, r8 - ts), (0, 0))).reshape(bs * r8, d)
    mem = mem_prompt.reshape(bp * N_MEM, d)
    ck = jnp.transpose(cache_k, (0, 1, 3, 4, 2))
    cv = jnp.transpose(cache_v, (0, 1, 3, 4, 2))
    cmk = cache_mem_k.reshape(depth, bs, N_MEM * M_HEADS, M_HD)
    cmv = cache_mem_v.reshape(depth, bs, N_MEM * M_HEADS, M_HD)
    lane_group = jnp.arange(BRANCH_W) // (BRANCH_W // A_GROUPS)

    outs = {name: [] for name in ("kp", "vp", "cp", "mkp", "mvp", "ks", "vs", "cs", "vns")}
    for l in range(depth):
        wsx = jnp.transpose(w_spatial[l][:, :r8, :ts], (2, 1, 0))[:, :, lane_group]
        bsx = jnp.transpose(b_spatial[l][:, :r8], (1, 0))[:, lane_group]
        lw = {
            "g_pre": g_pre[l].reshape(1, d), "g_post": g_post[l].reshape(1, d),
            "ln_v_gain": ln_v_gain[l].reshape(1, BRANCH_W),
            "w_spatial": w_spatial[l], "b_spatial_t": b_spatial[l].T,
            "w_spatial_x": wsx, "b_spatial_x": bsx,
            "conv_w": conv_w[l],
            "w_merge": w_merge[l].astype(BF16), "b_merge": b_merge[l].reshape(1, 4 * d),
            "w_branch": w_branch[l].astype(BF16), "w_out": w_out[l].astype(BF16),
        }
        w_in_b = w_in[l].astype(BF16)

        mkv = _inproj(mem, g_mem[l], w_mem_kv[l].astype(BF16), 1024, 1024)
        parts = _inproj(xp, g_pre[l], w_in_b, 1024, 1664)
        k_p, v_p, kb, vb, kmean = _kvpost(parts, tab_p, tp)
        kmean = kmean.reshape(bp, tp // MOBA_BLOCK, BRANCH_W)
        out_c = _moba_prompt(parts, tab_p, kb, vb, kmean, bp, tp)
        xp, ztail = _prompt_tail(xp, parts, out_c, mkv, lw, tp)
        outs["kp"].append(jnp.transpose(k_p, (0, 3, 1, 2)))
        outs["vp"].append(jnp.transpose(v_p, (0, 3, 1, 2)))
        outs["cp"].append(ztail.reshape(bp, tp // TAIL_TM, 8, BRANCH_W)[:, -1, 8 - (CONV_W - 1):, :])
        outs["mkp"].append(mkv[:, :BRANCH_W].reshape(bp, N_MEM, M_HEADS, M_HD))
        outs["mvp"].append(mkv[:, BRANCH_W:].reshape(bp, N_MEM, M_HEADS, M_HD))

        parts_s = _inproj(xs, g_pre[l], w_in_b, bs * r8, 1664).reshape(bs, r8, IN_W)
        k_s, v_s, c_s, vn_s, br_s = _sample_branch(l, parts_s, tab_s, state_conv, ck, cv, cmk, cmv, page_table,
                                                   lw, ts)
        xs = _sample_tail(xs, br_s.reshape(bs * r8, 4 * BRANCH_W), lw)
        outs["ks"].append(k_s[:, :ts].reshape(bs, ts, C_HEADS, C_HD))
        outs["vs"].append(v_s[:, :ts].reshape(bs, ts, C_HEADS, C_HD))
        outs["cs"].append(c_s.reshape(bs, CONV_W - 1, BRANCH_W))
        outs["vns"].append(vn_s[:, :ts])

    st = lambda name: jnp.stack(outs[name])
    return (xp.reshape(bp, tp, d), xs.reshape(bs, r8, d)[:, :ts],
            st("kp"), st("vp"), st("cp"), st("mkp"), st("mvp"),
            st("ks"), st("vs"), st("cs"), st("vns"))
```

```python
import functools

import jax
import jax.numpy as jnp
from jax import lax
from jax.experimental import pallas as pl
from jax.experimental.pallas import tpu as pltpu

D_MODEL = 1024
BRANCH_W = 512
N_PARTS = 13
IN_W = N_PARTS * BRANCH_W
CHUNK = 128
A_GROUPS = 4
CONV_W = 3
C_HEADS = 8
C_HD = 64
ROPE_DIM = 16
ROPE_THETA = 500000.0
MOBA_BLOCK = 256
MOBA_TOPK = 3
M_HEADS = 4
M_HD = 128
N_MEM = 256
EPS = 1e-6

P_AU, P_AV, P_AG, P_BB, P_BC, P_BX, P_BG, P_CQ, P_CK, P_CV, P_CG, P_MQ, P_MG = range(13)

SAMPLE_ROWS = 8
NEG = -1e30
GATE_COLS = 128
VMEM_LIMIT = 56 * 1024 * 1024

F32 = jnp.float32
BF16 = jnp.bfloat16
NT_DIMS = (((1,), (1,)), ((), ()))


def _cparams(sem):
    return pltpu.CompilerParams(dimension_semantics=sem, vmem_limit_bytes=VMEM_LIMIT)


def _rms(x, g):
    r = lax.rsqrt(jnp.mean(x * x, axis=-1, keepdims=True) + EPS)
    return (x * r) * g


def _layernorm(x, g):
    mu = jnp.mean(x, axis=-1, keepdims=True)
    xc = x - mu
    r = lax.rsqrt(jnp.mean(xc * xc, axis=-1, keepdims=True) + EPS)
    return (xc * r) * g


def _sigmoid(x):
    return 1.0 / (1.0 + jnp.exp(-x))


def _silu(x):
    return x * _sigmoid(x)


def _rope(x, tab):
    c, s_lo, s_hi = tab[:, 0:128], tab[:, 128:256], tab[:, 256:384]
    half = ROPE_DIM // 2
    outs = []
    for g in range(BRANCH_W // 128):
        xg = x[:, g * 128:(g + 1) * 128]
        outs.append(xg * c + pltpu.roll(xg, 128 - half, axis=1) * s_lo + pltpu.roll(xg, half, axis=1) * s_hi)
    return jnp.concatenate(outs, axis=1)


def _rope_table(pos):
    half = ROPE_DIM // 2
    inv = jnp.power(jnp.float32(ROPE_THETA), -jnp.arange(half, dtype=F32) * (2.0 / ROPE_DIM))
    ang = pos.astype(F32)[:, None] * inv[None, :]
    cos, sin = jnp.cos(ang), jnp.sin(ang)
    d = jnp.arange(128) % C_HD
    idx = d % half
    cosl = jnp.where(d[None, :] < ROPE_DIM, cos[:, idx], 1.0)
    s_lo = jnp.where(d[None, :] < half, -sin[:, idx], 0.0)
    s_hi = jnp.where((d[None, :] >= half) & (d[None, :] < ROPE_DIM), sin[:, idx], 0.0)
    return jnp.concatenate([cosl, s_lo, s_hi], axis=1).astype(F32)


def _top_blocks(gate, valid, axis):
    idx = lax.broadcasted_iota(jnp.int32, gate.shape, axis).astype(F32)
    g = jnp.where(valid, gate, -jnp.inf)
    sel = jnp.zeros(gate.shape, F32)
    for _ in range(MOBA_TOPK):
        m = jnp.max(g, axis=axis, keepdims=True)
        cand = (g == m) & valid & (sel == 0.0)
        first = jnp.min(jnp.where(cand, idx, float(gate.shape[axis])), axis=axis, keepdims=True)
        pick = idx == first
        sel = jnp.where(pick, 1.0, sel)
        g = jnp.where(pick, -jnp.inf, g)
    return sel


def _inproj_kernel(x_ref, g_ref, w_ref, o_ref, h_ref):
    @pl.when(pl.program_id(1) == 0)
    def _():
        h_ref[...] = _rms(x_ref[...], g_ref[...]).astype(BF16)

    o_ref[...] = jnp.dot(h_ref[...], w_ref[...], preferred_element_type=F32)


def _inproj(x, g, w_bf16, tm, tn):
    n, d = x.shape
    n_out = w_bf16.shape[1]
    return pl.pallas_call(
        _inproj_kernel,
        grid=(n // tm, n_out // tn),
        in_specs=[
            pl.BlockSpec((tm, d), lambda i, j: (i, 0)),
            pl.BlockSpec((1, d), lambda i, j: (0, 0)),
            pl.BlockSpec((d, tn), lambda i, j: (0, j)),
        ],
        out_specs=pl.BlockSpec((tm, tn), lambda i, j: (i, j)),
        out_shape=jax.ShapeDtypeStruct((n, n_out), F32),
        scratch_shapes=[pltpu.VMEM((tm, d), BF16)],
        compiler_params=_cparams(("parallel", "arbitrary")),
        name="inproj",
    )(x, g.reshape(1, d), w_bf16)


KV_TM = 1024


V_ROWS = 80
LOG2E = 1.4426950408889634


def _kvpost_kernel(k_ref, v_ref, tab_ref, kt_ref, vt_ref, kb_ref, vtb_ref, km_ref, *, tiles_per_seq):
    k = _rope(k_ref[...], tab_ref[...])
    k_t = k.T
    v_t = v_ref[...].T
    kt_ref[0] = k_t.reshape(C_HEADS, C_HD, KV_TM)
    vt_ref[0] = v_t.reshape(C_HEADS, C_HD, KV_TM)
    lane = lax.broadcasted_iota(jnp.int32, (KV_TM, BRANCH_W), 1)
    row = lax.broadcasted_iota(jnp.int32, (KV_TM, BRANCH_W), 0)
    blk = (pl.program_id(0) % tiles_per_seq) * (KV_TM // MOBA_BLOCK) + row // MOBA_BLOCK
    onehot = jnp.where(lane % C_HD == blk, 1.0, 0.0)
    odd_head = (lane // C_HD) % 2 == 1
    kb_ref[0] = jnp.where(odd_head, onehot, k).astype(BF16)
    kb_ref[1] = jnp.where(odd_head, k, onehot).astype(BF16)
    extra = jnp.where(lax.broadcasted_iota(jnp.int32, (V_ROWS - C_HD, MOBA_BLOCK), 0) == 0, 1.0, 0.0)
    for i in range(KV_TM // MOBA_BLOCK):
        rows = slice(i * MOBA_BLOCK, (i + 1) * MOBA_BLOCK)
        pieces = []
        for h in range(C_HEADS):
            pieces += [v_t[h * C_HD:(h + 1) * C_HD, rows], extra]
        vtb_ref[i] = jnp.concatenate(pieces, axis=0).astype(BF16)
        km_ref[0, i:i + 1, :] = jnp.sum(k[rows, :], axis=0, keepdims=True) * (1.0 / MOBA_BLOCK)


def _kvpost(parts, tab, seq):
    n = parts.shape[0]
    tiles_per_seq = seq // KV_TM
    blocks_per_tile = KV_TM // MOBA_BLOCK
    assert seq // MOBA_BLOCK <= C_HD
    blk = lambda c: pl.BlockSpec((KV_TM, BRANCH_W), lambda i: (i, c))
    slab = pl.BlockSpec((1, C_HEADS, C_HD, KV_TM), lambda i: (i // tiles_per_seq, 0, 0, i % tiles_per_seq))
    return pl.pallas_call(
        functools.partial(_kvpost_kernel, tiles_per_seq=tiles_per_seq),
        grid=(n // KV_TM,),
        in_specs=[blk(P_CK), blk(P_CV),
                  pl.BlockSpec((KV_TM, 384), lambda i: (i % tiles_per_seq, 0))],
        out_specs=[slab, slab,
                   pl.BlockSpec((2, KV_TM, BRANCH_W), lambda i: (0, i, 0)),
                   pl.BlockSpec((blocks_per_tile, C_HEADS * V_ROWS, MOBA_BLOCK), lambda i: (i, 0, 0)),
                   pl.BlockSpec((1, blocks_per_tile, BRANCH_W), lambda i: (i, 0, 0))],
        out_shape=[jax.ShapeDtypeStruct((n // seq, C_HEADS, C_HD, seq), F32),
                   jax.ShapeDtypeStruct((n // seq, C_HEADS, C_HD, seq), F32),
                   jax.ShapeDtypeStruct((2, n, BRANCH_W), BF16),
                   jax.ShapeDtypeStruct((n // MOBA_BLOCK, C_HEADS * V_ROWS, MOBA_BLOCK), BF16),
                   jax.ShapeDtypeStruct((n // KV_TM, blocks_per_tile, BRANCH_W), F32)],
        compiler_params=_cparams(("parallel",)),
        name="kvpost",
    )(parts, parts, tab)


MOBA_HEADS_PER_LOOP = 4


def _moba_prompt_kernel(q_ref, tab_ref, kb_ref, vtb_ref, km_ref, o_ref, qt_ref, m_ref, acc_ref):
    j = pl.program_id(1)
    tq = MOBA_BLOCK
    nb = km_ref.shape[1]
    q = _rope(q_ref[...], tab_ref[...])
    km = km_ref[0]
    blk_i = lax.broadcasted_iota(jnp.int32, (nb, tq), 0)
    dim_i = lax.broadcasted_iota(jnp.int32, (128, tq), 0)
    for g in range(C_HEADS // 2):
        lanes = slice(g * 128, (g + 1) * 128)
        qt = q[:, lanes].T
        for hh in range(2):
            h = 2 * g + hh
            qh = jnp.where((dim_i >= hh * C_HD) & (dim_i < (hh + 1) * C_HD), qt, 0.0)
            gate = jnp.dot(km[:, lanes], qh, precision=lax.Precision.HIGHEST, preferred_element_type=F32)
            sel = _top_blocks(gate, blk_i < j, 0)
            pen = jnp.where((sel > 0.0) | (blk_i == j), 0.0, NEG)
            pen = jnp.concatenate([pen, jnp.zeros((C_HD - nb, tq), F32)], axis=0)
            qs = qh * (C_HD ** -0.5 * LOG2E)
            full = (jnp.concatenate([qs[0:C_HD, :], pen], axis=0) if hh == 0
                    else jnp.concatenate([pen, qs[C_HD:2 * C_HD, :]], axis=0))
            qt_ref[h] = full.astype(BF16)
    m_ref[...] = jnp.full(m_ref.shape, NEG, F32)
    acc_ref[...] = jnp.zeros(acc_ref.shape, F32)

    def scores(heads, n):
        r0 = pl.multiple_of(n * MOBA_BLOCK, MOBA_BLOCK)
        out = []
        for h in heads:
            kblk = kb_ref[h % 2, pl.ds(r0, MOBA_BLOCK), (h // 2) * 128:(h // 2 + 1) * 128]
            out.append(jnp.dot(kblk, qt_ref[h], preferred_element_type=F32))
        return tuple(out)

    def accumulate(heads, n, s_list):
        for h, s in zip(heads, s_list):
            rows = slice(h * V_ROWS, (h + 1) * V_ROWS)
            m = m_ref[h:h + 1, :]
            m_new = jnp.maximum(m, jnp.max(s, axis=0, keepdims=True))
            a = jnp.exp2(m - m_new)
            p = jnp.exp2(s - m_new).astype(BF16)
            pv = jnp.dot(vtb_ref[n, rows, :], p, preferred_element_type=F32)
            acc_ref[rows, :] = a * acc_ref[rows, :] + pv
            m_ref[h:h + 1, :] = m_new

    key_i = lax.broadcasted_iota(jnp.int32, (MOBA_BLOCK, tq), 0)
    qry_i = lax.broadcasted_iota(jnp.int32, (MOBA_BLOCK, tq), 1)
    causal = key_i <= qry_i
    for h0 in range(0, C_HEADS, MOBA_HEADS_PER_LOOP):
        heads = tuple(range(h0, h0 + MOBA_HEADS_PER_LOOP))
        s_own = tuple(jnp.where(causal, s, NEG) for s in scores(heads, j))

        def body(n, s_prev, heads=heads):
            s_next = scores(heads, n)
            accumulate(heads, jnp.where(n == 0, j, n - 1), s_prev)
            return s_next

        s_last = lax.fori_loop(0, j, body, s_own)
        accumulate(heads, jnp.maximum(j - 1, 0), s_last)
    out_t = jnp.concatenate(
        [acc_ref[h * V_ROWS:h * V_ROWS + C_HD, :] / acc_ref[h * V_ROWS + C_HD:h * V_ROWS + C_HD + 1, :]
         for h in range(C_HEADS)], axis=0)
    o_ref[...] = out_t.T


def _moba_prompt(parts, tab, kb, vtb, kmean, batch, seq):
    n = parts.shape[0]
    nb = seq // MOBA_BLOCK
    return pl.pallas_call(
        _moba_prompt_kernel,
        grid=(batch, nb),
        in_specs=[
            pl.BlockSpec((MOBA_BLOCK, BRANCH_W), lambda b, j: (b * nb + j, P_CQ)),
            pl.BlockSpec((MOBA_BLOCK, 384), lambda b, j: (j, 0)),
            pl.BlockSpec((2, seq, BRANCH_W), lambda b, j: (0, b, 0)),
            pl.BlockSpec((nb, C_HEADS * V_ROWS, MOBA_BLOCK), lambda b, j: (b, 0, 0)),
            pl.BlockSpec((1, nb, BRANCH_W), lambda b, j: (b, 0, 0)),
        ],
        out_specs=pl.BlockSpec((MOBA_BLOCK, BRANCH_W), lambda b, j: (b * nb + j, 0)),
        out_shape=jax.ShapeDtypeStruct((n, BRANCH_W), F32),
        scratch_shapes=[pltpu.VMEM((C_HEADS, 128, MOBA_BLOCK), BF16),
                        pltpu.VMEM((C_HEADS, MOBA_BLOCK), F32),
                        pltpu.VMEM((C_HEADS * V_ROWS, MOBA_BLOCK), F32)],
        compiler_params=_cparams(("parallel", "arbitrary")),
        name="moba_prompt",
    )(parts, tab, kb, vtb, kmean)


def _tail(x, branches, gpre_ref, gpost_ref, wm_ref, bm_ref, wb_ref, wo_ref):
    h = _rms(x, gpre_ref[...]).astype(BF16)
    acc = None
    for n, br in enumerate(branches):
        cols = slice(n * D_MODEL, (n + 1) * D_MODEL)
        gate = _sigmoid(jnp.dot(h, wm_ref[:, cols], preferred_element_type=F32) + bm_ref[:, cols])
        proj = jnp.dot(br.astype(BF16), wb_ref[n], preferred_element_type=F32)
        acc = gate * proj if acc is None else acc + gate * proj
    y = jnp.dot(acc.astype(BF16), wo_ref[...], preferred_element_type=F32)
    return x + _rms(y, gpost_ref[...])


def _mem_attend(problems):
    lanes = lambda h: slice(h * M_HD, (h + 1) * M_HD)
    scores = [[lax.dot_general(mq[:, lanes(h)].astype(BF16), mk_head(h).astype(BF16), NT_DIMS,
                               preferred_element_type=F32) * (M_HD ** -0.5) for h in range(M_HEADS)]
              for mq, mk_head, _ in problems]
    probs = [[jnp.exp(s - jnp.max(s, axis=1, keepdims=True)) for s in per_head] for per_head in scores]
    return [jnp.concatenate(
        [jnp.dot(p.astype(BF16), mv_head(h).astype(BF16), preferred_element_type=F32)
         / jnp.sum(p, axis=1, keepdims=True) for h, p in enumerate(per_head)], axis=1)
        for (_, _, mv_head), per_head in zip(problems, probs)]


TAIL_TM = 512


def _prompt_tail_kernel(x_ref, au_ref, av_ref, ag_ref, bb_ref, bc_ref, bx_ref, bg_ref, cg_ref, mq_ref, mg_ref,
                        hc_ref, hx_ref, oc_ref, mk_ref, mv_ref,
                        gpre_ref, gpost_ref, lng_ref, ws_ref, bs_ref, cw_ref, wm_ref, bm_ref, wb_ref, wo_ref,
                        y_ref, zt_ref, *, tiles_per_seq):
    tm = TAIL_TM
    x = x_ref[...]
    vn = _layernorm(av_ref[...], lng_ref[...]).astype(BF16)
    t_i = lax.broadcasted_iota(jnp.int32, (CHUNK, CHUNK), 0)
    s_i = lax.broadcasted_iota(jnp.int32, (CHUNK, CHUNK), 1)
    sp_cols = []
    for g in range(A_GROUPS):
        lanes = slice(g * 128, (g + 1) * 128)
        ws = jnp.where(s_i <= t_i, ws_ref[g], 0.0).astype(BF16)
        bias = bs_ref[:, g:g + 1]
        rows = [jnp.dot(ws, vn[c * CHUNK:(c + 1) * CHUNK, lanes], preferred_element_type=F32) + bias
                for c in range(tm // CHUNK)]
        sp_cols.append(jnp.concatenate(rows, axis=0))
    br_a = _silu(ag_ref[...]) * (au_ref[...] * jnp.concatenate(sp_cols, axis=1))
    z = bc_ref[...] * bx_ref[...]
    first_of_seq = (pl.program_id(0) % tiles_per_seq) == 0
    halo = jnp.where(first_of_seq, 0.0, hc_ref[...] * hx_ref[...])
    rix = lax.broadcasted_iota(jnp.int32, (tm, BRANCH_W), 0)
    z1 = jnp.where(rix == 0, halo[7:8, :], pltpu.roll(z, 1, axis=0))
    z2 = pltpu.roll(z, 2, axis=0)
    z2 = jnp.where(rix == 0, halo[6:7, :], jnp.where(rix == 1, halo[7:8, :], z2))
    conv = cw_ref[0:1, :] * z2 + cw_ref[1:2, :] * z1 + cw_ref[2:3, :] * z
    br_b = _silu(bg_ref[...]) * (bb_ref[...] * conv)
    zt_ref[0] = z[tm - 8:tm, :]
    br_c = _silu(cg_ref[...]) * oc_ref[...]
    out_m, = _mem_attend([(mq_ref[...], lambda h: mk_ref[:, h * M_HD:(h + 1) * M_HD],
                           lambda h: mv_ref[:, h * M_HD:(h + 1) * M_HD])])
    br_m = _silu(mg_ref[...]) * out_m
    y_ref[...] = _tail(x, (br_a, br_b, br_c, br_m), gpre_ref, gpost_ref, wm_ref, bm_ref, wb_ref, wo_ref)


def _const_spec(shape):
    zeros = (0,) * len(shape)
    return pl.BlockSpec(shape, lambda i: zeros, pipeline_mode=pl.Buffered(1))


def _prompt_tail(x, parts, out_c, mkv, lw, seq):
    n = x.shape[0]
    tm = TAIL_TM
    tiles_per_seq = seq // tm
    part = lambda c: pl.BlockSpec((tm, BRANCH_W), lambda i: (i, c))
    halo = lambda c: pl.BlockSpec((8, BRANCH_W), lambda i: (jnp.maximum(i * (tm // 8) - 1, 0), c))
    row512 = pl.BlockSpec((tm, BRANCH_W), lambda i: (i, 0))
    in_specs = [
        pl.BlockSpec((tm, D_MODEL), lambda i: (i, 0)),
        part(P_AU), part(P_AV), part(P_AG), part(P_BB), part(P_BC), part(P_BX), part(P_BG),
        part(P_CG), part(P_MQ), part(P_MG),
        halo(P_BC), halo(P_BX),
        row512,
        pl.BlockSpec((N_MEM, BRANCH_W), lambda i: (i // tiles_per_seq, 0)),
        pl.BlockSpec((N_MEM, BRANCH_W), lambda i: (i // tiles_per_seq, 1)),
        _const_spec((1, D_MODEL)), _const_spec((1, D_MODEL)), _const_spec((1, BRANCH_W)),
        _const_spec((A_GROUPS, CHUNK, CHUNK)), _const_spec((CHUNK, A_GROUPS)), _const_spec((CONV_W, BRANCH_W)),
        _const_spec((D_MODEL, 4 * D_MODEL)), _const_spec((1, 4 * D_MODEL)),
        _const_spec((4, BRANCH_W, D_MODEL)), _const_spec((D_MODEL, D_MODEL)),
    ]
    return pl.pallas_call(
        functools.partial(_prompt_tail_kernel, tiles_per_seq=tiles_per_seq),
        grid=(n // tm,),
        in_specs=in_specs,
        out_specs=[pl.BlockSpec((tm, D_MODEL), lambda i: (i, 0)),
                   pl.BlockSpec((1, 8, BRANCH_W), lambda i: (i, 0, 0))],
        out_shape=[jax.ShapeDtypeStruct((n, D_MODEL), F32),
                   jax.ShapeDtypeStruct((n // tm, 8, BRANCH_W), F32)],
        compiler_params=_cparams(("parallel",)),
        name="prompt_tail",
    )(x, *([parts] * 10), parts, parts, out_c, mkv, mkv,
      lw["g_pre"], lw["g_post"], lw["ln_v_gain"], lw["w_spatial"], lw["b_spatial_t"], lw["conv_w"],
      lw["w_merge"], lw["b_merge"], lw["w_branch"], lw["w_out"])


N_PAGES = 16
PAGE = 128
PAGES_PER_BLOCK = MOBA_BLOCK // PAGE


SAMPLE_SEQS = 2


def _sample_branch_kernel(pt_ref, parts_ref, tab_ref, conv_ref, mk_ref, mv_ref, *rest, dec_seq):
    ns = SAMPLE_SEQS
    k_pages = [rest[s * N_PAGES:(s + 1) * N_PAGES] for s in range(ns)]
    v_pages = [rest[(ns + s) * N_PAGES:(ns + s + 1) * N_PAGES] for s in range(ns)]
    lng_ref, wsx_ref, bsx_ref, cw_ref, ko_ref, vo_ref, co_ref, vn_ref, br_ref = rest[2 * ns * N_PAGES:]
    del pt_ref
    r8 = SAMPLE_ROWS
    part = lambda s, c: parts_ref[s, :, c * BRANCH_W:(c + 1) * BRANCH_W]
    rix = lax.broadcasted_iota(jnp.int32, (r8, BRANCH_W), 0)

    for s in range(ns):
        vn = _layernorm(part(s, P_AV), lng_ref[...])
        vn_ref[s] = vn
        sp = bsx_ref[...]
        for i in range(dec_seq):
            sp = sp + jnp.where(rix >= i, wsx_ref[i], 0.0) * vn[i:i + 1, :]
        br_ref[s, :, 0:BRANCH_W] = _silu(part(s, P_AG)) * (part(s, P_AU) * sp)

        z = part(s, P_BC) * part(s, P_BX)
        prev = conv_ref[0, s]
        zrow = lambda i, prev=prev, z=z: (prev[i:i + 1, :] if i < CONV_W - 1
                                          else z[i - (CONV_W - 1):i - (CONV_W - 2), :])
        conv = jnp.zeros((r8, BRANCH_W), F32)
        for t in range(dec_seq):
            c_t = cw_ref[0:1, :] * zrow(t) + cw_ref[1:2, :] * zrow(t + 1) + cw_ref[2:3, :] * zrow(t + 2)
            conv = jnp.where(rix == t, c_t, conv)
        br_ref[s, :, BRANCH_W:2 * BRANCH_W] = _silu(part(s, P_BG)) * (part(s, P_BB) * conv)
        co_ref[s, 0] = z[dec_seq - (CONV_W - 1):dec_seq, :]

    tab = tab_ref[...]
    q = [_rope(part(s, P_CQ), tab) for s in range(ns)]
    k = [_rope(part(s, P_CK), tab) for s in range(ns)]
    v = [part(s, P_CV) for s in range(ns)]
    for s in range(ns):
        ko_ref[s] = k[s]
        vo_ref[s] = v[s]
    n_past = N_PAGES // PAGES_PER_BLOCK
    scale = C_HD ** -0.5
    hr = ns * C_HEADS * r8
    stack = lambda f: jnp.concatenate([f(s, h) for s in range(ns) for h in range(C_HEADS)], axis=0)
    head_lanes = lambda h: slice(h * C_HD, (h + 1) * C_HD)
    group = lambda x, s, h: x[(s * C_HEADS + h) * r8:(s * C_HEADS + h + 1) * r8, :]
    q_rows = stack(lambda s, h: q[s][:, head_lanes(h)]) * scale
    t_q = lax.broadcasted_iota(jnp.int32, (hr, 1), 0) % r8

    def head_block_t(pages, n, h):
        return jnp.concatenate([pages[n * PAGES_PER_BLOCK + i][0, 0, h] for i in range(PAGES_PER_BLOCK)], axis=1)

    s_own = []
    for c in range(dec_seq):
        k_c = stack(lambda s, h: jnp.broadcast_to(k[s][c:c + 1, head_lanes(h)], (r8, C_HD)))
        s_own.append(jnp.where(t_q >= c, jnp.sum(q_rows * k_c, axis=1, keepdims=True), NEG))
    q_hi = q_rows.astype(BF16).astype(F32)
    q_lo = q_rows - q_hi

    q2 = {(s, h): jnp.concatenate([group(q_hi, s, h), group(q_lo, s, h)], axis=0).astype(BF16)
          for s in range(ns) for h in range(C_HEADS)}

    def past_scores(s, h, n):
        s2 = jnp.dot(q2[s, h], head_block_t(k_pages[s], n, h).astype(BF16), preferred_element_type=F32)
        return s2[0:r8, :] + s2[r8:2 * r8, :]

    s_past = []
    gate = jnp.zeros((hr, GATE_COLS), F32)
    gate_col = lax.broadcasted_iota(jnp.int32, (hr, GATE_COLS), 1)
    for n in range(n_past):
        s_n = stack(lambda s, h: past_scores(s, h, n))
        s_past.append(s_n)
        gate = jnp.where(gate_col == n, jnp.sum(s_n, axis=1, keepdims=True) * (1.0 / MOBA_BLOCK), gate)
    sel = _top_blocks(gate, gate_col < n_past, 1)
    s_past = [jnp.where(sel[:, n:n + 1] > 0.0, s_past[n], NEG) for n in range(n_past)]
    m = s_own[0]
    for s_c in s_own[1:]:
        m = jnp.maximum(m, s_c)
    for s_n in s_past:
        m = jnp.maximum(m, jnp.max(s_n, axis=1, keepdims=True))
    l = jnp.zeros((hr, 1), F32)
    acc = jnp.zeros((hr, C_HD), F32)
    for c in range(dec_seq):
        v_c = stack(lambda s, h: jnp.broadcast_to(v[s][c:c + 1, head_lanes(h)], (r8, C_HD)))
        p = jnp.exp(s_own[c] - m)
        l = l + p
        acc = acc + p * v_c
    for n in range(n_past):
        p = jnp.exp(s_past[n] - m)
        l = l + jnp.sum(p, axis=1, keepdims=True)
        p = p.astype(BF16)
        acc = acc + stack(lambda s, h: lax.dot_general(group(p, s, h), head_block_t(v_pages[s], n, h).astype(BF16),
                                                       NT_DIMS, preferred_element_type=F32))
    acc = acc / l

    out_m = _mem_attend([(part(s, P_MQ),
                          lambda h, s=s: mk_ref[0, s, pl.ds(h, N_MEM, stride=M_HEADS), :],
                          lambda h, s=s: mv_ref[0, s, pl.ds(h, N_MEM, stride=M_HEADS), :]) for s in range(ns)])
    for s in range(ns):
        out_c = jnp.concatenate([group(acc, s, h) for h in range(C_HEADS)], axis=1)
        br_ref[s, :, 2 * BRANCH_W:3 * BRANCH_W] = _silu(part(s, P_CG)) * out_c
        br_ref[s, :, 3 * BRANCH_W:4 * BRANCH_W] = _silu(part(s, P_MG)) * out_m[s]


def _sample_branch(layer, parts3, tab, state_conv, cache_k, cache_v, cache_mem_k, cache_mem_v, page_table, lw,
                   dec_seq):
    bs = parts3.shape[0]
    r8 = SAMPLE_ROWS
    ns = SAMPLE_SEQS
    page_spec = lambda s, p: pl.BlockSpec((1, 1, C_HEADS, C_HD, PAGE),
                                          lambda b, pt: (layer, pt[ns * b + s, p], 0, 0, 0))
    page_specs = [page_spec(s, p) for s in range(ns) for p in range(N_PAGES)]
    const = lambda shape: pl.BlockSpec(shape, lambda b, pt: (0,) * len(shape))
    seq_spec = lambda rows, w: pl.BlockSpec((ns, rows, w), lambda b, pt: (b, 0, 0))
    in_specs = [
        seq_spec(r8, IN_W),
        const((r8, 384)),
        pl.BlockSpec((1, ns, CONV_W - 1, BRANCH_W), lambda b, pt: (layer, b, 0, 0)),
        pl.BlockSpec((1, ns, N_MEM * M_HEADS, M_HD), lambda b, pt: (layer, b, 0, 0)),
        pl.BlockSpec((1, ns, N_MEM * M_HEADS, M_HD), lambda b, pt: (layer, b, 0, 0)),
    ] + page_specs + page_specs + [
        const((1, BRANCH_W)), const((dec_seq, r8, BRANCH_W)), const((r8, BRANCH_W)), const((CONV_W, BRANCH_W)),
    ]
    out_specs = [seq_spec(r8, BRANCH_W), seq_spec(r8, BRANCH_W),
                 pl.BlockSpec((ns, 1, CONV_W - 1, BRANCH_W), lambda b, pt: (b, 0, 0, 0)),
                 seq_spec(r8, BRANCH_W), seq_spec(r8, 4 * BRANCH_W)]
    out_shape = [jax.ShapeDtypeStruct((bs, r8, BRANCH_W), F32), jax.ShapeDtypeStruct((bs, r8, BRANCH_W), F32),
                 jax.ShapeDtypeStruct((bs, 1, CONV_W - 1, BRANCH_W), F32),
                 jax.ShapeDtypeStruct((bs, r8, BRANCH_W), F32), jax.ShapeDtypeStruct((bs, r8, 4 * BRANCH_W), F32)]
    grid_spec = pltpu.PrefetchScalarGridSpec(
        num_scalar_prefetch=1, grid=(bs // ns,), in_specs=in_specs, out_specs=out_specs)
    return pl.pallas_call(
        functools.partial(_sample_branch_kernel, dec_seq=dec_seq),
        grid_spec=grid_spec,
        out_shape=out_shape,
        compiler_params=_cparams(("arbitrary",)),
        name="sample_branch",
    )(page_table, parts3, tab, state_conv, cache_mem_k, cache_mem_v,
      *([cache_k] * (ns * N_PAGES)), *([cache_v] * (ns * N_PAGES)),
      lw["ln_v_gain"], lw["w_spatial_x"], lw["b_spatial_x"], lw["conv_w"])


def _sample_tail_kernel(x_ref, br_ref, gpre_ref, gpost_ref, wm_ref, bm_ref, wb_ref, wo_ref, y_ref):
    branches = tuple(br_ref[:, n * BRANCH_W:(n + 1) * BRANCH_W] for n in range(4))
    y_ref[...] = _tail(x_ref[...], branches, gpre_ref, gpost_ref, wm_ref, bm_ref, wb_ref, wo_ref)


def _sample_tail(x, branches, lw):
    n = x.shape[0]
    tm = TAIL_TM
    return pl.pallas_call(
        _sample_tail_kernel,
        grid=(n // tm,),
        in_specs=[
            pl.BlockSpec((tm, D_MODEL), lambda i: (i, 0)),
            pl.BlockSpec((tm, 4 * BRANCH_W), lambda i: (i, 0)),
            _const_spec((1, D_MODEL)), _const_spec((1, D_MODEL)),
            _const_spec((D_MODEL, 4 * D_MODEL)), _const_spec((1, 4 * D_MODEL)),
            _const_spec((4, BRANCH_W, D_MODEL)), _const_spec((D_MODEL, D_MODEL)),
        ],
        out_specs=pl.BlockSpec((tm, D_MODEL), lambda i: (i, 0)),
        out_shape=jax.ShapeDtypeStruct((n, D_MODEL), F32),
        compiler_params=_cparams(("parallel",)),
        name="sample_tail",
    )(x, branches, lw["g_pre"], lw["g_post"], lw["w_merge"], lw["b_merge"], lw["w_branch"], lw["w_out"])


def kernel(x_prompt, x_sample, cache_k, cache_v, cache_mem_k, cache_mem_v, state_conv, page_table, mem_prompt,
           g_pre, g_post, w_in, ln_v_gain, w_spatial, b_spatial, conv_w, g_mem, w_mem_kv, w_merge, b_merge,
           w_branch, w_out):
    bp, tp, d = x_prompt.shape
    bs, ts, _ = x_sample.shape
    depth = w_in.shape[0]
    page = cache_k.shape[2]
    past_len = page_table.shape[1] * page
    assert (d, page, page_table.shape[1]) == (D_MODEL, PAGE, N_PAGES) and ts <= SAMPLE_ROWS
    assert past_len % MOBA_BLOCK == 0 and tp % KV_TM == 0 and ts >= CONV_W - 1
    r8 = SAMPLE_ROWS

    tab_p = _rope_table(jnp.arange(tp, dtype=jnp.int32))
    tab_s = _rope_table(past_len + jnp.arange(r8, dtype=jnp.int32))
    xp = x_prompt.reshape(bp * tp, d)
    xs = jnp.pad(x_sample, ((0, 0), (0, r8 - ts), (0, 0))).reshape(bs * r8, d)
    mem = mem_prompt.reshape(bp * N_MEM, d)
    ck = jnp.transpose(cache_k, (0, 1, 3, 4, 2))
    cv = jnp.transpose(cache_v, (0, 1, 3, 4, 2))
    cmk = cache_mem_k.reshape(depth, bs, N_MEM * M_HEADS, M_HD)
    cmv = cache_mem_v.reshape(depth, bs, N_MEM * M_HEADS, M_HD)
    lane_group = jnp.arange(BRANCH_W) // (BRANCH_W // A_GROUPS)

    outs = {name: [] for name in ("kp", "vp", "cp", "mkp", "mvp", "ks", "vs", "cs", "vns")}
    for l in range(depth):
        wsx = jnp.transpose(w_spatial[l][:, :r8, :ts], (2, 1, 0))[:, :, lane_group]
        bsx = jnp.transpose(b_spatial[l][:, :r8], (1, 0))[:, lane_group]
        lw = {
            "g_pre": g_pre[l].reshape(1, d), "g_post": g_post[l].reshape(1, d),
            "ln_v_gain": ln_v_gain[l].reshape(1, BRANCH_W),
            "w_spatial": w_spatial[l], "b_spatial_t": b_spatial[l].T,
            "w_spatial_x": wsx, "b_spatial_x": bsx,
            "conv_w": conv_w[l],
            "w_merge": w_merge[l].astype(BF16), "b_merge": b_merge[l].reshape(1, 4 * d),
            "w_branch": w_branch[l].astype(BF16), "w_out": w_out[l].astype(BF16),
        }
        w_in_b = w_in[l].astype(BF16)

        mkv = _inproj(mem, g_mem[l], w_mem_kv[l].astype(BF16), 1024, 1024)
        parts = _inproj(xp, g_pre[l], w_in_b, 1024, 3328)
        k_p, v_p, kb, vb, kmean = _kvpost(parts, tab_p, tp)
        kmean = kmean.reshape(bp, tp // MOBA_BLOCK, BRANCH_W)
        out_c = _moba_prompt(parts, tab_p, kb, vb, kmean, bp, tp)
        xp, ztail = _prompt_tail(xp, parts, out_c, mkv, lw, tp)
        outs["kp"].append(jnp.transpose(k_p, (0, 3, 1, 2)))
        outs["vp"].append(jnp.transpose(v_p, (0, 3, 1, 2)))
        outs["cp"].append(ztail.reshape(bp, tp // TAIL_TM, 8, BRANCH_W)[:, -1, 8 - (CONV_W - 1):, :])
        outs["mkp"].append(mkv[:, :BRANCH_W].reshape(bp, N_MEM, M_HEADS, M_HD))
        outs["mvp"].append(mkv[:, BRANCH_W:].reshape(bp, N_MEM, M_HEADS, M_HD))

        parts_s = _inproj(xs, g_pre[l], w_in_b, bs * r8, 1664).reshape(bs, r8, IN_W)
        k_s, v_s, c_s, vn_s, br_s = _sample_branch(l, parts_s, tab_s, state_conv, ck, cv, cmk, cmv, page_table,
                                                   lw, ts)
        xs = _sample_tail(xs, br_s.reshape(bs * r8, 4 * BRANCH_W), lw)
        outs["ks"].append(k_s[:, :ts].reshape(bs, ts, C_HEADS, C_HD))
        outs["vs"].append(v_s[:, :ts].reshape(bs, ts, C_HEADS, C_HD))
        outs["cs"].append(c_s.reshape(bs, CONV_W - 1, BRANCH_W))
        outs["vns"].append(vn_s[:, :ts])

    st = lambda name: jnp.stack(outs[name])
    return (xp.reshape(bp, tp, d), xs.reshape(bs, r8, d)[:, :ts],
            st("kp"), st("vp"), st("cp"), st("mkp"), st("mvp"),
            st("ks"), st("vs"), st("cs"), st("vns"))
```

```python
import functools

import jax
import jax.numpy as jnp
from jax import lax
from jax.experimental import pallas as pl
from jax.experimental.pallas import tpu as pltpu

D_MODEL = 1024
BRANCH_W = 512
N_PARTS = 13
IN_W = N_PARTS * BRANCH_W
CHUNK = 128
A_GROUPS = 4
CONV_W = 3
C_HEADS = 8
C_HD = 64
ROPE_DIM = 16
ROPE_THETA = 500000.0
MOBA_BLOCK = 256
MOBA_TOPK = 3
M_HEADS = 4
M_HD = 128
N_MEM = 256
EPS = 1e-6

P_AU, P_AV, P_AG, P_BB, P_BC, P_BX, P_BG, P_CQ, P_CK, P_CV, P_CG, P_MQ, P_MG = range(13)

SAMPLE_ROWS = 8
NEG = -1e30
GATE_COLS = 128
VMEM_LIMIT = 56 * 1024 * 1024

F32 = jnp.float32
BF16 = jnp.bfloat16
NT_DIMS = (((1,), (1,)), ((), ()))


def _cparams(sem):
    return pltpu.CompilerParams(dimension_semantics=sem, vmem_limit_bytes=VMEM_LIMIT)


def _rms(x, g):
    r = lax.rsqrt(jnp.mean(x * x, axis=-1, keepdims=True) + EPS)
    return (x * r) * g


def _layernorm(x, g):
    mu = jnp.mean(x, axis=-1, keepdims=True)
    xc = x - mu
    r = lax.rsqrt(jnp.mean(xc * xc, axis=-1, keepdims=True) + EPS)
    return (xc * r) * g


def _sigmoid(x):
    return 1.0 / (1.0 + jnp.exp(-x))


def _silu(x):
    return x * _sigmoid(x)


def _rope(x, tab):
    c, s_lo, s_hi = tab[:, 0:128], tab[:, 128:256], tab[:, 256:384]
    half = ROPE_DIM // 2
    outs = []
    for g in range(BRANCH_W // 128):
        xg = x[:, g * 128:(g + 1) * 128]
        outs.append(xg * c + pltpu.roll(xg, 128 - half, axis=1) * s_lo + pltpu.roll(xg, half, axis=1) * s_hi)
    return jnp.concatenate(outs, axis=1)


def _rope_table(pos):
    half = ROPE_DIM // 2
    inv = jnp.power(jnp.float32(ROPE_THETA), -jnp.arange(half, dtype=F32) * (2.0 / ROPE_DIM))
    ang = pos.astype(F32)[:, None] * inv[None, :]
    cos, sin = jnp.cos(ang), jnp.sin(ang)
    d = jnp.arange(128) % C_HD
    idx = d % half
    cosl = jnp.where(d[None, :] < ROPE_DIM, cos[:, idx], 1.0)
    s_lo = jnp.where(d[None, :] < half, -sin[:, idx], 0.0)
    s_hi = jnp.where((d[None, :] >= half) & (d[None, :] < ROPE_DIM), sin[:, idx], 0.0)
    return jnp.concatenate([cosl, s_lo, s_hi], axis=1).astype(F32)


def _top_blocks(gate, valid, axis):
    idx = lax.broadcasted_iota(jnp.int32, gate.shape, axis).astype(F32)
    g = jnp.where(valid, gate, -jnp.inf)
    sel = jnp.zeros(gate.shape, F32)
    for _ in range(MOBA_TOPK):
        m = jnp.max(g, axis=axis, keepdims=True)
        cand = (g == m) & valid & (sel == 0.0)
        first = jnp.min(jnp.where(cand, idx, float(gate.shape[axis])), axis=axis, keepdims=True)
        pick = idx == first
        sel = jnp.where(pick, 1.0, sel)
        g = jnp.where(pick, -jnp.inf, g)
    return sel


def _inproj_kernel(x_ref, g_ref, w_ref, o_ref, h_ref):
    @pl.when(pl.program_id(1) == 0)
    def _():
        h_ref[...] = _rms(x_ref[...], g_ref[...]).astype(BF16)

    o_ref[...] = jnp.dot(h_ref[...], w_ref[...], preferred_element_type=F32)


def _inproj(x, g, w_bf16, tm, tn):
    n, d = x.shape
    n_out = w_bf16.shape[1]
    return pl.pallas_call(
        _inproj_kernel,
        grid=(n // tm, n_out // tn),
        in_specs=[
            pl.BlockSpec((tm, d), lambda i, j: (i, 0)),
            pl.BlockSpec((1, d), lambda i, j: (0, 0)),
            pl.BlockSpec((d, tn), lambda i, j: (0, j)),
        ],
        out_specs=pl.BlockSpec((tm, tn), lambda i, j: (i, j)),
        out_shape=jax.ShapeDtypeStruct((n, n_out), F32),
        scratch_shapes=[pltpu.VMEM((tm, d), BF16)],
        compiler_params=_cparams(("parallel", "arbitrary")),
        name="inproj",
    )(x, g.reshape(1, d), w_bf16)


KV_TM = 1024


V_ROWS = 80
LOG2E = 1.4426950408889634


def _kvpost_kernel(k_ref, v_ref, tab_ref, kt_ref, vt_ref, kb_ref, vtb_ref, km_ref, *, tiles_per_seq):
    k = _rope(k_ref[...], tab_ref[...])
    k_t = k.T
    v_t = v_ref[...].T
    kt_ref[0] = k_t.reshape(C_HEADS, C_HD, KV_TM)
    vt_ref[0] = v_t.reshape(C_HEADS, C_HD, KV_TM)
    lane = lax.broadcasted_iota(jnp.int32, (KV_TM, BRANCH_W), 1)
    row = lax.broadcasted_iota(jnp.int32, (KV_TM, BRANCH_W), 0)
    blk = (pl.program_id(0) % tiles_per_seq) * (KV_TM // MOBA_BLOCK) + row // MOBA_BLOCK
    onehot = jnp.where(lane % C_HD == blk, 1.0, 0.0)
    odd_head = (lane // C_HD) % 2 == 1
    kb_ref[0] = jnp.where(odd_head, onehot, k).astype(BF16)
    kb_ref[1] = jnp.where(odd_head, k, onehot).astype(BF16)
    extra = jnp.where(lax.broadcasted_iota(jnp.int32, (V_ROWS - C_HD, MOBA_BLOCK), 0) == 0, 1.0, 0.0)
    for i in range(KV_TM // MOBA_BLOCK):
        rows = slice(i * MOBA_BLOCK, (i + 1) * MOBA_BLOCK)
        pieces = []
        for h in range(C_HEADS):
            pieces += [v_t[h * C_HD:(h + 1) * C_HD, rows], extra]
        vtb_ref[i] = jnp.concatenate(pieces, axis=0).astype(BF16)
        km_ref[0, i:i + 1, :] = jnp.sum(k[rows, :], axis=0, keepdims=True) * (1.0 / MOBA_BLOCK)


def _kvpost(parts, tab, seq):
    n = parts.shape[0]
    tiles_per_seq = seq // KV_TM
    blocks_per_tile = KV_TM // MOBA_BLOCK
    assert seq // MOBA_BLOCK <= C_HD
    blk = lambda c: pl.BlockSpec((KV_TM, BRANCH_W), lambda i: (i, c))
    slab = pl.BlockSpec((1, C_HEADS, C_HD, KV_TM), lambda i: (i // tiles_per_seq, 0, 0, i % tiles_per_seq))
    return pl.pallas_call(
        functools.partial(_kvpost_kernel, tiles_per_seq=tiles_per_seq),
        grid=(n // KV_TM,),
        in_specs=[blk(P_CK), blk(P_CV),
                  pl.BlockSpec((KV_TM, 384), lambda i: (i % tiles_per_seq, 0))],
        out_specs=[slab, slab,
                   pl.BlockSpec((2, KV_TM, BRANCH_W), lambda i: (0, i, 0)),
                   pl.BlockSpec((blocks_per_tile, C_HEADS * V_ROWS, MOBA_BLOCK), lambda i: (i, 0, 0)),
                   pl.BlockSpec((1, blocks_per_tile, BRANCH_W), lambda i: (i, 0, 0))],
        out_shape=[jax.ShapeDtypeStruct((n // seq, C_HEADS, C_HD, seq), F32),
                   jax.ShapeDtypeStruct((n // seq, C_HEADS, C_HD, seq), F32),
                   jax.ShapeDtypeStruct((2, n, BRANCH_W), BF16),
                   jax.ShapeDtypeStruct((n // MOBA_BLOCK, C_HEADS * V_ROWS, MOBA_BLOCK), BF16),
                   jax.ShapeDtypeStruct((n // KV_TM, blocks_per_tile, BRANCH_W), F32)],
        compiler_params=_cparams(("parallel",)),
        name="kvpost",
    )(parts, parts, tab)


MOBA_HEADS_PER_LOOP = 4


def _moba_prompt_kernel(q_ref, tab_ref, kb_ref, vtb_ref, km_ref, o_ref, qt_ref, m_ref, acc_ref, sa_ref, sb_ref):
    j = pl.program_id(1)
    tq = MOBA_BLOCK
    nb = km_ref.shape[1]
    q = _rope(q_ref[...], tab_ref[...])
    km = km_ref[0]
    blk_i = lax.broadcasted_iota(jnp.int32, (nb, tq), 0)
    dim_i = lax.broadcasted_iota(jnp.int32, (128, tq), 0)
    for g in range(C_HEADS // 2):
        lanes = slice(g * 128, (g + 1) * 128)
        qt = q[:, lanes].T
        for hh in range(2):
            h = 2 * g + hh
            qh = jnp.where((dim_i >= hh * C_HD) & (dim_i < (hh + 1) * C_HD), qt, 0.0)
            gate = jnp.dot(km[:, lanes], qh, precision=lax.Precision.HIGHEST, preferred_element_type=F32)
            sel = _top_blocks(gate, blk_i < j, 0)
            pen = jnp.where((sel > 0.0) | (blk_i == j), 0.0, NEG)
            pen = jnp.concatenate([pen, jnp.zeros((C_HD - nb, tq), F32)], axis=0)
            qs = qh * (C_HD ** -0.5 * LOG2E)
            full = (jnp.concatenate([qs[0:C_HD, :], pen], axis=0) if hh == 0
                    else jnp.concatenate([pen, qs[C_HD:2 * C_HD, :]], axis=0))
            qt_ref[h] = full.astype(BF16)
    m_ref[...] = jnp.full(m_ref.shape, NEG, F32)
    acc_ref[...] = jnp.zeros(acc_ref.shape, F32)

    def scores(heads, n, dst_ref, mask_fn=None):
        r0 = pl.multiple_of(n * MOBA_BLOCK, MOBA_BLOCK)
        for i, h in enumerate(heads):
            kblk = kb_ref[h % 2, pl.ds(r0, MOBA_BLOCK), (h // 2) * 128:(h // 2 + 1) * 128]
            s = jnp.dot(kblk, qt_ref[h], preferred_element_type=F32)
            dst_ref[i] = s if mask_fn is None else mask_fn(s)

    def accumulate(heads, n, src_ref, weight=None):
        for i, h in enumerate(heads):
            rows = slice(h * V_ROWS, (h + 1) * V_ROWS)
            m = m_ref[h:h + 1, :]
            m_new = jnp.maximum(m, jnp.max(src_ref[i], axis=0, keepdims=True))
            a = jnp.exp2(m - m_new)
            p = jnp.exp2(src_ref[i] - m_new).astype(BF16)
            pv = jnp.dot(vtb_ref[n, rows, :], p, preferred_element_type=F32)
            if weight is not None:
                pv = pv * weight
            acc_ref[rows, :] = a * acc_ref[rows, :] + pv
            m_ref[h:h + 1, :] = m_new

    key_i = lax.broadcasted_iota(jnp.int32, (MOBA_BLOCK, tq), 0)
    qry_i = lax.broadcasted_iota(jnp.int32, (MOBA_BLOCK, tq), 1)
    causal = key_i <= qry_i
    for h0 in range(0, C_HEADS, MOBA_HEADS_PER_LOOP):
        heads = tuple(range(h0, h0 + MOBA_HEADS_PER_LOOP))
        scores(heads, j, sa_ref, lambda s: jnp.where(causal, s, NEG))

        def body(i, carry, heads=heads):
            n0 = 2 * i
            n1 = n0 + 1
            scores(heads, n0, sb_ref)
            accumulate(heads, jnp.where(i == 0, j, n0 - 1), sa_ref)
            n1c = jnp.minimum(n1, j - 1)
            scores(heads, n1c, sa_ref)
            accumulate(heads, n0, sb_ref)
            return carry

        trips = (j + 1) // 2
        lax.fori_loop(0, trips, body, 0)
        last_valid = jnp.where(j % 2 == 1, 0.0, 1.0)
        accumulate(heads, jnp.where(j == 0, j, j - 1), sa_ref, weight=last_valid)
    out_t = jnp.concatenate(
        [acc_ref[h * V_ROWS:h * V_ROWS + C_HD, :] / acc_ref[h * V_ROWS + C_HD:h * V_ROWS + C_HD + 1, :]
         for h in range(C_HEADS)], axis=0)
    o_ref[...] = out_t.T


def _moba_prompt(parts, tab, kb, vtb, kmean, batch, seq):
    n = parts.shape[0]
    nb = seq // MOBA_BLOCK
    return pl.pallas_call(
        _moba_prompt_kernel,
        grid=(batch, nb),
        in_specs=[
            pl.BlockSpec((MOBA_BLOCK, BRANCH_W), lambda b, j: (b * nb + j, P_CQ)),
            pl.BlockSpec((MOBA_BLOCK, 384), lambda b, j: (j, 0)),
            pl.BlockSpec((2, seq, BRANCH_W), lambda b, j: (0, b, 0)),
            pl.BlockSpec((nb, C_HEADS * V_ROWS, MOBA_BLOCK), lambda b, j: (b, 0, 0)),
            pl.BlockSpec((1, nb, BRANCH_W), lambda b, j: (b, 0, 0)),
        ],
        out_specs=pl.BlockSpec((MOBA_BLOCK, BRANCH_W), lambda b, j: (b * nb + j, 0)),
        out_shape=jax.ShapeDtypeStruct((n, BRANCH_W), F32),
        scratch_shapes=[pltpu.VMEM((C_HEADS, 128, MOBA_BLOCK), BF16),
                        pltpu.VMEM((C_HEADS, MOBA_BLOCK), F32),
                        pltpu.VMEM((C_HEADS * V_ROWS, MOBA_BLOCK), F32),
                        pltpu.VMEM((MOBA_HEADS_PER_LOOP, MOBA_BLOCK, MOBA_BLOCK), F32),
                        pltpu.VMEM((MOBA_HEADS_PER_LOOP, MOBA_BLOCK, MOBA_BLOCK), F32)],
        compiler_params=_cparams(("parallel", "arbitrary")),
        name="moba_prompt",
    )(parts, tab, kb, vtb, kmean)


def _tail(x, branches, gpre_ref, gpost_ref, wm_ref, bm_ref, wb_ref, wo_ref):
    h = _rms(x, gpre_ref[...]).astype(BF16)
    acc = None
    for n, br in enumerate(branches):
        cols = slice(n * D_MODEL, (n + 1) * D_MODEL)
        gate = _sigmoid(jnp.dot(h, wm_ref[:, cols], preferred_element_type=F32) + bm_ref[:, cols])
        proj = jnp.dot(br.astype(BF16), wb_ref[n], preferred_element_type=F32)
        acc = gate * proj if acc is None else acc + gate * proj
    y = jnp.dot(acc.astype(BF16), wo_ref[...], preferred_element_type=F32)
    return x + _rms(y, gpost_ref[...])


def _mem_attend(problems):
    lanes = lambda h: slice(h * M_HD, (h + 1) * M_HD)
    scores = [[lax.dot_general(mq[:, lanes(h)].astype(BF16), mk_head(h).astype(BF16), NT_DIMS,
                               preferred_element_type=F32) * (M_HD ** -0.5) for h in range(M_HEADS)]
              for mq, mk_head, _ in problems]
    probs = [[jnp.exp(s - jnp.max(s, axis=1, keepdims=True)) for s in per_head] for per_head in scores]
    return [jnp.concatenate(
        [jnp.dot(p.astype(BF16), mv_head(h).astype(BF16), preferred_element_type=F32)
         / jnp.sum(p, axis=1, keepdims=True) for h, p in enumerate(per_head)], axis=1)
        for (_, _, mv_head), per_head in zip(problems, probs)]


TAIL_TM = 512


def _prompt_tail_kernel(x_ref, au_ref, av_ref, ag_ref, bb_ref, bc_ref, bx_ref, bg_ref, cg_ref, mq_ref, mg_ref,
                        hc_ref, hx_ref, oc_ref, mk_ref, mv_ref,
                        gpre_ref, gpost_ref, lng_ref, ws_ref, bs_ref, cw_ref, wm_ref, bm_ref, wb_ref, wo_ref,
                        y_ref, zt_ref, *, tiles_per_seq):
    tm = TAIL_TM
    x = x_ref[...]
    vn = _layernorm(av_ref[...], lng_ref[...]).astype(BF16)
    t_i = lax.broadcasted_iota(jnp.int32, (CHUNK, CHUNK), 0)
    s_i = lax.broadcasted_iota(jnp.int32, (CHUNK, CHUNK), 1)
    sp_cols = []
    for g in range(A_GROUPS):
        lanes = slice(g * 128, (g + 1) * 128)
        ws = jnp.where(s_i <= t_i, ws_ref[g], 0.0).astype(BF16)
        bias = bs_ref[:, g:g + 1]
        rows = [jnp.dot(ws, vn[c * CHUNK:(c + 1) * CHUNK, lanes], preferred_element_type=F32) + bias
                for c in range(tm // CHUNK)]
        sp_cols.append(jnp.concatenate(rows, axis=0))
    br_a = _silu(ag_ref[...]) * (au_ref[...] * jnp.concatenate(sp_cols, axis=1))
    z = bc_ref[...] * bx_ref[...]
    first_of_seq = (pl.program_id(0) % tiles_per_seq) == 0
    halo = jnp.where(first_of_seq, 0.0, hc_ref[...] * hx_ref[...])
    rix = lax.broadcasted_iota(jnp.int32, (tm, BRANCH_W), 0)
    z1 = jnp.where(rix == 0, halo[7:8, :], pltpu.roll(z, 1, axis=0))
    z2 = pltpu.roll(z, 2, axis=0)
    z2 = jnp.where(rix == 0, halo[6:7, :], jnp.where(rix == 1, halo[7:8, :], z2))
    conv = cw_ref[0:1, :] * z2 + cw_ref[1:2, :] * z1 + cw_ref[2:3, :] * z
    br_b = _silu(bg_ref[...]) * (bb_ref[...] * conv)
    zt_ref[0] = z[tm - 8:tm, :]
    br_c = _silu(cg_ref[...]) * oc_ref[...]
    out_m, = _mem_attend([(mq_ref[...], lambda h: mk_ref[:, h * M_HD:(h + 1) * M_HD],
                           lambda h: mv_ref[:, h * M_HD:(h + 1) * M_HD])])
    br_m = _silu(mg_ref[...]) * out_m
    y_ref[...] = _tail(x, (br_a, br_b, br_c, br_m), gpre_ref, gpost_ref, wm_ref, bm_ref, wb_ref, wo_ref)


def _const_spec(shape):
    zeros = (0,) * len(shape)
    return pl.BlockSpec(shape, lambda i: zeros, pipeline_mode=pl.Buffered(1))


def _prompt_tail(x, parts, out_c, mkv, lw, seq):
    n = x.shape[0]
    tm = TAIL_TM
    tiles_per_seq = seq // tm
    part = lambda c: pl.BlockSpec((tm, BRANCH_W), lambda i: (i, c))
    halo = lambda c: pl.BlockSpec((8, BRANCH_W), lambda i: (jnp.maximum(i * (tm // 8) - 1, 0), c))
    row512 = pl.BlockSpec((tm, BRANCH_W), lambda i: (i, 0))
    in_specs = [
        pl.BlockSpec((tm, D_MODEL), lambda i: (i, 0)),
        part(P_AU), part(P_AV), part(P_AG), part(P_BB), part(P_BC), part(P_BX), part(P_BG),
        part(P_CG), part(P_MQ), part(P_MG),
        halo(P_BC), halo(P_BX),
        row512,
        pl.BlockSpec((N_MEM, BRANCH_W), lambda i: (i // tiles_per_seq, 0)),
        pl.BlockSpec((N_MEM, BRANCH_W), lambda i: (i // tiles_per_seq, 1)),
        _const_spec((1, D_MODEL)), _const_spec((1, D_MODEL)), _const_spec((1, BRANCH_W)),
        _const_spec((A_GROUPS, CHUNK, CHUNK)), _const_spec((CHUNK, A_GROUPS)), _const_spec((CONV_W, BRANCH_W)),
        _const_spec((D_MODEL, 4 * D_MODEL)), _const_spec((1, 4 * D_MODEL)),
        _const_spec((4, BRANCH_W, D_MODEL)), _const_spec((D_MODEL, D_MODEL)),
    ]
    return pl.pallas_call(
        functools.partial(_prompt_tail_kernel, tiles_per_seq=tiles_per_seq),
        grid=(n // tm,),
        in_specs=in_specs,
        out_specs=[pl.BlockSpec((tm, D_MODEL), lambda i: (i, 0)),
                   pl.BlockSpec((1, 8, BRANCH_W), lambda i: (i, 0, 0))],
        out_shape=[jax.ShapeDtypeStruct((n, D_MODEL), F32),
                   jax.ShapeDtypeStruct((n // tm, 8, BRANCH_W), F32)],
        compiler_params=_cparams(("parallel",)),
        name="prompt_tail",
    )(x, *([parts] * 10), parts, parts, out_c, mkv, mkv,
      lw["g_pre"], lw["g_post"], lw["ln_v_gain"], lw["w_spatial"], lw["b_spatial_t"], lw["conv_w"],
      lw["w_merge"], lw["b_merge"], lw["w_branch"], lw["w_out"])


N_PAGES = 16
PAGE = 128
PAGES_PER_BLOCK = MOBA_BLOCK // PAGE


SAMPLE_SEQS = 2


def _sample_branch_kernel(pt_ref, parts_ref, tab_ref, conv_ref, mk_ref, mv_ref, *rest, dec_seq):
    ns = SAMPLE_SEQS
    k_pages = [rest[s * N_PAGES:(s + 1) * N_PAGES] for s in range(ns)]
    v_pages = [rest[(ns + s) * N_PAGES:(ns + s + 1) * N_PAGES] for s in range(ns)]
    lng_ref, wsx_ref, bsx_ref, cw_ref, ko_ref, vo_ref, co_ref, vn_ref, br_ref = rest[2 * ns * N_PAGES:]
    del pt_ref
    r8 = SAMPLE_ROWS
    part = lambda s, c: parts_ref[s, :, c * BRANCH_W:(c + 1) * BRANCH_W]
    rix = lax.broadcasted_iota(jnp.int32, (r8, BRANCH_W), 0)

    for s in range(ns):
        vn = _layernorm(part(s, P_AV), lng_ref[...])
        vn_ref[s] = vn
        sp = bsx_ref[...]
        for i in range(dec_seq):
            sp = sp + jnp.where(rix >= i, wsx_ref[i], 0.0) * vn[i:i + 1, :]
        br_ref[s, :, 0:BRANCH_W] = _silu(part(s, P_AG)) * (part(s, P_AU) * sp)

        z = part(s, P_BC) * part(s, P_BX)
        prev = conv_ref[0, s]
        zrow = lambda i, prev=prev, z=z: (prev[i:i + 1, :] if i < CONV_W - 1
                                          else z[i - (CONV_W - 1):i - (CONV_W - 2), :])
        conv = jnp.zeros((r8, BRANCH_W), F32)
        for t in range(dec_seq):
            c_t = cw_ref[0:1, :] * zrow(t) + cw_ref[1:2, :] * zrow(t + 1) + cw_ref[2:3, :] * zrow(t + 2)
            conv = jnp.where(rix == t, c_t, conv)
        br_ref[s, :, BRANCH_W:2 * BRANCH_W] = _silu(part(s, P_BG)) * (part(s, P_BB) * conv)
        co_ref[s, 0] = z[dec_seq - (CONV_W - 1):dec_seq, :]

    tab = tab_ref[...]
    q = [_rope(part(s, P_CQ), tab) for s in range(ns)]
    k = [_rope(part(s, P_CK), tab) for s in range(ns)]
    v = [part(s, P_CV) for s in range(ns)]
    for s in range(ns):
        ko_ref[s] = k[s]
        vo_ref[s] = v[s]
    n_past = N_PAGES // PAGES_PER_BLOCK
    scale = C_HD ** -0.5
    hr = ns * C_HEADS * r8
    stack = lambda f: jnp.concatenate([f(s, h) for s in range(ns) for h in range(C_HEADS)], axis=0)
    head_lanes = lambda h: slice(h * C_HD, (h + 1) * C_HD)
    group = lambda x, s, h: x[(s * C_HEADS + h) * r8:(s * C_HEADS + h + 1) * r8, :]
    q_rows = stack(lambda s, h: q[s][:, head_lanes(h)]) * scale
    t_q = lax.broadcasted_iota(jnp.int32, (hr, 1), 0) % r8

    def head_block_t(pages, n, h):
        return jnp.concatenate([pages[n * PAGES_PER_BLOCK + i][0, 0, h] for i in range(PAGES_PER_BLOCK)], axis=1)

    s_own = []
    for c in range(dec_seq):
        k_c = stack(lambda s, h: jnp.broadcast_to(k[s][c:c + 1, head_lanes(h)], (r8, C_HD)))
        s_own.append(jnp.where(t_q >= c, jnp.sum(q_rows * k_c, axis=1, keepdims=True), NEG))
    q_hi = q_rows.astype(BF16).astype(F32)
    q_lo = q_rows - q_hi

    q2 = {(s, h): jnp.concatenate([group(q_hi, s, h), group(q_lo, s, h)], axis=0).astype(BF16)
          for s in range(ns) for h in range(C_HEADS)}

    def past_scores(s, h, n):
        s2 = jnp.dot(q2[s, h], head_block_t(k_pages[s], n, h).astype(BF16), preferred_element_type=F32)
        return s2[0:r8, :] + s2[r8:2 * r8, :]

    s_past = []
    gate = jnp.zeros((hr, GATE_COLS), F32)
    gate_col = lax.broadcasted_iota(jnp.int32, (hr, GATE_COLS), 1)
    for n in range(n_past):
        s_n = stack(lambda s, h: past_scores(s, h, n))
        s_past.append(s_n)
        gate = jnp.where(gate_col == n, jnp.sum(s_n, axis=1, keepdims=True) * (1.0 / MOBA_BLOCK), gate)
    sel = _top_blocks(gate, gate_col < n_past, 1)
    s_past = [jnp.where(sel[:, n:n + 1] > 0.0, s_past[n], NEG) for n in range(n_past)]
    m = s_own[0]
    for s_c in s_own[1:]:
        m = jnp.maximum(m, s_c)
    for s_n in s_past:
        m = jnp.maximum(m, jnp.max(s_n, axis=1, keepdims=True))
    l = jnp.zeros((hr, 1), F32)
    acc = jnp.zeros((hr, C_HD), F32)
    for c in range(dec_seq):
        v_c = stack(lambda s, h: jnp.broadcast_to(v[s][c:c + 1, head_lanes(h)], (r8, C_HD)))
        p = jnp.exp(s_own[c] - m)
        l = l + p
        acc = acc + p * v_c
    for n in range(n_past):
        p = jnp.exp(s_past[n] - m)
        l = l + jnp.sum(p, axis=1, keepdims=True)
        p = p.astype(BF16)
        acc = acc + stack(lambda s, h: lax.dot_general(group(p, s, h), head_block_t(v_pages[s], n, h).astype(BF16),
                                                       NT_DIMS, preferred_element_type=F32))
    acc = acc / l

    out_m = _mem_attend([(part(s, P_MQ),
                          lambda h, s=s: mk_ref[0, s, pl.ds(h, N_MEM, stride=M_HEADS), :],
                          lambda h, s=s: mv_ref[0, s, pl.ds(h, N_MEM, stride=M_HEADS), :]) for s in range(ns)])
    for s in range(ns):
        out_c = jnp.concatenate([group(acc, s, h) for h in range(C_HEADS)], axis=1)
        br_ref[s, :, 2 * BRANCH_W:3 * BRANCH_W] = _silu(part(s, P_CG)) * out_c
        br_ref[s, :, 3 * BRANCH_W:4 * BRANCH_W] = _silu(part(s, P_MG)) * out_m[s]


def _sample_branch(layer, parts3, tab, state_conv, cache_k, cache_v, cache_mem_k, cache_mem_v, page_table, lw,
                   dec_seq):
    bs = parts3.shape[0]
    r8 = SAMPLE_ROWS
    ns = SAMPLE_SEQS
    page_spec = lambda s, p: pl.BlockSpec((1, 1, C_HEADS, C_HD, PAGE),
                                          lambda b, pt: (layer, pt[ns * b + s, p], 0, 0, 0))
    page_specs = [page_spec(s, p) for s in range(ns) for p in range(N_PAGES)]
    const = lambda shape: pl.BlockSpec(shape, lambda b, pt: (0,) * len(shape))
    seq_spec = lambda rows, w: pl.BlockSpec((ns, rows, w), lambda b, pt: (b, 0, 0))
    in_specs = [
        seq_spec(r8, IN_W),
        const((r8, 384)),
        pl.BlockSpec((1, ns, CONV_W - 1, BRANCH_W), lambda b, pt: (layer, b, 0, 0)),
        pl.BlockSpec((1, ns, N_MEM * M_HEADS, M_HD), lambda b, pt: (layer, b, 0, 0)),
        pl.BlockSpec((1, ns, N_MEM * M_HEADS, M_HD), lambda b, pt: (layer, b, 0, 0)),
    ] + page_specs + page_specs + [
        const((1, BRANCH_W)), const((dec_seq, r8, BRANCH_W)), const((r8, BRANCH_W)), const((CONV_W, BRANCH_W)),
    ]
    out_specs = [seq_spec(r8, BRANCH_W), seq_spec(r8, BRANCH_W),
                 pl.BlockSpec((ns, 1, CONV_W - 1, BRANCH_W), lambda b, pt: (b, 0, 0, 0)),
                 seq_spec(r8, BRANCH_W), seq_spec(r8, 4 * BRANCH_W)]
    out_shape = [jax.ShapeDtypeStruct((bs, r8, BRANCH_W), F32), jax.ShapeDtypeStruct((bs, r8, BRANCH_W), F32),
                 jax.ShapeDtypeStruct((bs, 1, CONV_W - 1, BRANCH_W), F32),
                 jax.ShapeDtypeStruct((bs, r8, BRANCH_W), F32), jax.ShapeDtypeStruct((bs, r8, 4 * BRANCH_W), F32)]
    grid_spec = pltpu.PrefetchScalarGridSpec(
        num_scalar_prefetch=1, grid=(bs // ns,), in_specs=in_specs, out_specs=out_specs)
    return pl.pallas_call(
        functools.partial(_sample_branch_kernel, dec_seq=dec_seq),
        grid_spec=grid_spec,
        out_shape=out_shape,
        compiler_params=_cparams(("arbitrary",)),
        name="sample_branch",
    )(page_table, parts3, tab, state_conv, cache_mem_k, cache_mem_v,
      *([cache_k] * (ns * N_PAGES)), *([cache_v] * (ns * N_PAGES)),
      lw["ln_v_gain"], lw["w_spatial_x"], lw["b_spatial_x"], lw["conv_w"])


def _sample_tail_kernel(x_ref, br_ref, gpre_ref, gpost_ref, wm_ref, bm_ref, wb_ref, wo_ref, y_ref):
    branches = tuple(br_ref[:, n * BRANCH_W:(n + 1) * BRANCH_W] for n in range(4))
    y_ref[...] = _tail(x_ref[...], branches, gpre_ref, gpost_ref, wm_ref, bm_ref, wb_ref, wo_ref)


def _sample_tail(x, branches, lw):
    n = x.shape[0]
    tm = TAIL_TM
    return pl.pallas_call(
        _sample_tail_kernel,
        grid=(n // tm,),
        in_specs=[
            pl.BlockSpec((tm, D_MODEL), lambda i: (i, 0)),
            pl.BlockSpec((tm, 4 * BRANCH_W), lambda i: (i, 0)),
            _const_spec((1, D_MODEL)), _const_spec((1, D_MODEL)),
            _const_spec((D_MODEL, 4 * D_MODEL)), _const_spec((1, 4 * D_MODEL)),
            _const_spec((4, BRANCH_W, D_MODEL)), _const_spec((D_MODEL, D_MODEL)),
        ],
        out_specs=pl.BlockSpec((tm, D_MODEL), lambda i: (i, 0)),
        out_shape=jax.ShapeDtypeStruct((n, D_MODEL), F32),
        compiler_params=_cparams(("parallel",)),
        name="sample_tail",
    )(x, branches, lw["g_pre"], lw["g_post"], lw["w_merge"], lw["b_merge"], lw["w_branch"], lw["w_out"])


def kernel(x_prompt, x_sample, cache_k, cache_v, cache_mem_k, cache_mem_v, state_conv, page_table, mem_prompt,
           g_pre, g_post, w_in, ln_v_gain, w_spatial, b_spatial, conv_w, g_mem, w_mem_kv, w_merge, b_merge,
           w_branch, w_out):
    bp, tp, d = x_prompt.shape
    bs, ts, _ = x_sample.shape
    depth = w_in.shape[0]
    page = cache_k.shape[2]
    past_len = page_table.shape[1] * page
    assert (d, page, page_table.shape[1]) == (D_MODEL, PAGE, N_PAGES) and ts <= SAMPLE_ROWS
    assert past_len % MOBA_BLOCK == 0 and tp % KV_TM == 0 and ts >= CONV_W - 1
    r8 = SAMPLE_ROWS

    tab_p = _rope_table(jnp.arange(tp, dtype=jnp.int32))
    tab_s = _rope_table(past_len + jnp.arange(r8, dtype=jnp.int32))
    xp = x_prompt.reshape(bp * tp, d)
    xs = jnp.pad(x_sample, ((0, 0), (0, r8 - ts), (0, 0))).reshape(bs * r8, d)
    mem = mem_prompt.reshape(bp * N_MEM, d)
    ck = jnp.transpose(cache_k, (0, 1, 3, 4, 2))
    cv = jnp.transpose(cache_v, (0, 1, 3, 4, 2))
    cmk = cache_mem_k.reshape(depth, bs, N_MEM * M_HEADS, M_HD)
    cmv = cache_mem_v.reshape(depth, bs, N_MEM * M_HEADS, M_HD)
    lane_group = jnp.arange(BRANCH_W) // (BRANCH_W // A_GROUPS)

    outs = {name: [] for name in ("kp", "vp", "cp", "mkp", "mvp", "ks", "vs", "cs", "vns")}
    for l in range(depth):
        wsx = jnp.transpose(w_spatial[l][:, :r8, :ts], (2, 1, 0))[:, :, lane_group]
        bsx = jnp.transpose(b_spatial[l][:, :r8], (1, 0))[:, lane_group]
        lw = {
            "g_pre": g_pre[l].reshape(1, d), "g_post": g_post[l].reshape(1, d),
            "ln_v_gain": ln_v_gain[l].reshape(1, BRANCH_W),
            "w_spatial": w_spatial[l], "b_spatial_t": b_spatial[l].T,
            "w_spatial_x": wsx, "b_spatial_x": bsx,
            "conv_w": conv_w[l],
            "w_merge": w_merge[l].astype(BF16), "b_merge": b_merge[l].reshape(1, 4 * d),
            "w_branch": w_branch[l].astype(BF16), "w_out": w_out[l].astype(BF16),
        }
        w_in_b = w_in[l].astype(BF16)

        mkv = _inproj(mem, g_mem[l], w_mem_kv[l].astype(BF16), 1024, 1024)
        parts = _inproj(xp, g_pre[l], w_in_b, 1024, 3328)
        k_p, v_p, kb, vb, kmean = _kvpost(parts, tab_p, tp)
        kmean = kmean.reshape(bp, tp // MOBA_BLOCK, BRANCH_W)
        out_c = _moba_prompt(parts, tab_p, kb, vb, kmean, bp, tp)
        xp, ztail = _prompt_tail(xp, parts, out_c, mkv, lw, tp)
        outs["kp"].append(jnp.transpose(k_p, (0, 3, 1, 2)))
        outs["vp"].append(jnp.transpose(v_p, (0, 3, 1, 2)))
        outs["cp"].append(ztail.reshape(bp, tp // TAIL_TM, 8, BRANCH_W)[:, -1, 8 - (CONV_W - 1):, :])
        outs["mkp"].append(mkv[:, :BRANCH_W].reshape(bp, N_MEM, M_HEADS, M_HD))
        outs["mvp"].append(mkv[:, BRANCH_W:].reshape(bp, N_MEM, M_HEADS, M_HD))

        parts_s = _inproj(xs, g_pre[l], w_in_b, bs * r8, 1664).reshape(bs, r8, IN_W)
        k_s, v_s, c_s, vn_s, br_s = _sample_branch(l, parts_s, tab_s, state_conv, ck, cv, cmk, cmv, page_table,
                                                   lw, ts)
        xs = _sample_tail(xs, br_s.reshape(bs * r8, 4 * BRANCH_W), lw)
        outs["ks"].append(k_s[:, :ts].reshape(bs, ts, C_HEADS, C_HD))
        outs["vs"].append(v_s[:, :ts].reshape(bs, ts, C_HEADS, C_HD))
        outs["cs"].append(c_s.reshape(bs, CONV_W - 1, BRANCH_W))
        outs["vns"].append(vn_s[:, :ts])

    st = lambda name: jnp.stack(outs[name])
    return (xp.reshape(bp, tp, d), xs.reshape(bs, r8, d)[:, :ts],
            st("kp"), st("vp"), st("cp"), st("mkp"), st("mvp"),
            st("ks"), st("vs"), st("cs"), st("vns"))
```

```python
import functools

import jax
import jax.numpy as jnp
from jax import lax
from jax.experimental import pallas as pl
from jax.experimental.pallas import tpu as pltpu

D_MODEL = 1024
BRANCH_W = 512
N_PARTS = 13
IN_W = N_PARTS * BRANCH_W
CHUNK = 128
A_GROUPS = 4
CONV_W = 3
N_BRANCH = 4
C_HEADS = 8
C_HD = 64
ROPE_DIM = 16
ROPE_THETA = 500000.0
MOBA_BLOCK = 256
MOBA_TOPK = 3
M_HEADS = 4
M_HD = 128
N_MEM = 256
EPS = 1e-6

P_AU, P_AV, P_AG, P_BB, P_BC, P_BX, P_BG, P_CQ, P_CK, P_CV, P_CG, P_MQ, P_MG = range(13)

SAMPLE_ROWS = 8
NEG = -1e30
GATE_COLS = 128
VMEM_LIMIT = 56 * 1024 * 1024

F32 = jnp.float32
BF16 = jnp.bfloat16
NT_DIMS = (((1,), (1,)), ((), ()))


def _cparams(sem):
    return pltpu.CompilerParams(dimension_semantics=sem, vmem_limit_bytes=VMEM_LIMIT)


def _rms(x, g):
    r = lax.rsqrt(jnp.mean(x * x, axis=-1, keepdims=True) + EPS)
    return (x * r) * g


def _layernorm(x, g):
    mu = jnp.mean(x, axis=-1, keepdims=True)
    xc = x - mu
    r = lax.rsqrt(jnp.mean(xc * xc, axis=-1, keepdims=True) + EPS)
    return (xc * r) * g


def _sigmoid(x):
    return 1.0 / (1.0 + jnp.exp(-x))


def _silu(x):
    return x * _sigmoid(x)


def _rope(x, tab):
    c, s_lo, s_hi = tab[:, 0:128], tab[:, 128:256], tab[:, 256:384]
    half = ROPE_DIM // 2
    outs = []
    for g in range(BRANCH_W // 128):
        xg = x[:, g * 128:(g + 1) * 128]
        outs.append(xg * c + pltpu.roll(xg, 128 - half, axis=1) * s_lo + pltpu.roll(xg, half, axis=1) * s_hi)
    return jnp.concatenate(outs, axis=1)


def _rope_table(pos):
    half = ROPE_DIM // 2
    inv = jnp.power(jnp.float32(ROPE_THETA), -jnp.arange(half, dtype=F32) * (2.0 / ROPE_DIM))
    ang = pos.astype(F32)[:, None] * inv[None, :]
    cos, sin = jnp.cos(ang), jnp.sin(ang)
    d = jnp.arange(128) % C_HD
    idx = d % half
    cosl = jnp.where(d[None, :] < ROPE_DIM, cos[:, idx], 1.0)
    s_lo = jnp.where(d[None, :] < half, -sin[:, idx], 0.0)
    s_hi = jnp.where((d[None, :] >= half) & (d[None, :] < ROPE_DIM), sin[:, idx], 0.0)
    return jnp.concatenate([cosl, s_lo, s_hi], axis=1).astype(F32)


def _top_blocks(gate, valid, axis):
    idx = lax.broadcasted_iota(jnp.int32, gate.shape, axis).astype(F32)
    g = jnp.where(valid, gate, -jnp.inf)
    sel = jnp.zeros(gate.shape, F32)
    for _ in range(MOBA_TOPK):
        m = jnp.max(g, axis=axis, keepdims=True)
        cand = (g == m) & valid & (sel == 0.0)
        first = jnp.min(jnp.where(cand, idx, float(gate.shape[axis])), axis=axis, keepdims=True)
        pick = idx == first
        sel = jnp.where(pick, 1.0, sel)
        g = jnp.where(pick, -jnp.inf, g)
    return sel


def _inproj_kernel(x_ref, g_ref, w_ref, o_ref):
    h = _rms(x_ref[...], g_ref[...]).astype(BF16)
    o_ref[...] = jnp.dot(h, w_ref[...], preferred_element_type=F32)


def _inproj(x, g, w_bf16, tm, tn):
    n, d = x.shape
    n_out = w_bf16.shape[1]
    return pl.pallas_call(
        _inproj_kernel,
        grid=(n_out // tn, n // tm),
        in_specs=[
            pl.BlockSpec((tm, d), lambda j, i: (i, 0)),
            pl.BlockSpec((1, d), lambda j, i: (0, 0)),
            pl.BlockSpec((d, tn), lambda j, i: (0, j)),
        ],
        out_specs=pl.BlockSpec((tm, tn), lambda j, i: (i, j)),
        out_shape=jax.ShapeDtypeStruct((n, n_out), F32),
        compiler_params=_cparams(("parallel", "parallel")),
        name="inproj",
    )(x, g.reshape(1, d), w_bf16)


KV_TM = 1024


V_ROWS = 80
LOG2E = 1.4426950408889634


def _kvpost_kernel(k_ref, v_ref, tab_ref, *rest, tiles_per_seq):
    kt_ref, vt_ref, kb_ref, vtb_ref, km_ref = rest[-5:]
    k = _rope(k_ref[...], tab_ref[...])
    k_t = k.T
    v_t = v_ref[...].T
    kt_ref[0, 0] = k_t.reshape(C_HEADS, C_HD, KV_TM)
    vt_ref[0, 0] = v_t.reshape(C_HEADS, C_HD, KV_TM)
    lane = lax.broadcasted_iota(jnp.int32, (KV_TM, BRANCH_W), 1)
    row = lax.broadcasted_iota(jnp.int32, (KV_TM, BRANCH_W), 0)
    blk = (pl.program_id(0) % tiles_per_seq) * (KV_TM // MOBA_BLOCK) + row // MOBA_BLOCK
    onehot = jnp.where(lane % C_HD == blk, 1.0, 0.0)
    odd_head = (lane // C_HD) % 2 == 1
    kb_ref[0] = jnp.where(odd_head, onehot, k).astype(BF16)
    kb_ref[1] = jnp.where(odd_head, k, onehot).astype(BF16)
    extra = jnp.where(lax.broadcasted_iota(jnp.int32, (V_ROWS - C_HD, MOBA_BLOCK), 0) == 0, 1.0, 0.0)
    for i in range(KV_TM // MOBA_BLOCK):
        rows = slice(i * MOBA_BLOCK, (i + 1) * MOBA_BLOCK)
        pieces = []
        for h in range(C_HEADS):
            pieces += [v_t[h * C_HD:(h + 1) * C_HD, rows], extra]
        vtb_ref[i] = jnp.concatenate(pieces, axis=0).astype(BF16)
        km_ref[0, i:i + 1, :] = jnp.sum(k[rows, :], axis=0, keepdims=True) * (1.0 / MOBA_BLOCK)


def _kvpost(parts, tab, seq, layer, depth, state):
    n = parts.shape[0]
    tiles_per_seq = seq // KV_TM
    blocks_per_tile = KV_TM // MOBA_BLOCK
    assert seq // MOBA_BLOCK <= C_HD
    blk = lambda c: pl.BlockSpec((KV_TM, BRANCH_W), lambda i: (i, c))
    slab = pl.BlockSpec((1, 1, C_HEADS, C_HD, KV_TM),
                        lambda i: (layer, i // tiles_per_seq, 0, 0, i % tiles_per_seq))
    carried = list(state)
    return pl.pallas_call(
        functools.partial(_kvpost_kernel, tiles_per_seq=tiles_per_seq),
        grid=(n // KV_TM,),
        in_specs=[blk(P_CK), blk(P_CV),
                  pl.BlockSpec((KV_TM, 384), lambda i: (i % tiles_per_seq, 0))]
        + [pl.BlockSpec(memory_space=pl.ANY)] * len(carried),
        input_output_aliases={3 + i: i for i in range(len(carried))},
        out_specs=[slab, slab,
                   pl.BlockSpec((2, KV_TM, BRANCH_W), lambda i: (0, i, 0)),
                   pl.BlockSpec((blocks_per_tile, C_HEADS * V_ROWS, MOBA_BLOCK), lambda i: (i, 0, 0)),
                   pl.BlockSpec((1, blocks_per_tile, BRANCH_W), lambda i: (i, 0, 0))],
        out_shape=[jax.ShapeDtypeStruct((depth, n // seq, C_HEADS, C_HD, seq), F32),
                   jax.ShapeDtypeStruct((depth, n // seq, C_HEADS, C_HD, seq), F32),
                   jax.ShapeDtypeStruct((2, n, BRANCH_W), BF16),
                   jax.ShapeDtypeStruct((n // MOBA_BLOCK, C_HEADS * V_ROWS, MOBA_BLOCK), BF16),
                   jax.ShapeDtypeStruct((n // KV_TM, blocks_per_tile, BRANCH_W), F32)],
        compiler_params=_cparams(("parallel",)),
        name="kvpost",
    )(parts, parts, tab, *carried)


MOBA_HEADS_PER_LOOP = 4


def _moba_prompt_kernel(q_ref, tab_ref, kb_ref, vtb_ref, km_ref, o_ref, qt_ref, m_ref, acc_ref, sa_ref, sb_ref):
    j = pl.program_id(1)
    tq = MOBA_BLOCK
    nb = km_ref.shape[1]
    q = _rope(q_ref[...], tab_ref[...])
    km = km_ref[0]
    km_hi = km.astype(BF16).astype(F32)
    km2 = jnp.concatenate([km_hi, km - km_hi], axis=0).astype(BF16)
    blk_i = lax.broadcasted_iota(jnp.int32, (nb, tq), 0)
    dim_i = lax.broadcasted_iota(jnp.int32, (128, tq), 0)
    for g in range(C_HEADS // 2):
        lanes = slice(g * 128, (g + 1) * 128)
        qt = q[:, lanes].T
        for hh in range(2):
            h = 2 * g + hh
            qh = jnp.where((dim_i >= hh * C_HD) & (dim_i < (hh + 1) * C_HD), qt, 0.0)
            q_hi = qh.astype(BF16)
            q_lo = (qh - q_hi.astype(F32)).astype(BF16)
            gate2 = jnp.dot(km2[:, lanes], q_hi, preferred_element_type=F32)
            gate = (gate2[0:nb, :] + gate2[nb:2 * nb, :]
                    + jnp.dot(km2[0:nb, lanes], q_lo, preferred_element_type=F32))
            sel = _top_blocks(gate, blk_i < j, 0)
            pen = jnp.where((sel > 0.0) | (blk_i == j), 0.0, NEG)
            pen = jnp.concatenate([pen, jnp.zeros((C_HD - nb, tq), F32)], axis=0)
            qs = qh * (C_HD ** -0.5 * LOG2E)
            full = (jnp.concatenate([qs[0:C_HD, :], pen], axis=0) if hh == 0
                    else jnp.concatenate([pen, qs[C_HD:2 * C_HD, :]], axis=0))
            qt_ref[h] = full.astype(BF16)
    m_ref[...] = jnp.full(m_ref.shape, NEG, F32)
    acc_ref[...] = jnp.zeros(acc_ref.shape, F32)

    def scores(heads, n, dst_ref, mask_fn=None):
        r0 = pl.multiple_of(n * MOBA_BLOCK, MOBA_BLOCK)
        for i, h in enumerate(heads):
            kblk = kb_ref[h % 2, pl.ds(r0, MOBA_BLOCK), (h // 2) * 128:(h // 2 + 1) * 128]
            s = jnp.dot(kblk, qt_ref[h], preferred_element_type=F32)
            dst_ref[i] = s if mask_fn is None else mask_fn(s)

    def accumulate(heads, n, src_ref, weight=None):
        for i, h in enumerate(heads):
            rows = slice(h * V_ROWS, (h + 1) * V_ROWS)
            m = m_ref[h:h + 1, :]
            m_new = jnp.maximum(m, jnp.max(src_ref[i], axis=0, keepdims=True))
            a = jnp.exp2(m - m_new)
            p = jnp.exp2(src_ref[i] - m_new).astype(BF16)
            pv = jnp.dot(vtb_ref[n, rows, :], p, preferred_element_type=F32)
            if weight is not None:
                pv = pv * weight
            acc_ref[rows, :] = a * acc_ref[rows, :] + pv
            m_ref[h:h + 1, :] = m_new

    key_i = lax.broadcasted_iota(jnp.int32, (MOBA_BLOCK, tq), 0)
    qry_i = lax.broadcasted_iota(jnp.int32, (MOBA_BLOCK, tq), 1)
    causal = key_i <= qry_i
    for h0 in range(0, C_HEADS, MOBA_HEADS_PER_LOOP):
        heads = tuple(range(h0, h0 + MOBA_HEADS_PER_LOOP))
        scores(heads, j, sa_ref, lambda s: jnp.where(causal, s, NEG))

        def body(i, carry, heads=heads):
            n0 = 2 * i
            n1 = n0 + 1
            scores(heads, n0, sb_ref)
            accumulate(heads, jnp.where(i == 0, j, n0 - 1), sa_ref)
            n1c = jnp.minimum(n1, j - 1)
            scores(heads, n1c, sa_ref)
            accumulate(heads, n0, sb_ref)
            return carry

        trips = (j + 1) // 2
        lax.fori_loop(0, trips, body, 0)
        last_valid = jnp.where(j % 2 == 1, 0.0, 1.0)
        accumulate(heads, jnp.where(j == 0, j, j - 1), sa_ref, weight=last_valid)
    out_t = jnp.concatenate(
        [acc_ref[h * V_ROWS:h * V_ROWS + C_HD, :] / acc_ref[h * V_ROWS + C_HD:h * V_ROWS + C_HD + 1, :]
         for h in range(C_HEADS)], axis=0)
    o_ref[...] = out_t.T


def _moba_prompt(parts, tab, kb, vtb, kmean, batch, seq):
    n = parts.shape[0]
    nb = seq // MOBA_BLOCK
    return pl.pallas_call(
        _moba_prompt_kernel,
        grid=(batch, nb),
        in_specs=[
            pl.BlockSpec((MOBA_BLOCK, BRANCH_W), lambda b, j: (b * nb + j, P_CQ)),
            pl.BlockSpec((MOBA_BLOCK, 384), lambda b, j: (j, 0)),
            pl.BlockSpec((2, seq, BRANCH_W), lambda b, j: (0, b, 0)),
            pl.BlockSpec((nb, C_HEADS * V_ROWS, MOBA_BLOCK), lambda b, j: (b, 0, 0)),
            pl.BlockSpec((1, nb, BRANCH_W), lambda b, j: (b, 0, 0)),
        ],
        out_specs=pl.BlockSpec((MOBA_BLOCK, BRANCH_W), lambda b, j: (b * nb + j, 0)),
        out_shape=jax.ShapeDtypeStruct((n, BRANCH_W), F32),
        scratch_shapes=[pltpu.VMEM((C_HEADS, 128, MOBA_BLOCK), BF16),
                        pltpu.VMEM((C_HEADS, MOBA_BLOCK), F32),
                        pltpu.VMEM((C_HEADS * V_ROWS, MOBA_BLOCK), F32),
                        pltpu.VMEM((MOBA_HEADS_PER_LOOP, MOBA_BLOCK, MOBA_BLOCK), F32),
                        pltpu.VMEM((MOBA_HEADS_PER_LOOP, MOBA_BLOCK, MOBA_BLOCK), F32)],
        compiler_params=_cparams(("parallel", "arbitrary")),
        name="moba_prompt",
    )(parts, tab, kb, vtb, kmean)


def _merge_gates(x, gpre_ref, wm_ref, bm_ref):
    h = _rms(x, gpre_ref[...]).astype(BF16)
    cols = lambda n: slice(n * D_MODEL, (n + 1) * D_MODEL)
    return [_sigmoid(jnp.dot(h, wm_ref[:, cols(n)], preferred_element_type=F32) + bm_ref[:, cols(n)])
            for n in range(N_BRANCH)]


def _merge_out(x, gates, branches, gpost_ref, wb_ref, wo_ref):
    acc = None
    for n, (gate, br) in enumerate(zip(gates, branches)):
        proj = jnp.dot(br.astype(BF16), wb_ref[n], preferred_element_type=F32)
        acc = gate * proj if acc is None else acc + gate * proj
    y = jnp.dot(acc.astype(BF16), wo_ref[...], preferred_element_type=F32)
    return x + _rms(y, gpost_ref[...])


def _mem_scores(problems):
    lanes = lambda h: slice(h * M_HD, (h + 1) * M_HD)
    return [[lax.dot_general(mq[:, lanes(h)].astype(BF16), mk_head(h).astype(BF16), NT_DIMS,
                             preferred_element_type=F32) * (M_HD ** -0.5) for h in range(M_HEADS)]
            for mq, mk_head, _ in problems]


def _mem_values(problems, scores):
    probs = [[jnp.exp(s - jnp.max(s, axis=1, keepdims=True)) for s in per_head] for per_head in scores]
    return [jnp.concatenate(
        [jnp.dot(p.astype(BF16), mv_head(h).astype(BF16), preferred_element_type=F32)
         / jnp.sum(p, axis=1, keepdims=True) for h, p in enumerate(per_head)], axis=1)
        for (_, _, mv_head), per_head in zip(problems, probs)]


def _mem_attend(problems):
    return _mem_values(problems, _mem_scores(problems))


TAIL_TM = 512


def _prompt_tail_kernel(x_ref, au_ref, av_ref, ag_ref, bb_ref, bc_ref, bx_ref, bg_ref, cg_ref, mq_ref, mg_ref,
                        hc_ref, hx_ref, oc_ref, mk_ref, mv_ref,
                        gpre_ref, gpost_ref, lng_ref, ws_ref, bs_ref, cw_ref, wm_ref, bm_ref, wb_ref, wo_ref,
                        y_ref, zt_ref, *, tiles_per_seq):
    tm = TAIL_TM
    x = x_ref[...]
    vn = _layernorm(av_ref[...], lng_ref[...]).astype(BF16)
    t_i = lax.broadcasted_iota(jnp.int32, (CHUNK, CHUNK), 0)
    s_i = lax.broadcasted_iota(jnp.int32, (CHUNK, CHUNK), 1)
    sp_cols = []
    for g in range(A_GROUPS):
        lanes = slice(g * 128, (g + 1) * 128)
        ws = jnp.where(s_i <= t_i, ws_ref[g], 0.0).astype(BF16)
        bias = bs_ref[:, g:g + 1]
        rows = [jnp.dot(ws, vn[c * CHUNK:(c + 1) * CHUNK, lanes], preferred_element_type=F32) + bias
                for c in range(tm // CHUNK)]
        sp_cols.append(jnp.concatenate(rows, axis=0))
    br_a = _silu(ag_ref[...]) * (au_ref[...] * jnp.concatenate(sp_cols, axis=1))
    mem = [(mq_ref[...], lambda h: mk_ref[:, h * M_HD:(h + 1) * M_HD], lambda h: mv_ref[:, h * M_HD:(h + 1) * M_HD])]
    out_m, = _mem_attend(mem)
    br_m = _silu(mg_ref[...]) * out_m
    z = bc_ref[...] * bx_ref[...]
    first_of_seq = (pl.program_id(0) % tiles_per_seq) == 0
    halo = jnp.where(first_of_seq, 0.0, hc_ref[...] * hx_ref[...])
    rix = lax.broadcasted_iota(jnp.int32, (tm, BRANCH_W), 0)
    z1 = jnp.where(rix == 0, halo[7:8, :], pltpu.roll(z, 1, axis=0))
    z2 = pltpu.roll(z, 2, axis=0)
    z2 = jnp.where(rix == 0, halo[6:7, :], jnp.where(rix == 1, halo[7:8, :], z2))
    conv = cw_ref[0:1, :] * z2 + cw_ref[1:2, :] * z1 + cw_ref[2:3, :] * z
    br_b = _silu(bg_ref[...]) * (bb_ref[...] * conv)
    zt_ref[0] = z[tm - 8:tm, :]
    br_c = _silu(cg_ref[...]) * oc_ref[...]
    gates = _merge_gates(x, gpre_ref, wm_ref, bm_ref)
    y_ref[...] = _merge_out(x, gates, (br_a, br_b, br_c, br_m), gpost_ref, wb_ref, wo_ref)


def _const_spec(shape):
    zeros = (0,) * len(shape)
    return pl.BlockSpec(shape, lambda i: zeros, pipeline_mode=pl.Buffered(1))


def _prompt_tail(x, parts, out_c, mkv, lw, seq):
    n = x.shape[0]
    tm = TAIL_TM
    tiles_per_seq = seq // tm
    part = lambda c: pl.BlockSpec((tm, BRANCH_W), lambda i: (i, c))
    halo = lambda c: pl.BlockSpec((8, BRANCH_W), lambda i: (jnp.maximum(i * (tm // 8) - 1, 0), c))
    row512 = pl.BlockSpec((tm, BRANCH_W), lambda i: (i, 0))
    in_specs = [
        pl.BlockSpec((tm, D_MODEL), lambda i: (i, 0)),
        part(P_AU), part(P_AV), part(P_AG), part(P_BB), part(P_BC), part(P_BX), part(P_BG),
        part(P_CG), part(P_MQ), part(P_MG),
        halo(P_BC), halo(P_BX),
        row512,
        pl.BlockSpec((N_MEM, BRANCH_W), lambda i: (i // tiles_per_seq, 0)),
        pl.BlockSpec((N_MEM, BRANCH_W), lambda i: (i // tiles_per_seq, 1)),
        _const_spec((1, D_MODEL)), _const_spec((1, D_MODEL)), _const_spec((1, BRANCH_W)),
        _const_spec((A_GROUPS, CHUNK, CHUNK)), _const_spec((CHUNK, A_GROUPS)), _const_spec((CONV_W, BRANCH_W)),
        _const_spec((D_MODEL, 4 * D_MODEL)), _const_spec((1, 4 * D_MODEL)),
        _const_spec((4, BRANCH_W, D_MODEL)), _const_spec((D_MODEL, D_MODEL)),
    ]
    return pl.pallas_call(
        functools.partial(_prompt_tail_kernel, tiles_per_seq=tiles_per_seq),
        grid=(n // tm,),
        in_specs=in_specs,
        out_specs=[pl.BlockSpec((tm, D_MODEL), lambda i: (i, 0)),
                   pl.BlockSpec((1, 8, BRANCH_W), lambda i: (i, 0, 0))],
        out_shape=[jax.ShapeDtypeStruct((n, D_MODEL), F32),
                   jax.ShapeDtypeStruct((n // tm, 8, BRANCH_W), F32)],
        compiler_params=_cparams(("parallel",)),
        name="prompt_tail",
    )(x, *([parts] * 10), parts, parts, out_c, mkv, mkv,
      lw["g_pre"], lw["g_post"], lw["ln_v_gain"], lw["w_spatial"], lw["b_spatial_t"], lw["conv_w"],
      lw["w_merge"], lw["b_merge"], lw["w_branch"], lw["w_out"])


N_PAGES = 16
PAGE = 128
PAGES_PER_BLOCK = MOBA_BLOCK // PAGE


SAMPLE_SEQS = 2


def _sample_branch_kernel(pt_ref, parts_ref, tab_ref, conv_ref, mk_ref, mv_ref, *rest, dec_seq):
    ns = SAMPLE_SEQS
    k_pages = [rest[s * N_PAGES:(s + 1) * N_PAGES] for s in range(ns)]
    v_pages = [rest[(ns + s) * N_PAGES:(ns + s + 1) * N_PAGES] for s in range(ns)]
    lng_ref, wsx_ref, bsx_ref, cw_ref, ko_ref, vo_ref, co_ref, vn_ref, br_ref = rest[2 * ns * N_PAGES:]
    del pt_ref
    r8 = SAMPLE_ROWS
    part = lambda s, c: parts_ref[s, :, c * BRANCH_W:(c + 1) * BRANCH_W]
    rix = lax.broadcasted_iota(jnp.int32, (r8, BRANCH_W), 0)

    for s in range(ns):
        vn = _layernorm(part(s, P_AV), lng_ref[...])
        vn_ref[s] = vn
        sp = bsx_ref[...]
        for i in range(dec_seq):
            sp = sp + jnp.where(rix >= i, wsx_ref[i], 0.0) * vn[i:i + 1, :]
        br_ref[s, :, 0:BRANCH_W] = _silu(part(s, P_AG)) * (part(s, P_AU) * sp)

        z = part(s, P_BC) * part(s, P_BX)
        prev = conv_ref[0, s]
        zrow = lambda i, prev=prev, z=z: (prev[i:i + 1, :] if i < CONV_W - 1
                                          else z[i - (CONV_W - 1):i - (CONV_W - 2), :])
        conv = jnp.zeros((r8, BRANCH_W), F32)
        for t in range(dec_seq):
            c_t = cw_ref[0:1, :] * zrow(t) + cw_ref[1:2, :] * zrow(t + 1) + cw_ref[2:3, :] * zrow(t + 2)
            conv = jnp.where(rix == t, c_t, conv)
        br_ref[s, :, BRANCH_W:2 * BRANCH_W] = _silu(part(s, P_BG)) * (part(s, P_BB) * conv)
        co_ref[s, 0] = z[dec_seq - (CONV_W - 1):dec_seq, :]

    tab = tab_ref[...]
    q = [_rope(part(s, P_CQ), tab) for s in range(ns)]
    k = [_rope(part(s, P_CK), tab) for s in range(ns)]
    v = [part(s, P_CV) for s in range(ns)]
    for s in range(ns):
        ko_ref[s] = k[s]
        vo_ref[s] = v[s]
    n_past = N_PAGES // PAGES_PER_BLOCK
    scale = C_HD ** -0.5
    hr = ns * C_HEADS * r8
    stack = lambda f: jnp.concatenate([f(s, h) for s in range(ns) for h in range(C_HEADS)], axis=0)
    head_lanes = lambda h: slice(h * C_HD, (h + 1) * C_HD)
    group = lambda x, s, h: x[(s * C_HEADS + h) * r8:(s * C_HEADS + h + 1) * r8, :]
    q_rows = stack(lambda s, h: q[s][:, head_lanes(h)]) * scale
    t_q = lax.broadcasted_iota(jnp.int32, (hr, 1), 0) % r8

    def head_block_t(pages, n, h):
        return jnp.concatenate([pages[n * PAGES_PER_BLOCK + i][0, 0, h] for i in range(PAGES_PER_BLOCK)], axis=1)

    s_own = []
    for c in range(dec_seq):
        k_c = stack(lambda s, h: jnp.broadcast_to(k[s][c:c + 1, head_lanes(h)], (r8, C_HD)))
        s_own.append(jnp.where(t_q >= c, jnp.sum(q_rows * k_c, axis=1, keepdims=True), NEG))
    q_hi = q_rows.astype(BF16).astype(F32)
    q_lo = q_rows - q_hi

    q2 = {(s, h): jnp.concatenate([group(q_hi, s, h), group(q_lo, s, h)], axis=0).astype(BF16)
          for s in range(ns) for h in range(C_HEADS)}

    def past_scores(s, h, n):
        s2 = jnp.dot(q2[s, h], head_block_t(k_pages[s], n, h).astype(BF16), preferred_element_type=F32)
        return s2[0:r8, :] + s2[r8:2 * r8, :]

    s_past = []
    gate = jnp.zeros((hr, GATE_COLS), F32)
    gate_col = lax.broadcasted_iota(jnp.int32, (hr, GATE_COLS), 1)
    for n in range(n_past):
        s_n = stack(lambda s, h: past_scores(s, h, n))
        s_past.append(s_n)
        gate = jnp.where(gate_col == n, jnp.sum(s_n, axis=1, keepdims=True) * (1.0 / MOBA_BLOCK), gate)
    sel = _top_blocks(gate, gate_col < n_past, 1)
    s_past = [jnp.where(sel[:, n:n + 1] > 0.0, s_past[n], NEG) for n in range(n_past)]
    m = s_own[0]
    for s_c in s_own[1:]:
        m = jnp.maximum(m, s_c)
    for s_n in s_past:
        m = jnp.maximum(m, jnp.max(s_n, axis=1, keepdims=True))
    l = jnp.zeros((hr, 1), F32)
    acc = jnp.zeros((hr, C_HD), F32)
    for c in range(dec_seq):
        v_c = stack(lambda s, h: jnp.broadcast_to(v[s][c:c + 1, head_lanes(h)], (r8, C_HD)))
        p = jnp.exp(s_own[c] - m)
        l = l + p
        acc = acc + p * v_c
    for n in range(n_past):
        p = jnp.exp(s_past[n] - m)
        l = l + jnp.sum(p, axis=1, keepdims=True)
        p = p.astype(BF16)
        acc = acc + stack(lambda s, h: lax.dot_general(group(p, s, h), head_block_t(v_pages[s], n, h).astype(BF16),
                                                       NT_DIMS, preferred_element_type=F32))
    acc = acc / l

    out_m = _mem_attend([(part(s, P_MQ),
                          lambda h, s=s: mk_ref[0, s, pl.ds(h, N_MEM, stride=M_HEADS), :],
                          lambda h, s=s: mv_ref[0, s, pl.ds(h, N_MEM, stride=M_HEADS), :]) for s in range(ns)])
    for s in range(ns):
        out_c = jnp.concatenate([group(acc, s, h) for h in range(C_HEADS)], axis=1)
        br_ref[s, :, 2 * BRANCH_W:3 * BRANCH_W] = _silu(part(s, P_CG)) * out_c
        br_ref[s, :, 3 * BRANCH_W:4 * BRANCH_W] = _silu(part(s, P_MG)) * out_m[s]


def _sample_branch(layer, parts3, tab, state_conv, cache_k, cache_v, cache_mem_k, cache_mem_v, page_table, lw,
                   dec_seq):
    bs = parts3.shape[0]
    r8 = SAMPLE_ROWS
    ns = SAMPLE_SEQS
    page_spec = lambda s, p: pl.BlockSpec((1, 1, C_HEADS, C_HD, PAGE),
                                          lambda b, pt: (layer, pt[ns * b + s, p], 0, 0, 0))
    page_specs = [page_spec(s, p) for s in range(ns) for p in range(N_PAGES)]
    const = lambda shape: pl.BlockSpec(shape, lambda b, pt: (0,) * len(shape))
    seq_spec = lambda rows, w: pl.BlockSpec((ns, rows, w), lambda b, pt: (b, 0, 0))
    in_specs = [
        seq_spec(r8, IN_W),
        const((r8, 384)),
        pl.BlockSpec((1, ns, CONV_W - 1, BRANCH_W), lambda b, pt: (layer, b, 0, 0)),
        pl.BlockSpec((1, ns, N_MEM * M_HEADS, M_HD), lambda b, pt: (layer, b, 0, 0)),
        pl.BlockSpec((1, ns, N_MEM * M_HEADS, M_HD), lambda b, pt: (layer, b, 0, 0)),
    ] + page_specs + page_specs + [
        const((1, BRANCH_W)), const((dec_seq, r8, BRANCH_W)), const((r8, BRANCH_W)), const((CONV_W, BRANCH_W)),
    ]
    out_specs = [seq_spec(r8, BRANCH_W), seq_spec(r8, BRANCH_W),
                 pl.BlockSpec((ns, 1, CONV_W - 1, BRANCH_W), lambda b, pt: (b, 0, 0, 0)),
                 seq_spec(r8, BRANCH_W), seq_spec(r8, 4 * BRANCH_W)]
    out_shape = [jax.ShapeDtypeStruct((bs, r8, BRANCH_W), F32), jax.ShapeDtypeStruct((bs, r8, BRANCH_W), F32),
                 jax.ShapeDtypeStruct((bs, 1, CONV_W - 1, BRANCH_W), F32),
                 jax.ShapeDtypeStruct((bs, r8, BRANCH_W), F32), jax.ShapeDtypeStruct((bs, r8, 4 * BRANCH_W), F32)]
    grid_spec = pltpu.PrefetchScalarGridSpec(
        num_scalar_prefetch=1, grid=(bs // ns,), in_specs=in_specs, out_specs=out_specs)
    return pl.pallas_call(
        functools.partial(_sample_branch_kernel, dec_seq=dec_seq),
        grid_spec=grid_spec,
        out_shape=out_shape,
        compiler_params=_cparams(("arbitrary",)),
        name="sample_branch",
    )(page_table, parts3, tab, state_conv, cache_mem_k, cache_mem_v,
      *([cache_k] * (ns * N_PAGES)), *([cache_v] * (ns * N_PAGES)),
      lw["ln_v_gain"], lw["w_spatial_x"], lw["b_spatial_x"], lw["conv_w"])


def _sample_tail_kernel(x_ref, br_ref, gpre_ref, gpost_ref, wm_ref, bm_ref, wb_ref, wo_ref, y_ref):
    branches = tuple(br_ref[:, n * BRANCH_W:(n + 1) * BRANCH_W] for n in range(N_BRANCH))
    x = x_ref[...]
    y_ref[...] = _merge_out(x, _merge_gates(x, gpre_ref, wm_ref, bm_ref), branches, gpost_ref, wb_ref, wo_ref)


def _sample_tail(x, branches, lw):
    n = x.shape[0]
    tm = TAIL_TM
    return pl.pallas_call(
        _sample_tail_kernel,
        grid=(n // tm,),
        in_specs=[
            pl.BlockSpec((tm, D_MODEL), lambda i: (i, 0)),
            pl.BlockSpec((tm, 4 * BRANCH_W), lambda i: (i, 0)),
            _const_spec((1, D_MODEL)), _const_spec((1, D_MODEL)),
            _const_spec((D_MODEL, 4 * D_MODEL)), _const_spec((1, 4 * D_MODEL)),
            _const_spec((4, BRANCH_W, D_MODEL)), _const_spec((D_MODEL, D_MODEL)),
        ],
        out_specs=pl.BlockSpec((tm, D_MODEL), lambda i: (i, 0)),
        out_shape=jax.ShapeDtypeStruct((n, D_MODEL), F32),
        compiler_params=_cparams(("parallel",)),
        name="sample_tail",
    )(x, branches, lw["g_pre"], lw["g_post"], lw["w_merge"], lw["b_merge"], lw["w_branch"], lw["w_out"])


def kernel(x_prompt, x_sample, cache_k, cache_v, cache_mem_k, cache_mem_v, state_conv, page_table, mem_prompt,
           g_pre, g_post, w_in, ln_v_gain, w_spatial, b_spatial, conv_w, g_mem, w_mem_kv, w_merge, b_merge,
           w_branch, w_out):
    bp, tp, d = x_prompt.shape
    bs, ts, _ = x_sample.shape
    depth = w_in.shape[0]
    page = cache_k.shape[2]
    past_len = page_table.shape[1] * page
    assert (d, page, page_table.shape[1]) == (D_MODEL, PAGE, N_PAGES) and ts <= SAMPLE_ROWS
    assert past_len % MOBA_BLOCK == 0 and tp % KV_TM == 0 and ts >= CONV_W - 1
    r8 = SAMPLE_ROWS

    tab_p = _rope_table(jnp.arange(tp, dtype=jnp.int32))
    tab_s = _rope_table(past_len + jnp.arange(r8, dtype=jnp.int32))
    xp = x_prompt.reshape(bp * tp, d)
    xs = jnp.pad(x_sample, ((0, 0), (0, r8 - ts), (0, 0))).reshape(bs * r8, d)
    mem = mem_prompt.reshape(bp * N_MEM, d)
    ck = jnp.transpose(cache_k, (0, 1, 3, 4, 2))
    cv = jnp.transpose(cache_v, (0, 1, 3, 4, 2))
    cmk = cache_mem_k.reshape(depth, bs, N_MEM * M_HEADS, M_HD)
    cmv = cache_mem_v.reshape(depth, bs, N_MEM * M_HEADS, M_HD)
    lane_group = jnp.arange(BRANCH_W) // (BRANCH_W // A_GROUPS)

    outs = {name: [] for name in ("cp", "mkp", "mvp", "ks", "vs", "cs", "vns")}
    kv_state = tuple(jnp.zeros((depth, bp, C_HEADS, C_HD, tp), F32) for _ in range(2))
    for l in range(depth):
        wsx = jnp.transpose(w_spatial[l][:, :r8, :ts], (2, 1, 0))[:, :, lane_group]
        bsx = jnp.transpose(b_spatial[l][:, :r8], (1, 0))[:, lane_group]
        lw = {
            "g_pre": g_pre[l].reshape(1, d), "g_post": g_post[l].reshape(1, d),
            "ln_v_gain": ln_v_gain[l].reshape(1, BRANCH_W),
            "w_spatial": w_spatial[l], "b_spatial_t": b_spatial[l].T,
            "w_spatial_x": wsx, "b_spatial_x": bsx,
            "conv_w": conv_w[l],
            "w_merge": w_merge[l].astype(BF16), "b_merge": b_merge[l].reshape(1, 4 * d),
            "w_branch": w_branch[l].astype(BF16), "w_out": w_out[l].astype(BF16),
        }
        w_in_b = w_in[l].astype(BF16)

        mkv = _inproj(mem, g_mem[l], w_mem_kv[l].astype(BF16), 1024, 1024)
        parts = _inproj(xp, g_pre[l], w_in_b, 1024, 3328)
        k_p, v_p, kb, vb, kmean = _kvpost(parts, tab_p, tp, l, depth, kv_state)
        kv_state = (k_p, v_p)
        kmean = kmean.reshape(bp, tp // MOBA_BLOCK, BRANCH_W)
        out_c = _moba_prompt(parts, tab_p, kb, vb, kmean, bp, tp)
        xp, ztail = _prompt_tail(xp, parts, out_c, mkv, lw, tp)
        outs["cp"].append(ztail.reshape(bp, tp // TAIL_TM, 8, BRANCH_W)[:, -1, 8 - (CONV_W - 1):, :])
        outs["mkp"].append(mkv[:, :BRANCH_W].reshape(bp, N_MEM, M_HEADS, M_HD))
        outs["mvp"].append(mkv[:, BRANCH_W:].reshape(bp, N_MEM, M_HEADS, M_HD))

        parts_s = _inproj(xs, g_pre[l], w_in_b, bs * r8, 1664).reshape(bs, r8, IN_W)
        k_s, v_s, c_s, vn_s, br_s = _sample_branch(l, parts_s, tab_s, state_conv, ck, cv, cmk, cmv, page_table,
                                                   lw, ts)
        xs = _sample_tail(xs, br_s.reshape(bs * r8, 4 * BRANCH_W), lw)
        outs["ks"].append(k_s[:, :ts].reshape(bs, ts, C_HEADS, C_HD))
        outs["vs"].append(v_s[:, :ts].reshape(bs, ts, C_HEADS, C_HD))
        outs["cs"].append(c_s.reshape(bs, CONV_W - 1, BRANCH_W))
        outs["vns"].append(vn_s[:, :ts])

    st = lambda name: jnp.stack(outs[name])
    return (xp.reshape(bp, tp, d), xs.reshape(bs, r8, d)[:, :ts],
            jnp.transpose(kv_state[0], (0, 1, 4, 2, 3)), jnp.transpose(kv_state[1], (0, 1, 4, 2, 3)),
            st("cp"), st("mkp"), st("mvp"),
            st("ks"), st("vs"), st("cs"), st("vns"))
```

```python
import functools

import jax
import jax.numpy as jnp
from jax import lax
from jax.experimental import pallas as pl
from jax.experimental.pallas import tpu as pltpu

D_MODEL = 1024
BRANCH_W = 512
N_PARTS = 13
IN_W = N_PARTS * BRANCH_W
CHUNK = 128
A_GROUPS = 4
CONV_W = 3
N_BRANCH = 4
C_HEADS = 8
C_HD = 64
ROPE_DIM = 16
ROPE_THETA = 500000.0
MOBA_BLOCK = 256
MOBA_TOPK = 3
M_HEADS = 4
M_HD = 128
N_MEM = 256
EPS = 1e-6

P_AU, P_AV, P_AG, P_BB, P_BC, P_BX, P_BG, P_CQ, P_CK, P_CV, P_CG, P_MQ, P_MG = range(13)

SAMPLE_ROWS = 8
NEG = -1e30
GATE_COLS = 128
VMEM_LIMIT = 56 * 1024 * 1024

F32 = jnp.float32
BF16 = jnp.bfloat16
NT_DIMS = (((1,), (1,)), ((), ()))


def _cparams(sem):
    return pltpu.CompilerParams(dimension_semantics=sem, vmem_limit_bytes=VMEM_LIMIT)


def _rms(x, g):
    r = lax.rsqrt(jnp.mean(x * x, axis=-1, keepdims=True) + EPS)
    return (x * r) * g


def _layernorm(x, g):
    mu = jnp.mean(x, axis=-1, keepdims=True)
    xc = x - mu
    r = lax.rsqrt(jnp.mean(xc * xc, axis=-1, keepdims=True) + EPS)
    return (xc * r) * g


def _sigmoid(x):
    return 1.0 / (1.0 + jnp.exp(-x))


def _silu(x):
    return x * _sigmoid(x)


def _rope(x, tab):
    c, s_lo, s_hi = tab[:, 0:128], tab[:, 128:256], tab[:, 256:384]
    half = ROPE_DIM // 2
    outs = []
    for g in range(BRANCH_W // 128):
        xg = x[:, g * 128:(g + 1) * 128]
        outs.append(xg * c + pltpu.roll(xg, 128 - half, axis=1) * s_lo + pltpu.roll(xg, half, axis=1) * s_hi)
    return jnp.concatenate(outs, axis=1)


def _rope_table(pos):
    half = ROPE_DIM // 2
    inv = jnp.power(jnp.float32(ROPE_THETA), -jnp.arange(half, dtype=F32) * (2.0 / ROPE_DIM))
    ang = pos.astype(F32)[:, None] * inv[None, :]
    cos, sin = jnp.cos(ang), jnp.sin(ang)
    d = jnp.arange(128) % C_HD
    idx = d % half
    cosl = jnp.where(d[None, :] < ROPE_DIM, cos[:, idx], 1.0)
    s_lo = jnp.where(d[None, :] < half, -sin[:, idx], 0.0)
    s_hi = jnp.where((d[None, :] >= half) & (d[None, :] < ROPE_DIM), sin[:, idx], 0.0)
    return jnp.concatenate([cosl, s_lo, s_hi], axis=1).astype(F32)


def _top_blocks(gate, valid, axis):
    idx = lax.broadcasted_iota(jnp.int32, gate.shape, axis).astype(F32)
    g = jnp.where(valid, gate, -jnp.inf)
    sel = jnp.zeros(gate.shape, F32)
    for _ in range(MOBA_TOPK):
        m = jnp.max(g, axis=axis, keepdims=True)
        cand = (g == m) & valid & (sel == 0.0)
        first = jnp.min(jnp.where(cand, idx, float(gate.shape[axis])), axis=axis, keepdims=True)
        pick = idx == first
        sel = jnp.where(pick, 1.0, sel)
        g = jnp.where(pick, -jnp.inf, g)
    return sel


def _inproj_kernel(x_ref, g_ref, w_ref, o_ref):
    h = _rms(x_ref[...], g_ref[...]).astype(BF16)
    o_ref[...] = jnp.dot(h, w_ref[...], preferred_element_type=F32)


def _inproj(x, g, w_bf16, tm, tn):
    n, d = x.shape
    n_out = w_bf16.shape[1]
    return pl.pallas_call(
        _inproj_kernel,
        grid=(n_out // tn, n // tm),
        in_specs=[
            pl.BlockSpec((tm, d), lambda j, i: (i, 0)),
            pl.BlockSpec((1, d), lambda j, i: (0, 0)),
            pl.BlockSpec((d, tn), lambda j, i: (0, j)),
        ],
        out_specs=pl.BlockSpec((tm, tn), lambda j, i: (i, j)),
        out_shape=jax.ShapeDtypeStruct((n, n_out), F32),
        compiler_params=_cparams(("parallel", "parallel")),
        name="inproj",
    )(x, g.reshape(1, d), w_bf16)


KV_TM = 1024


V_ROWS = 80
LOG2E = 1.4426950408889634


def _kvpost_kernel(k_ref, v_ref, tab_ref, *rest, tiles_per_seq):
    kt_ref, vt_ref, kb_ref, vtb_ref, km_ref = rest[-5:]
    k = _rope(k_ref[...], tab_ref[...])
    k_t = k.T
    v_t = v_ref[...].T
    kt_ref[0, 0] = k_t.reshape(C_HEADS, C_HD, KV_TM)
    vt_ref[0, 0] = v_t.reshape(C_HEADS, C_HD, KV_TM)
    lane = lax.broadcasted_iota(jnp.int32, (KV_TM, BRANCH_W), 1)
    row = lax.broadcasted_iota(jnp.int32, (KV_TM, BRANCH_W), 0)
    blk = (pl.program_id(0) % tiles_per_seq) * (KV_TM // MOBA_BLOCK) + row // MOBA_BLOCK
    onehot = jnp.where(lane % C_HD == blk, 1.0, 0.0)
    odd_head = (lane // C_HD) % 2 == 1
    kb_ref[0] = jnp.where(odd_head, onehot, k).astype(BF16)
    kb_ref[1] = jnp.where(odd_head, k, onehot).astype(BF16)
    extra = jnp.where(lax.broadcasted_iota(jnp.int32, (V_ROWS - C_HD, MOBA_BLOCK), 0) == 0, 1.0, 0.0)
    for i in range(KV_TM // MOBA_BLOCK):
        rows = slice(i * MOBA_BLOCK, (i + 1) * MOBA_BLOCK)
        pieces = []
        for h in range(C_HEADS):
            pieces += [v_t[h * C_HD:(h + 1) * C_HD, rows], extra]
        vtb_ref[i] = jnp.concatenate(pieces, axis=0).astype(BF16)
        km_ref[0, i:i + 1, :] = jnp.sum(k[rows, :], axis=0, keepdims=True) * (1.0 / MOBA_BLOCK)


def _kvpost(parts, tab, seq, layer, depth, state):
    n = parts.shape[0]
    tiles_per_seq = seq // KV_TM
    blocks_per_tile = KV_TM // MOBA_BLOCK
    assert seq // MOBA_BLOCK <= C_HD
    blk = lambda c: pl.BlockSpec((KV_TM, BRANCH_W), lambda i: (i, c))
    slab = pl.BlockSpec((1, 1, C_HEADS, C_HD, KV_TM),
                        lambda i: (layer, i // tiles_per_seq, 0, 0, i % tiles_per_seq))
    carried = list(state)
    return pl.pallas_call(
        functools.partial(_kvpost_kernel, tiles_per_seq=tiles_per_seq),
        grid=(n // KV_TM,),
        in_specs=[blk(P_CK), blk(P_CV),
                  pl.BlockSpec((KV_TM, 384), lambda i: (i % tiles_per_seq, 0))]
        + [pl.BlockSpec(memory_space=pl.ANY)] * len(carried),
        input_output_aliases={3 + i: i for i in range(len(carried))},
        out_specs=[slab, slab,
                   pl.BlockSpec((2, KV_TM, BRANCH_W), lambda i: (0, i, 0)),
                   pl.BlockSpec((blocks_per_tile, C_HEADS * V_ROWS, MOBA_BLOCK), lambda i: (i, 0, 0)),
                   pl.BlockSpec((1, blocks_per_tile, BRANCH_W), lambda i: (i, 0, 0))],
        out_shape=[jax.ShapeDtypeStruct((depth, n // seq, C_HEADS, C_HD, seq), F32),
                   jax.ShapeDtypeStruct((depth, n // seq, C_HEADS, C_HD, seq), F32),
                   jax.ShapeDtypeStruct((2, n, BRANCH_W), BF16),
                   jax.ShapeDtypeStruct((n // MOBA_BLOCK, C_HEADS * V_ROWS, MOBA_BLOCK), BF16),
                   jax.ShapeDtypeStruct((n // KV_TM, blocks_per_tile, BRANCH_W), F32)],
        compiler_params=_cparams(("parallel",)),
        name="kvpost",
    )(parts, parts, tab, *carried)


MOBA_HEADS_PER_LOOP = 4


def _moba_prompt_kernel(q_ref, tab_ref, kb_ref, vtb_ref, km_ref, o_ref, qt_ref, m_ref, acc_ref, sa_ref, sb_ref):
    j = pl.program_id(1)
    tq = MOBA_BLOCK
    nb = km_ref.shape[1]
    q = _rope(q_ref[...], tab_ref[...])
    km = km_ref[0]
    km_hi = km.astype(BF16).astype(F32)
    km2 = jnp.concatenate([km_hi, km - km_hi], axis=0).astype(BF16)
    blk_i = lax.broadcasted_iota(jnp.int32, (nb, tq), 0)
    dim_i = lax.broadcasted_iota(jnp.int32, (128, tq), 0)
    for g in range(C_HEADS // 2):
        lanes = slice(g * 128, (g + 1) * 128)
        qt = q[:, lanes].T
        for hh in range(2):
            h = 2 * g + hh
            qh = jnp.where((dim_i >= hh * C_HD) & (dim_i < (hh + 1) * C_HD), qt, 0.0)
            q_hi = qh.astype(BF16)
            q_lo = (qh - q_hi.astype(F32)).astype(BF16)
            gate2 = jnp.dot(km2[:, lanes], q_hi, preferred_element_type=F32)
            gate = (gate2[0:nb, :] + gate2[nb:2 * nb, :]
                    + jnp.dot(km2[0:nb, lanes], q_lo, preferred_element_type=F32))
            sel = _top_blocks(gate, blk_i < j, 0)
            pen = jnp.where((sel > 0.0) | (blk_i == j), 0.0, NEG)
            pen = jnp.concatenate([pen, jnp.zeros((C_HD - nb, tq), F32)], axis=0)
            qs = qh * (C_HD ** -0.5 * LOG2E)
            full = (jnp.concatenate([qs[0:C_HD, :], pen], axis=0) if hh == 0
                    else jnp.concatenate([pen, qs[C_HD:2 * C_HD, :]], axis=0))
            qt_ref[h] = full.astype(BF16)
    m_ref[...] = jnp.full(m_ref.shape, NEG, F32)
    acc_ref[...] = jnp.zeros(acc_ref.shape, F32)

    def scores(heads, n, dst_ref, mask_fn=None):
        r0 = pl.multiple_of(n * MOBA_BLOCK, MOBA_BLOCK)
        for i, h in enumerate(heads):
            kblk = kb_ref[h % 2, pl.ds(r0, MOBA_BLOCK), (h // 2) * 128:(h // 2 + 1) * 128]
            s = jnp.dot(kblk, qt_ref[h], preferred_element_type=F32)
            dst_ref[i] = s if mask_fn is None else mask_fn(s)

    def accumulate(heads, n, src_ref, weight=None):
        for i, h in enumerate(heads):
            rows = slice(h * V_ROWS, (h + 1) * V_ROWS)
            m = m_ref[h:h + 1, :]
            m_new = jnp.maximum(m, jnp.max(src_ref[i], axis=0, keepdims=True))
            a = jnp.exp2(m - m_new)
            p = jnp.exp2(src_ref[i] - m_new).astype(BF16)
            pv = jnp.dot(vtb_ref[n, rows, :], p, preferred_element_type=F32)
            if weight is not None:
                pv = pv * weight
            acc_ref[rows, :] = a * acc_ref[rows, :] + pv
            m_ref[h:h + 1, :] = m_new

    key_i = lax.broadcasted_iota(jnp.int32, (MOBA_BLOCK, tq), 0)
    qry_i = lax.broadcasted_iota(jnp.int32, (MOBA_BLOCK, tq), 1)
    causal = key_i <= qry_i
    for h0 in range(0, C_HEADS, MOBA_HEADS_PER_LOOP):
        heads = tuple(range(h0, h0 + MOBA_HEADS_PER_LOOP))
        scores(heads, j, sa_ref, lambda s: jnp.where(causal, s, NEG))

        def body(i, carry, heads=heads):
            n0 = 2 * i
            n1 = n0 + 1
            scores(heads, n0, sb_ref)
            accumulate(heads, jnp.where(i == 0, j, n0 - 1), sa_ref)
            n1c = jnp.minimum(n1, j - 1)
            scores(heads, n1c, sa_ref)
            accumulate(heads, n0, sb_ref)
            return carry

        trips = (j + 1) // 2
        lax.fori_loop(0, trips, body, 0)
        last_valid = jnp.where(j % 2 == 1, 0.0, 1.0)
        accumulate(heads, jnp.where(j == 0, j, j - 1), sa_ref, weight=last_valid)
    out_t = jnp.concatenate(
        [acc_ref[h * V_ROWS:h * V_ROWS + C_HD, :] / acc_ref[h * V_ROWS + C_HD:h * V_ROWS + C_HD + 1, :]
         for h in range(C_HEADS)], axis=0)
    o_ref[...] = out_t.T


def _moba_prompt(parts, tab, kb, vtb, kmean, batch, seq):
    n = parts.shape[0]
    nb = seq // MOBA_BLOCK
    return pl.pallas_call(
        _moba_prompt_kernel,
        grid=(batch, nb),
        in_specs=[
            pl.BlockSpec((MOBA_BLOCK, BRANCH_W), lambda b, j: (b * nb + j, P_CQ)),
            pl.BlockSpec((MOBA_BLOCK, 384), lambda b, j: (j, 0)),
            pl.BlockSpec((2, seq, BRANCH_W), lambda b, j: (0, b, 0)),
            pl.BlockSpec((nb, C_HEADS * V_ROWS, MOBA_BLOCK), lambda b, j: (b, 0, 0)),
            pl.BlockSpec((1, nb, BRANCH_W), lambda b, j: (b, 0, 0)),
        ],
        out_specs=pl.BlockSpec((MOBA_BLOCK, BRANCH_W), lambda b, j: (b * nb + j, 0)),
        out_shape=jax.ShapeDtypeStruct((n, BRANCH_W), F32),
        scratch_shapes=[pltpu.VMEM((C_HEADS, 128, MOBA_BLOCK), BF16),
                        pltpu.VMEM((C_HEADS, MOBA_BLOCK), F32),
                        pltpu.VMEM((C_HEADS * V_ROWS, MOBA_BLOCK), F32),
                        pltpu.VMEM((MOBA_HEADS_PER_LOOP, MOBA_BLOCK, MOBA_BLOCK), F32),
                        pltpu.VMEM((MOBA_HEADS_PER_LOOP, MOBA_BLOCK, MOBA_BLOCK), F32)],
        compiler_params=_cparams(("parallel", "arbitrary")),
        name="moba_prompt",
    )(parts, tab, kb, vtb, kmean)


def _merge_gates(x, gpre_ref, wm_ref, bm_ref):
    h = _rms(x, gpre_ref[...]).astype(BF16)
    cols = lambda n: slice(n * D_MODEL, (n + 1) * D_MODEL)
    return [_sigmoid(jnp.dot(h, wm_ref[:, cols(n)], preferred_element_type=F32) + bm_ref[:, cols(n)])
            for n in range(N_BRANCH)]


def _merge_out(x, gates, branches, gpost_ref, wb_ref, wo_ref):
    acc = None
    for n, (gate, br) in enumerate(zip(gates, branches)):
        proj = jnp.dot(br.astype(BF16), wb_ref[n], preferred_element_type=F32)
        acc = gate * proj if acc is None else acc + gate * proj
    y = jnp.dot(acc.astype(BF16), wo_ref[...], preferred_element_type=F32)
    return x + _rms(y, gpost_ref[...])


def _mem_scores(problems):
    lanes = lambda h: slice(h * M_HD, (h + 1) * M_HD)
    return [[lax.dot_general(mq[:, lanes(h)].astype(BF16), mk_head(h).astype(BF16), NT_DIMS,
                             preferred_element_type=F32) * (M_HD ** -0.5) for h in range(M_HEADS)]
            for mq, mk_head, _ in problems]


def _mem_values(problems, scores):
    probs = [[jnp.exp(s - jnp.max(s, axis=1, keepdims=True)) for s in per_head] for per_head in scores]
    return [jnp.concatenate(
        [jnp.dot(p.astype(BF16), mv_head(h).astype(BF16), preferred_element_type=F32)
         / jnp.sum(p, axis=1, keepdims=True) for h, p in enumerate(per_head)], axis=1)
        for (_, _, mv_head), per_head in zip(problems, probs)]


def _mem_attend(problems):
    return _mem_values(problems, _mem_scores(problems))


TAIL_TM = 512


def _prompt_tail_kernel(x_ref, au_ref, av_ref, ag_ref, bb_ref, bc_ref, bx_ref, bg_ref, cg_ref, mq_ref, mg_ref,
                        hc_ref, hx_ref, oc_ref, mk_ref, mv_ref,
                        gpre_ref, gpost_ref, lng_ref, ws_ref, bs_ref, cw_ref, wm_ref, bm_ref, wb_ref, wo_ref,
                        y_ref, zt_ref, *, tiles_per_seq):
    tm = TAIL_TM
    x = x_ref[...]
    vn = _layernorm(av_ref[...], lng_ref[...]).astype(BF16)
    t_i = lax.broadcasted_iota(jnp.int32, (CHUNK, CHUNK), 0)
    s_i = lax.broadcasted_iota(jnp.int32, (CHUNK, CHUNK), 1)
    sp_cols = []
    for g in range(A_GROUPS):
        lanes = slice(g * 128, (g + 1) * 128)
        ws = jnp.where(s_i <= t_i, ws_ref[g], 0.0).astype(BF16)
        bias = bs_ref[:, g:g + 1]
        rows = [jnp.dot(ws, vn[c * CHUNK:(c + 1) * CHUNK, lanes], preferred_element_type=F32) + bias
                for c in range(tm // CHUNK)]
        sp_cols.append(jnp.concatenate(rows, axis=0))
    br_a = _silu(ag_ref[...]) * (au_ref[...] * jnp.concatenate(sp_cols, axis=1))
    mem = [(mq_ref[...], lambda h: mk_ref[:, h * M_HD:(h + 1) * M_HD], lambda h: mv_ref[:, h * M_HD:(h + 1) * M_HD])]
    out_m, = _mem_attend(mem)
    br_m = _silu(mg_ref[...]) * out_m
    z = bc_ref[...] * bx_ref[...]
    first_of_seq = (pl.program_id(0) % tiles_per_seq) == 0
    halo = jnp.where(first_of_seq, 0.0, hc_ref[...] * hx_ref[...])
    rix = lax.broadcasted_iota(jnp.int32, (tm, BRANCH_W), 0)
    z1 = jnp.where(rix == 0, halo[7:8, :], pltpu.roll(z, 1, axis=0))
    z2 = pltpu.roll(z, 2, axis=0)
    z2 = jnp.where(rix == 0, halo[6:7, :], jnp.where(rix == 1, halo[7:8, :], z2))
    conv = cw_ref[0:1, :] * z2 + cw_ref[1:2, :] * z1 + cw_ref[2:3, :] * z
    br_b = _silu(bg_ref[...]) * (bb_ref[...] * conv)
    zt_ref[0] = z[tm - 8:tm, :]
    br_c = _silu(cg_ref[...]) * oc_ref[...]
    gates = _merge_gates(x, gpre_ref, wm_ref, bm_ref)
    y_ref[...] = _merge_out(x, gates, (br_a, br_b, br_c, br_m), gpost_ref, wb_ref, wo_ref)


def _const_spec(shape):
    zeros = (0,) * len(shape)
    return pl.BlockSpec(shape, lambda i: zeros, pipeline_mode=pl.Buffered(1))


def _prompt_tail(x, parts, out_c, mkv, lw, seq):
    n = x.shape[0]
    tm = TAIL_TM
    tiles_per_seq = seq // tm
    part = lambda c: pl.BlockSpec((tm, BRANCH_W), lambda i: (i, c))
    halo = lambda c: pl.BlockSpec((8, BRANCH_W), lambda i: (jnp.maximum(i * (tm // 8) - 1, 0), c))
    row512 = pl.BlockSpec((tm, BRANCH_W), lambda i: (i, 0))
    in_specs = [
        pl.BlockSpec((tm, D_MODEL), lambda i: (i, 0)),
        part(P_AU), part(P_AV), part(P_AG), part(P_BB), part(P_BC), part(P_BX), part(P_BG),
        part(P_CG), part(P_MQ), part(P_MG),
        halo(P_BC), halo(P_BX),
        row512,
        pl.BlockSpec((N_MEM, BRANCH_W), lambda i: (i // tiles_per_seq, 0)),
        pl.BlockSpec((N_MEM, BRANCH_W), lambda i: (i // tiles_per_seq, 1)),
        _const_spec((1, D_MODEL)), _const_spec((1, D_MODEL)), _const_spec((1, BRANCH_W)),
        _const_spec((A_GROUPS, CHUNK, CHUNK)), _const_spec((CHUNK, A_GROUPS)), _const_spec((CONV_W, BRANCH_W)),
        _const_spec((D_MODEL, 4 * D_MODEL)), _const_spec((1, 4 * D_MODEL)),
        _const_spec((4, BRANCH_W, D_MODEL)), _const_spec((D_MODEL, D_MODEL)),
    ]
    return pl.pallas_call(
        functools.partial(_prompt_tail_kernel, tiles_per_seq=tiles_per_seq),
        grid=(n // tm,),
        in_specs=in_specs,
        out_specs=[pl.BlockSpec((tm, D_MODEL), lambda i: (i, 0)),
                   pl.BlockSpec((1, 8, BRANCH_W), lambda i: (i, 0, 0))],
        out_shape=[jax.ShapeDtypeStruct((n, D_MODEL), F32),
                   jax.ShapeDtypeStruct((n // tm, 8, BRANCH_W), F32)],
        compiler_params=_cparams(("parallel",)),
        name="prompt_tail",
    )(x, *([parts] * 10), parts, parts, out_c, mkv, mkv,
      lw["g_pre"], lw["g_post"], lw["ln_v_gain"], lw["w_spatial"], lw["b_spatial_t"], lw["conv_w"],
      lw["w_merge"], lw["b_merge"], lw["w_branch"], lw["w_out"])


N_PAGES = 16
PAGE = 128
PAGES_PER_BLOCK = MOBA_BLOCK // PAGE


SAMPLE_SEQS = 2


def _sample_branch_kernel(pt_ref, parts_ref, tab_ref, conv_ref, mk_ref, mv_ref, *rest, dec_seq):
    ns = SAMPLE_SEQS
    k_pages = [rest[s * N_PAGES:(s + 1) * N_PAGES] for s in range(ns)]
    v_pages = [rest[(ns + s) * N_PAGES:(ns + s + 1) * N_PAGES] for s in range(ns)]
    lng_ref, wsx_ref, bsx_ref, cw_ref, ko_ref, vo_ref, co_ref, vn_ref, br_ref = rest[2 * ns * N_PAGES:]
    del pt_ref
    r8 = SAMPLE_ROWS
    pad = jnp.zeros((r8 - dec_seq, BRANCH_W), F32)
    part = lambda s, c: jnp.concatenate([parts_ref[s, :, c * BRANCH_W:(c + 1) * BRANCH_W], pad], axis=0)
    new_rows = lambda x: x[0:dec_seq, :]
    rix = lax.broadcasted_iota(jnp.int32, (r8, BRANCH_W), 0)

    for s in range(ns):
        vn = _layernorm(part(s, P_AV), lng_ref[...])
        vn_ref[s] = new_rows(vn)
        sp = bsx_ref[...]
        for i in range(dec_seq):
            sp = sp + jnp.where(rix >= i, wsx_ref[i], 0.0) * vn[i:i + 1, :]
        br_ref[s, :, 0:BRANCH_W] = new_rows(_silu(part(s, P_AG)) * (part(s, P_AU) * sp))

        z = part(s, P_BC) * part(s, P_BX)
        prev = conv_ref[0, s]
        zrow = lambda i, prev=prev, z=z: (prev[i:i + 1, :] if i < CONV_W - 1
                                          else z[i - (CONV_W - 1):i - (CONV_W - 2), :])
        conv = jnp.zeros((r8, BRANCH_W), F32)
        for t in range(dec_seq):
            c_t = cw_ref[0:1, :] * zrow(t) + cw_ref[1:2, :] * zrow(t + 1) + cw_ref[2:3, :] * zrow(t + 2)
            conv = jnp.where(rix == t, c_t, conv)
        br_ref[s, :, BRANCH_W:2 * BRANCH_W] = new_rows(_silu(part(s, P_BG)) * (part(s, P_BB) * conv))
        co_ref[s, 0] = z[dec_seq - (CONV_W - 1):dec_seq, :]

    tab = tab_ref[...]
    q = [_rope(part(s, P_CQ), tab) for s in range(ns)]
    k = [_rope(part(s, P_CK), tab) for s in range(ns)]
    v = [part(s, P_CV) for s in range(ns)]
    for s in range(ns):
        ko_ref[s] = new_rows(k[s])
        vo_ref[s] = new_rows(v[s])
    n_past = N_PAGES // PAGES_PER_BLOCK
    scale = C_HD ** -0.5
    hr = ns * C_HEADS * r8
    stack = lambda f: jnp.concatenate([f(s, h) for s in range(ns) for h in range(C_HEADS)], axis=0)
    head_lanes = lambda h: slice(h * C_HD, (h + 1) * C_HD)
    group = lambda x, s, h: x[(s * C_HEADS + h) * r8:(s * C_HEADS + h + 1) * r8, :]
    q_rows = stack(lambda s, h: q[s][:, head_lanes(h)]) * scale
    t_q = lax.broadcasted_iota(jnp.int32, (hr, 1), 0) % r8

    def head_block_t(pages, n, h):
        return jnp.concatenate([pages[n * PAGES_PER_BLOCK + i][0, 0, h] for i in range(PAGES_PER_BLOCK)], axis=1)

    s_own = []
    for c in range(dec_seq):
        k_c = stack(lambda s, h: jnp.broadcast_to(k[s][c:c + 1, head_lanes(h)], (r8, C_HD)))
        s_own.append(jnp.where(t_q >= c, jnp.sum(q_rows * k_c, axis=1, keepdims=True), NEG))
    q_hi = q_rows.astype(BF16).astype(F32)
    q_lo = q_rows - q_hi

    q2 = {(s, h): jnp.concatenate([group(q_hi, s, h), group(q_lo, s, h)], axis=0).astype(BF16)
          for s in range(ns) for h in range(C_HEADS)}

    def past_scores(s, h, n):
        s2 = jnp.dot(q2[s, h], head_block_t(k_pages[s], n, h).astype(BF16), preferred_element_type=F32)
        return s2[0:r8, :] + s2[r8:2 * r8, :]

    s_past = []
    gate = jnp.zeros((hr, GATE_COLS), F32)
    gate_col = lax.broadcasted_iota(jnp.int32, (hr, GATE_COLS), 1)
    for n in range(n_past):
        s_n = stack(lambda s, h: past_scores(s, h, n))
        s_past.append(s_n)
        gate = jnp.where(gate_col == n, jnp.sum(s_n, axis=1, keepdims=True) * (1.0 / MOBA_BLOCK), gate)
    sel = _top_blocks(gate, gate_col < n_past, 1)
    s_past = [jnp.where(sel[:, n:n + 1] > 0.0, s_past[n], NEG) for n in range(n_past)]
    m = s_own[0]
    for s_c in s_own[1:]:
        m = jnp.maximum(m, s_c)
    for s_n in s_past:
        m = jnp.maximum(m, jnp.max(s_n, axis=1, keepdims=True))
    l = jnp.zeros((hr, 1), F32)
    acc = jnp.zeros((hr, C_HD), F32)
    for c in range(dec_seq):
        v_c = stack(lambda s, h: jnp.broadcast_to(v[s][c:c + 1, head_lanes(h)], (r8, C_HD)))
        p = jnp.exp(s_own[c] - m)
        l = l + p
        acc = acc + p * v_c
    for n in range(n_past):
        p = jnp.exp(s_past[n] - m)
        l = l + jnp.sum(p, axis=1, keepdims=True)
        p = p.astype(BF16)
        acc = acc + stack(lambda s, h: lax.dot_general(group(p, s, h), head_block_t(v_pages[s], n, h).astype(BF16),
                                                       NT_DIMS, preferred_element_type=F32))
    acc = acc / l

    out_m = _mem_attend([(part(s, P_MQ),
                          lambda h, s=s: mk_ref[0, s, pl.ds(h, N_MEM, stride=M_HEADS), :],
                          lambda h, s=s: mv_ref[0, s, pl.ds(h, N_MEM, stride=M_HEADS), :]) for s in range(ns)])
    for s in range(ns):
        out_c = jnp.concatenate([group(acc, s, h) for h in range(C_HEADS)], axis=1)
        br_ref[s, :, 2 * BRANCH_W:3 * BRANCH_W] = new_rows(_silu(part(s, P_CG)) * out_c)
        br_ref[s, :, 3 * BRANCH_W:4 * BRANCH_W] = new_rows(_silu(part(s, P_MG)) * out_m[s])


def _sample_branch(layer, parts3, tab, state_conv, cache_k, cache_v, cache_mem_k, cache_mem_v, page_table, lw,
                   dec_seq):
    bs = parts3.shape[0]
    r8 = SAMPLE_ROWS
    ns = SAMPLE_SEQS
    page_spec = lambda s, p: pl.BlockSpec((1, 1, C_HEADS, C_HD, PAGE),
                                          lambda b, pt: (layer, pt[ns * b + s, p], 0, 0, 0))
    page_specs = [page_spec(s, p) for s in range(ns) for p in range(N_PAGES)]
    const = lambda shape: pl.BlockSpec(shape, lambda b, pt: (0,) * len(shape))
    seq_spec = lambda rows, w: pl.BlockSpec((ns, rows, w), lambda b, pt: (b, 0, 0))
    in_specs = [
        seq_spec(dec_seq, IN_W),
        const((r8, 384)),
        pl.BlockSpec((1, ns, CONV_W - 1, BRANCH_W), lambda b, pt: (layer, b, 0, 0)),
        pl.BlockSpec((1, ns, N_MEM * M_HEADS, M_HD), lambda b, pt: (layer, b, 0, 0)),
        pl.BlockSpec((1, ns, N_MEM * M_HEADS, M_HD), lambda b, pt: (layer, b, 0, 0)),
    ] + page_specs + page_specs + [
        const((1, BRANCH_W)), const((dec_seq, r8, BRANCH_W)), const((r8, BRANCH_W)), const((CONV_W, BRANCH_W)),
    ]
    out_specs = [seq_spec(dec_seq, BRANCH_W), seq_spec(dec_seq, BRANCH_W),
                 pl.BlockSpec((ns, 1, CONV_W - 1, BRANCH_W), lambda b, pt: (b, 0, 0, 0)),
                 seq_spec(dec_seq, BRANCH_W), seq_spec(dec_seq, 4 * BRANCH_W)]
    out_shape = [jax.ShapeDtypeStruct((bs, dec_seq, BRANCH_W), F32),
                 jax.ShapeDtypeStruct((bs, dec_seq, BRANCH_W), F32),
                 jax.ShapeDtypeStruct((bs, 1, CONV_W - 1, BRANCH_W), F32),
                 jax.ShapeDtypeStruct((bs, dec_seq, BRANCH_W), F32),
                 jax.ShapeDtypeStruct((bs, dec_seq, 4 * BRANCH_W), F32)]
    grid_spec = pltpu.PrefetchScalarGridSpec(
        num_scalar_prefetch=1, grid=(bs // ns,), in_specs=in_specs, out_specs=out_specs)
    return pl.pallas_call(
        functools.partial(_sample_branch_kernel, dec_seq=dec_seq),
        grid_spec=grid_spec,
        out_shape=out_shape,
        compiler_params=_cparams(("arbitrary",)),
        name="sample_branch",
    )(page_table, parts3, tab, state_conv, cache_mem_k, cache_mem_v,
      *([cache_k] * (ns * N_PAGES)), *([cache_v] * (ns * N_PAGES)),
      lw["ln_v_gain"], lw["w_spatial_x"], lw["b_spatial_x"], lw["conv_w"])


def _sample_tail_kernel(x_ref, br_ref, gpre_ref, gpost_ref, wm_ref, bm_ref, wb_ref, wo_ref, y_ref):
    branches = tuple(br_ref[:, n * BRANCH_W:(n + 1) * BRANCH_W] for n in range(N_BRANCH))
    x = x_ref[...]
    y_ref[...] = _merge_out(x, _merge_gates(x, gpre_ref, wm_ref, bm_ref), branches, gpost_ref, wb_ref, wo_ref)


def _sample_tail(x, branches, lw):
    n = x.shape[0]
    tm = TAIL_TM
    return pl.pallas_call(
        _sample_tail_kernel,
        grid=(n // tm,),
        in_specs=[
            pl.BlockSpec((tm, D_MODEL), lambda i: (i, 0)),
            pl.BlockSpec((tm, 4 * BRANCH_W), lambda i: (i, 0)),
            _const_spec((1, D_MODEL)), _const_spec((1, D_MODEL)),
            _const_spec((D_MODEL, 4 * D_MODEL)), _const_spec((1, 4 * D_MODEL)),
            _const_spec((4, BRANCH_W, D_MODEL)), _const_spec((D_MODEL, D_MODEL)),
        ],
        out_specs=pl.BlockSpec((tm, D_MODEL), lambda i: (i, 0)),
        out_shape=jax.ShapeDtypeStruct((n, D_MODEL), F32),
        compiler_params=_cparams(("parallel",)),
        name="sample_tail",
    )(x, branches, lw["g_pre"], lw["g_post"], lw["w_merge"], lw["b_merge"], lw["w_branch"], lw["w_out"])


def kernel(x_prompt, x_sample, cache_k, cache_v, cache_mem_k, cache_mem_v, state_conv, page_table, mem_prompt,
           g_pre, g_post, w_in, ln_v_gain, w_spatial, b_spatial, conv_w, g_mem, w_mem_kv, w_merge, b_merge,
           w_branch, w_out):
    bp, tp, d = x_prompt.shape
    bs, ts, _ = x_sample.shape
    depth = w_in.shape[0]
    page = cache_k.shape[2]
    past_len = page_table.shape[1] * page
    assert (d, page, page_table.shape[1]) == (D_MODEL, PAGE, N_PAGES) and ts <= SAMPLE_ROWS
    assert past_len % MOBA_BLOCK == 0 and tp % KV_TM == 0 and ts >= CONV_W - 1
    r8 = SAMPLE_ROWS

    tab_p = _rope_table(jnp.arange(tp, dtype=jnp.int32))
    tab_s = _rope_table(past_len + jnp.arange(r8, dtype=jnp.int32))
    xp = x_prompt.reshape(bp * tp, d)
    xs = x_sample.reshape(bs * ts, d)
    mem = mem_prompt.reshape(bp * N_MEM, d)
    ck = jnp.transpose(cache_k, (0, 1, 3, 4, 2))
    cv = jnp.transpose(cache_v, (0, 1, 3, 4, 2))
    cmk = cache_mem_k.reshape(depth, bs, N_MEM * M_HEADS, M_HD)
    cmv = cache_mem_v.reshape(depth, bs, N_MEM * M_HEADS, M_HD)
    lane_group = jnp.arange(BRANCH_W) // (BRANCH_W // A_GROUPS)

    outs = {name: [] for name in ("cp", "mkp", "mvp", "ks", "vs", "cs", "vns")}
    kv_state = tuple(jnp.zeros((depth, bp, C_HEADS, C_HD, tp), F32) for _ in range(2))
    for l in range(depth):
        wsx = jnp.transpose(w_spatial[l][:, :r8, :ts], (2, 1, 0))[:, :, lane_group]
        bsx = jnp.transpose(b_spatial[l][:, :r8], (1, 0))[:, lane_group]
        lw = {
            "g_pre": g_pre[l].reshape(1, d), "g_post": g_post[l].reshape(1, d),
            "ln_v_gain": ln_v_gain[l].reshape(1, BRANCH_W),
            "w_spatial": w_spatial[l], "b_spatial_t": b_spatial[l].T,
            "w_spatial_x": wsx, "b_spatial_x": bsx,
            "conv_w": conv_w[l],
            "w_merge": w_merge[l].astype(BF16), "b_merge": b_merge[l].reshape(1, 4 * d),
            "w_branch": w_branch[l].astype(BF16), "w_out": w_out[l].astype(BF16),
        }
        w_in_b = w_in[l].astype(BF16)

        mkv = _inproj(mem, g_mem[l], w_mem_kv[l].astype(BF16), 1024, 1024)
        parts = _inproj(xp, g_pre[l], w_in_b, 1024, 3328)
        k_p, v_p, kb, vb, kmean = _kvpost(parts, tab_p, tp, l, depth, kv_state)
        kv_state = (k_p, v_p)
        kmean = kmean.reshape(bp, tp // MOBA_BLOCK, BRANCH_W)
        out_c = _moba_prompt(parts, tab_p, kb, vb, kmean, bp, tp)
        xp, ztail = _prompt_tail(xp, parts, out_c, mkv, lw, tp)
        outs["cp"].append(ztail.reshape(bp, tp // TAIL_TM, 8, BRANCH_W)[:, -1, 8 - (CONV_W - 1):, :])
        outs["mkp"].append(mkv[:, :BRANCH_W].reshape(bp, N_MEM, M_HEADS, M_HD))
        outs["mvp"].append(mkv[:, BRANCH_W:].reshape(bp, N_MEM, M_HEADS, M_HD))

        parts_s = _inproj(xs, g_pre[l], w_in_b, bs * ts, 1664).reshape(bs, ts, IN_W)
        k_s, v_s, c_s, vn_s, br_s = _sample_branch(l, parts_s, tab_s, state_conv, ck, cv, cmk, cmv, page_table,
                                                   lw, ts)
        xs = _sample_tail(xs, br_s.reshape(bs * ts, 4 * BRANCH_W), lw)
        outs["ks"].append(k_s.reshape(bs, ts, C_HEADS, C_HD))
        outs["vs"].append(v_s.reshape(bs, ts, C_HEADS, C_HD))
        outs["cs"].append(c_s.reshape(bs, CONV_W - 1, BRANCH_W))
        outs["vns"].append(vn_s)

    st = lambda name: jnp.stack(outs[name])
    return (xp.reshape(bp, tp, d), xs.reshape(bs, ts, d),
            jnp.transpose(kv_state[0], (0, 1, 4, 2, 3)), jnp.transpose(kv_state[1], (0, 1, 4, 2, 3)),
            st("cp"), st("mkp"), st("mvp"),
            st("ks"), st("vs"), st("cs"), st("vns"))
```

```python
import functools

import jax
import jax.numpy as jnp
from jax import lax
from jax.experimental import pallas as pl
from jax.experimental.pallas import tpu as pltpu

D_MODEL = 1024
BRANCH_W = 512
N_PARTS = 13
IN_W = N_PARTS * BRANCH_W
CHUNK = 128
A_GROUPS = 4
CONV_W = 3
N_BRANCH = 4
C_HEADS = 8
C_HD = 64
ROPE_DIM = 16
ROPE_THETA = 500000.0
MOBA_BLOCK = 256
MOBA_TOPK = 3
M_HEADS = 4
M_HD = 128
N_MEM = 256
EPS = 1e-6

P_AU, P_AV, P_AG, P_BB, P_BC, P_BX, P_BG, P_CQ, P_CK, P_CV, P_CG, P_MQ, P_MG = range(13)

SAMPLE_ROWS = 8
NEG = -1e30
GATE_COLS = 128
VMEM_LIMIT = 56 * 1024 * 1024

F32 = jnp.float32
BF16 = jnp.bfloat16
NT_DIMS = (((1,), (1,)), ((), ()))


def _cparams(sem):
    return pltpu.CompilerParams(dimension_semantics=sem, vmem_limit_bytes=VMEM_LIMIT)


def _rms(x, g):
    r = lax.rsqrt(jnp.mean(x * x, axis=-1, keepdims=True) + EPS)
    return (x * r) * g


def _layernorm(x, g):
    mu = jnp.mean(x, axis=-1, keepdims=True)
    xc = x - mu
    r = lax.rsqrt(jnp.mean(xc * xc, axis=-1, keepdims=True) + EPS)
    return (xc * r) * g


def _sigmoid(x):
    return 1.0 / (1.0 + jnp.exp(-x))


def _silu(x):
    return x * _sigmoid(x)


def _rope(x, tab):
    c, s_lo, s_hi = tab[:, 0:128], tab[:, 128:256], tab[:, 256:384]
    half = ROPE_DIM // 2
    outs = []
    for g in range(BRANCH_W // 128):
        xg = x[:, g * 128:(g + 1) * 128]
        outs.append(xg * c + pltpu.roll(xg, 128 - half, axis=1) * s_lo + pltpu.roll(xg, half, axis=1) * s_hi)
    return jnp.concatenate(outs, axis=1)


def _rope_table(pos):
    half = ROPE_DIM // 2
    inv = jnp.power(jnp.float32(ROPE_THETA), -jnp.arange(half, dtype=F32) * (2.0 / ROPE_DIM))
    ang = pos.astype(F32)[:, None] * inv[None, :]
    cos, sin = jnp.cos(ang), jnp.sin(ang)
    d = jnp.arange(128) % C_HD
    idx = d % half
    cosl = jnp.where(d[None, :] < ROPE_DIM, cos[:, idx], 1.0)
    s_lo = jnp.where(d[None, :] < half, -sin[:, idx], 0.0)
    s_hi = jnp.where((d[None, :] >= half) & (d[None, :] < ROPE_DIM), sin[:, idx], 0.0)
    return jnp.concatenate([cosl, s_lo, s_hi], axis=1).astype(F32)


def _top_blocks(gate, valid, axis):
    idx = lax.broadcasted_iota(jnp.int32, gate.shape, axis).astype(F32)
    g = jnp.where(valid, gate, -jnp.inf)
    sel = jnp.zeros(gate.shape, F32)
    for _ in range(MOBA_TOPK):
        m = jnp.max(g, axis=axis, keepdims=True)
        cand = (g == m) & valid & (sel == 0.0)
        first = jnp.min(jnp.where(cand, idx, float(gate.shape[axis])), axis=axis, keepdims=True)
        pick = idx == first
        sel = jnp.where(pick, 1.0, sel)
        g = jnp.where(pick, -jnp.inf, g)
    return sel


def _inproj_kernel(x_ref, g_ref, w_ref, o_ref):
    h = _rms(x_ref[...], g_ref[...]).astype(BF16)
    o_ref[...] = jnp.dot(h, w_ref[...], preferred_element_type=F32)


def _inproj(x, g, w_bf16, tm, tn):
    n, d = x.shape
    n_out = w_bf16.shape[1]
    return pl.pallas_call(
        _inproj_kernel,
        grid=(n_out // tn, n // tm),
        in_specs=[
            pl.BlockSpec((tm, d), lambda j, i: (i, 0)),
            pl.BlockSpec((1, d), lambda j, i: (0, 0)),
            pl.BlockSpec((d, tn), lambda j, i: (0, j)),
        ],
        out_specs=pl.BlockSpec((tm, tn), lambda j, i: (i, j)),
        out_shape=jax.ShapeDtypeStruct((n, n_out), F32),
        compiler_params=_cparams(("parallel", "parallel")),
        name="inproj",
    )(x, g.reshape(1, d), w_bf16)


KV_TM = 1024


V_ROWS = 80
LOG2E = 1.4426950408889634


def _kvpost_kernel(k_ref, v_ref, tab_ref, *rest, tiles_per_seq):
    kt_ref, vt_ref, kb_ref, vtb_ref, km_ref = rest[-5:]
    k = _rope(k_ref[...], tab_ref[...])
    k_t = k.T
    v_t = v_ref[...].T
    kt_ref[0, 0] = k_t.reshape(C_HEADS, C_HD, KV_TM)
    vt_ref[0, 0] = v_t.reshape(C_HEADS, C_HD, KV_TM)
    lane = lax.broadcasted_iota(jnp.int32, (KV_TM, BRANCH_W), 1)
    row = lax.broadcasted_iota(jnp.int32, (KV_TM, BRANCH_W), 0)
    blk = (pl.program_id(0) % tiles_per_seq) * (KV_TM // MOBA_BLOCK) + row // MOBA_BLOCK
    onehot = jnp.where(lane % C_HD == blk, 1.0, 0.0)
    odd_head = (lane // C_HD) % 2 == 1
    kb_ref[0] = jnp.where(odd_head, onehot, k).astype(BF16)
    kb_ref[1] = jnp.where(odd_head, k, onehot).astype(BF16)
    extra = jnp.where(lax.broadcasted_iota(jnp.int32, (V_ROWS - C_HD, MOBA_BLOCK), 0) == 0, 1.0, 0.0)
    for i in range(KV_TM // MOBA_BLOCK):
        rows = slice(i * MOBA_BLOCK, (i + 1) * MOBA_BLOCK)
        pieces = []
        for h in range(C_HEADS):
            pieces += [v_t[h * C_HD:(h + 1) * C_HD, rows], extra]
        vtb_ref[i] = jnp.concatenate(pieces, axis=0).astype(BF16)
        km_ref[0, i:i + 1, :] = jnp.sum(k[rows, :], axis=0, keepdims=True) * (1.0 / MOBA_BLOCK)


def _kvpost(parts, tab, seq, layer, depth, state):
    n = parts.shape[0]
    tiles_per_seq = seq // KV_TM
    blocks_per_tile = KV_TM // MOBA_BLOCK
    assert seq // MOBA_BLOCK <= C_HD
    blk = lambda c: pl.BlockSpec((KV_TM, BRANCH_W), lambda i: (i, c))
    slab = pl.BlockSpec((1, 1, C_HEADS, C_HD, KV_TM),
                        lambda i: (layer, i // tiles_per_seq, 0, 0, i % tiles_per_seq))
    carried = list(state)
    return pl.pallas_call(
        functools.partial(_kvpost_kernel, tiles_per_seq=tiles_per_seq),
        grid=(n // KV_TM,),
        in_specs=[blk(P_CK), blk(P_CV),
                  pl.BlockSpec((KV_TM, 384), lambda i: (i % tiles_per_seq, 0))]
        + [pl.BlockSpec(memory_space=pl.ANY)] * len(carried),
        input_output_aliases={3 + i: i for i in range(len(carried))},
        out_specs=[slab, slab,
                   pl.BlockSpec((2, KV_TM, BRANCH_W), lambda i: (0, i, 0)),
                   pl.BlockSpec((blocks_per_tile, C_HEADS * V_ROWS, MOBA_BLOCK), lambda i: (i, 0, 0)),
                   pl.BlockSpec((1, blocks_per_tile, BRANCH_W), lambda i: (i, 0, 0))],
        out_shape=[jax.ShapeDtypeStruct((depth, n // seq, C_HEADS, C_HD, seq), F32),
                   jax.ShapeDtypeStruct((depth, n // seq, C_HEADS, C_HD, seq), F32),
                   jax.ShapeDtypeStruct((2, n, BRANCH_W), BF16),
                   jax.ShapeDtypeStruct((n // MOBA_BLOCK, C_HEADS * V_ROWS, MOBA_BLOCK), BF16),
                   jax.ShapeDtypeStruct((n // KV_TM, blocks_per_tile, BRANCH_W), F32)],
        compiler_params=_cparams(("parallel",)),
        name="kvpost",
    )(parts, parts, tab, *carried)


MOBA_HEADS_PER_LOOP = 4


def _moba_prompt_kernel(q_ref, tab_ref, kb_ref, vtb_ref, km_ref, o_ref, qt_ref, m_ref, acc_ref, sa_ref, sb_ref):
    j = pl.program_id(1)
    tq = MOBA_BLOCK
    nb = km_ref.shape[1]
    q = _rope(q_ref[...], tab_ref[...])
    km = km_ref[0]
    km_hi = km.astype(BF16).astype(F32)
    km2 = jnp.concatenate([km_hi, km - km_hi], axis=0).astype(BF16)
    blk_i = lax.broadcasted_iota(jnp.int32, (nb, tq), 0)
    dim_i = lax.broadcasted_iota(jnp.int32, (128, tq), 0)
    for g in range(C_HEADS // 2):
        lanes = slice(g * 128, (g + 1) * 128)
        qt = q[:, lanes].T
        for hh in range(2):
            h = 2 * g + hh
            qh = jnp.where((dim_i >= hh * C_HD) & (dim_i < (hh + 1) * C_HD), qt, 0.0)
            q_hi = qh.astype(BF16)
            q_lo = (qh - q_hi.astype(F32)).astype(BF16)
            gate2 = jnp.dot(km2[:, lanes], q_hi, preferred_element_type=F32)
            gate = (gate2[0:nb, :] + gate2[nb:2 * nb, :]
                    + jnp.dot(km2[0:nb, lanes], q_lo, preferred_element_type=F32))
            sel = _top_blocks(gate, blk_i < j, 0)
            pen = jnp.where((sel > 0.0) | (blk_i == j), 0.0, NEG)
            pen = jnp.concatenate([pen, jnp.zeros((C_HD - nb, tq), F32)], axis=0)
            qs = qh * (C_HD ** -0.5 * LOG2E)
            full = (jnp.concatenate([qs[0:C_HD, :], pen], axis=0) if hh == 0
                    else jnp.concatenate([pen, qs[C_HD:2 * C_HD, :]], axis=0))
            qt_ref[h] = full.astype(BF16)
    m_ref[...] = jnp.full(m_ref.shape, NEG, F32)
    acc_ref[...] = jnp.zeros(acc_ref.shape, F32)

    def scores(heads, n, dst_ref, mask_fn=None):
        r0 = pl.multiple_of(n * MOBA_BLOCK, MOBA_BLOCK)
        for i, h in enumerate(heads):
            kblk = kb_ref[h % 2, pl.ds(r0, MOBA_BLOCK), (h // 2) * 128:(h // 2 + 1) * 128]
            s = jnp.dot(kblk, qt_ref[h], preferred_element_type=F32)
            dst_ref[i] = s if mask_fn is None else mask_fn(s)

    def accumulate(heads, n, src_ref, weight=None):
        for i, h in enumerate(heads):
            rows = slice(h * V_ROWS, (h + 1) * V_ROWS)
            m = m_ref[h:h + 1, :]
            m_new = jnp.maximum(m, jnp.max(src_ref[i], axis=0, keepdims=True))
            a = jnp.exp2(m - m_new)
            p = jnp.exp2(src_ref[i] - m_new).astype(BF16)
            pv = jnp.dot(vtb_ref[n, rows, :], p, preferred_element_type=F32)
            if weight is not None:
                pv = pv * weight
            acc_ref[rows, :] = a * acc_ref[rows, :] + pv
            m_ref[h:h + 1, :] = m_new

    key_i = lax.broadcasted_iota(jnp.int32, (MOBA_BLOCK, tq), 0)
    qry_i = lax.broadcasted_iota(jnp.int32, (MOBA_BLOCK, tq), 1)
    causal = key_i <= qry_i
    trips = (j + 1) // 2
    last_valid = jnp.where(j % 2 == 1, 0.0, 1.0)
    finish_previous = None
    for g, h0 in enumerate(range(0, C_HEADS, MOBA_HEADS_PER_LOOP)):
        heads = tuple(range(h0, h0 + MOBA_HEADS_PER_LOOP))
        a_ref, b_ref = (sa_ref, sb_ref) if g % 2 == 0 else (sb_ref, sa_ref)
        scores(heads, j, a_ref, lambda s: jnp.where(causal, s, NEG))
        if finish_previous is not None:
            finish_previous()

        def body(i, carry, heads=heads, a_ref=a_ref, b_ref=b_ref):
            n0 = 2 * i
            n1 = n0 + 1
            scores(heads, n0, b_ref)
            accumulate(heads, jnp.where(i == 0, j, n0 - 1), a_ref)
            n1c = jnp.minimum(n1, j - 1)
            scores(heads, n1c, a_ref)
            accumulate(heads, n0, b_ref)
            return carry

        lax.fori_loop(0, trips, body, 0)
        finish_previous = functools.partial(accumulate, heads, jnp.where(j == 0, j, j - 1), a_ref,
                                            weight=last_valid)
    finish_previous()
    out_t = jnp.concatenate(
        [acc_ref[h * V_ROWS:h * V_ROWS + C_HD, :] / acc_ref[h * V_ROWS + C_HD:h * V_ROWS + C_HD + 1, :]
         for h in range(C_HEADS)], axis=0)
    o_ref[...] = out_t.T


def _moba_prompt(parts, tab, kb, vtb, kmean, batch, seq):
    n = parts.shape[0]
    nb = seq // MOBA_BLOCK
    return pl.pallas_call(
        _moba_prompt_kernel,
        grid=(batch, nb),
        in_specs=[
            pl.BlockSpec((MOBA_BLOCK, BRANCH_W), lambda b, j: (b * nb + j, P_CQ)),
            pl.BlockSpec((MOBA_BLOCK, 384), lambda b, j: (j, 0)),
            pl.BlockSpec((2, seq, BRANCH_W), lambda b, j: (0, b, 0)),
            pl.BlockSpec((nb, C_HEADS * V_ROWS, MOBA_BLOCK), lambda b, j: (b, 0, 0)),
            pl.BlockSpec((1, nb, BRANCH_W), lambda b, j: (b, 0, 0)),
        ],
        out_specs=pl.BlockSpec((MOBA_BLOCK, BRANCH_W), lambda b, j: (b * nb + j, 0)),
        out_shape=jax.ShapeDtypeStruct((n, BRANCH_W), F32),
        scratch_shapes=[pltpu.VMEM((C_HEADS, 128, MOBA_BLOCK), BF16),
                        pltpu.VMEM((C_HEADS, MOBA_BLOCK), F32),
                        pltpu.VMEM((C_HEADS * V_ROWS, MOBA_BLOCK), F32),
                        pltpu.VMEM((MOBA_HEADS_PER_LOOP, MOBA_BLOCK, MOBA_BLOCK), F32),
                        pltpu.VMEM((MOBA_HEADS_PER_LOOP, MOBA_BLOCK, MOBA_BLOCK), F32)],
        compiler_params=_cparams(("parallel", "arbitrary")),
        name="moba_prompt",
    )(parts, tab, kb, vtb, kmean)


def _merge_gates(x, gpre_ref, wm_ref, bm_ref):
    h = _rms(x, gpre_ref[...]).astype(BF16)
    cols = lambda n: slice(n * D_MODEL, (n + 1) * D_MODEL)
    return [_sigmoid(jnp.dot(h, wm_ref[:, cols(n)], preferred_element_type=F32) + bm_ref[:, cols(n)])
            for n in range(N_BRANCH)]


def _merge_out(x, gates, branches, gpost_ref, wb_ref, wo_ref):
    acc = None
    for n, (gate, br) in enumerate(zip(gates, branches)):
        proj = jnp.dot(br.astype(BF16), wb_ref[n], preferred_element_type=F32)
        acc = gate * proj if acc is None else acc + gate * proj
    y = jnp.dot(acc.astype(BF16), wo_ref[...], preferred_element_type=F32)
    return x + _rms(y, gpost_ref[...])


def _mem_scores(problems):
    lanes = lambda h: slice(h * M_HD, (h + 1) * M_HD)
    return [[lax.dot_general(mq[:, lanes(h)].astype(BF16), mk_head(h).astype(BF16), NT_DIMS,
                             preferred_element_type=F32) * (M_HD ** -0.5) for h in range(M_HEADS)]
            for mq, mk_head, _ in problems]


def _mem_values(problems, scores):
    probs = [[jnp.exp(s - jnp.max(s, axis=1, keepdims=True)) for s in per_head] for per_head in scores]
    return [jnp.concatenate(
        [jnp.dot(p.astype(BF16), mv_head(h).astype(BF16), preferred_element_type=F32)
         / jnp.sum(p, axis=1, keepdims=True) for h, p in enumerate(per_head)], axis=1)
        for (_, _, mv_head), per_head in zip(problems, probs)]


def _mem_attend(problems):
    return _mem_values(problems, _mem_scores(problems))


TAIL_TM = 512


def _prompt_tail_kernel(x_ref, au_ref, av_ref, ag_ref, bb_ref, bc_ref, bx_ref, bg_ref, cg_ref, mq_ref, mg_ref,
                        hc_ref, hx_ref, oc_ref, mk_ref, mv_ref,
                        gpre_ref, gpost_ref, lng_ref, ws_ref, bs_ref, cw_ref, wm_ref, bm_ref, wb_ref, wo_ref,
                        y_ref, zt_ref, *, tiles_per_seq):
    tm = TAIL_TM
    x = x_ref[...]
    vn = _layernorm(av_ref[...], lng_ref[...]).astype(BF16)
    t_i = lax.broadcasted_iota(jnp.int32, (CHUNK, CHUNK), 0)
    s_i = lax.broadcasted_iota(jnp.int32, (CHUNK, CHUNK), 1)
    sp_cols = []
    for g in range(A_GROUPS):
        lanes = slice(g * 128, (g + 1) * 128)
        ws = jnp.where(s_i <= t_i, ws_ref[g], 0.0).astype(BF16)
        bias = bs_ref[:, g:g + 1]
        rows = [jnp.dot(ws, vn[c * CHUNK:(c + 1) * CHUNK, lanes], preferred_element_type=F32) + bias
                for c in range(tm // CHUNK)]
        sp_cols.append(jnp.concatenate(rows, axis=0))
    br_a = _silu(ag_ref[...]) * (au_ref[...] * jnp.concatenate(sp_cols, axis=1))
    mem = [(mq_ref[...], lambda h: mk_ref[:, h * M_HD:(h + 1) * M_HD], lambda h: mv_ref[:, h * M_HD:(h + 1) * M_HD])]
    out_m, = _mem_attend(mem)
    br_m = _silu(mg_ref[...]) * out_m
    z = bc_ref[...] * bx_ref[...]
    first_of_seq = (pl.program_id(0) % tiles_per_seq) == 0
    halo = jnp.where(first_of_seq, 0.0, hc_ref[...] * hx_ref[...])
    rix = lax.broadcasted_iota(jnp.int32, (tm, BRANCH_W), 0)
    z1 = jnp.where(rix == 0, halo[7:8, :], pltpu.roll(z, 1, axis=0))
    z2 = pltpu.roll(z, 2, axis=0)
    z2 = jnp.where(rix == 0, halo[6:7, :], jnp.where(rix == 1, halo[7:8, :], z2))
    conv = cw_ref[0:1, :] * z2 + cw_ref[1:2, :] * z1 + cw_ref[2:3, :] * z
    br_b = _silu(bg_ref[...]) * (bb_ref[...] * conv)
    zt_ref[0] = z[tm - 8:tm, :]
    br_c = _silu(cg_ref[...]) * oc_ref[...]
    gates = _merge_gates(x, gpre_ref, wm_ref, bm_ref)
    y_ref[...] = _merge_out(x, gates, (br_a, br_b, br_c, br_m), gpost_ref, wb_ref, wo_ref)


def _const_spec(shape):
    zeros = (0,) * len(shape)
    return pl.BlockSpec(shape, lambda i: zeros, pipeline_mode=pl.Buffered(1))


def _prompt_tail(x, parts, out_c, mkv, lw, seq):
    n = x.shape[0]
    tm = TAIL_TM
    tiles_per_seq = seq // tm
    part = lambda c: pl.BlockSpec((tm, BRANCH_W), lambda i: (i, c))
    halo = lambda c: pl.BlockSpec((8, BRANCH_W), lambda i: (jnp.maximum(i * (tm // 8) - 1, 0), c))
    row512 = pl.BlockSpec((tm, BRANCH_W), lambda i: (i, 0))
    in_specs = [
        pl.BlockSpec((tm, D_MODEL), lambda i: (i, 0)),
        part(P_AU), part(P_AV), part(P_AG), part(P_BB), part(P_BC), part(P_BX), part(P_BG),
        part(P_CG), part(P_MQ), part(P_MG),
        halo(P_BC), halo(P_BX),
        row512,
        pl.BlockSpec((N_MEM, BRANCH_W), lambda i: (i // tiles_per_seq, 0)),
        pl.BlockSpec((N_MEM, BRANCH_W), lambda i: (i // tiles_per_seq, 1)),
        _const_spec((1, D_MODEL)), _const_spec((1, D_MODEL)), _const_spec((1, BRANCH_W)),
        _const_spec((A_GROUPS, CHUNK, CHUNK)), _const_spec((CHUNK, A_GROUPS)), _const_spec((CONV_W, BRANCH_W)),
        _const_spec((D_MODEL, 4 * D_MODEL)), _const_spec((1, 4 * D_MODEL)),
        _const_spec((4, BRANCH_W, D_MODEL)), _const_spec((D_MODEL, D_MODEL)),
    ]
    return pl.pallas_call(
        functools.partial(_prompt_tail_kernel, tiles_per_seq=tiles_per_seq),
        grid=(n // tm,),
        in_specs=in_specs,
        out_specs=[pl.BlockSpec((tm, D_MODEL), lambda i: (i, 0)),
                   pl.BlockSpec((1, 8, BRANCH_W), lambda i: (i, 0, 0))],
        out_shape=[jax.ShapeDtypeStruct((n, D_MODEL), F32),
                   jax.ShapeDtypeStruct((n // tm, 8, BRANCH_W), F32)],
        compiler_params=_cparams(("parallel",)),
        name="prompt_tail",
    )(x, *([parts] * 10), parts, parts, out_c, mkv, mkv,
      lw["g_pre"], lw["g_post"], lw["ln_v_gain"], lw["w_spatial"], lw["b_spatial_t"], lw["conv_w"],
      lw["w_merge"], lw["b_merge"], lw["w_branch"], lw["w_out"])


N_PAGES = 16
PAGE = 128
PAGES_PER_BLOCK = MOBA_BLOCK // PAGE


SAMPLE_SEQS = 2


def _sample_branch_kernel(pt_ref, parts_ref, tab_ref, conv_ref, mk_ref, mv_ref, *rest, dec_seq):
    ns = SAMPLE_SEQS
    k_pages = [rest[s * N_PAGES:(s + 1) * N_PAGES] for s in range(ns)]
    v_pages = [rest[(ns + s) * N_PAGES:(ns + s + 1) * N_PAGES] for s in range(ns)]
    lng_ref, wsx_ref, bsx_ref, cw_ref, ko_ref, vo_ref, co_ref, vn_ref, br_ref = rest[2 * ns * N_PAGES:]
    del pt_ref
    r8 = SAMPLE_ROWS
    pad = jnp.zeros((r8 - dec_seq, BRANCH_W), F32)
    seq_rows = lambda s: slice(s * dec_seq, (s + 1) * dec_seq)
    part = lambda s, c: jnp.concatenate([parts_ref[seq_rows(s), c * BRANCH_W:(c + 1) * BRANCH_W], pad], axis=0)
    new_rows = lambda x: x[0:dec_seq, :]
    rix = lax.broadcasted_iota(jnp.int32, (r8, BRANCH_W), 0)

    for s in range(ns):
        vn = _layernorm(part(s, P_AV), lng_ref[...])
        vn_ref[seq_rows(s), :] = new_rows(vn)
        sp = bsx_ref[...]
        for i in range(dec_seq):
            sp = sp + jnp.where(rix >= i, wsx_ref[i], 0.0) * vn[i:i + 1, :]
        br_ref[seq_rows(s), 0:BRANCH_W] = new_rows(_silu(part(s, P_AG)) * (part(s, P_AU) * sp))

        z = part(s, P_BC) * part(s, P_BX)
        prev = conv_ref[0, s]
        zrow = lambda i, prev=prev, z=z: (prev[i:i + 1, :] if i < CONV_W - 1
                                          else z[i - (CONV_W - 1):i - (CONV_W - 2), :])
        conv = jnp.zeros((r8, BRANCH_W), F32)
        for t in range(dec_seq):
            c_t = cw_ref[0:1, :] * zrow(t) + cw_ref[1:2, :] * zrow(t + 1) + cw_ref[2:3, :] * zrow(t + 2)
            conv = jnp.where(rix == t, c_t, conv)
        br_ref[seq_rows(s), BRANCH_W:2 * BRANCH_W] = new_rows(_silu(part(s, P_BG)) * (part(s, P_BB) * conv))
        co_ref[s, 0] = z[dec_seq - (CONV_W - 1):dec_seq, :]

    tab = tab_ref[...]
    q = [_rope(part(s, P_CQ), tab) for s in range(ns)]
    k = [_rope(part(s, P_CK), tab) for s in range(ns)]
    v = [part(s, P_CV) for s in range(ns)]
    for s in range(ns):
        ko_ref[seq_rows(s), :] = new_rows(k[s])
        vo_ref[seq_rows(s), :] = new_rows(v[s])
    n_past = N_PAGES // PAGES_PER_BLOCK
    scale = C_HD ** -0.5
    hr = ns * C_HEADS * r8
    stack = lambda f: jnp.concatenate([f(s, h) for s in range(ns) for h in range(C_HEADS)], axis=0)
    head_lanes = lambda h: slice(h * C_HD, (h + 1) * C_HD)
    group = lambda x, s, h: x[(s * C_HEADS + h) * r8:(s * C_HEADS + h + 1) * r8, :]
    q_rows = stack(lambda s, h: q[s][:, head_lanes(h)]) * scale
    t_q = lax.broadcasted_iota(jnp.int32, (hr, 1), 0) % r8

    def head_block_t(pages, n, h):
        return jnp.concatenate([pages[n * PAGES_PER_BLOCK + i][0, 0, h] for i in range(PAGES_PER_BLOCK)], axis=1)

    s_own = []
    for c in range(dec_seq):
        k_c = stack(lambda s, h: jnp.broadcast_to(k[s][c:c + 1, head_lanes(h)], (r8, C_HD)))
        s_own.append(jnp.where(t_q >= c, jnp.sum(q_rows * k_c, axis=1, keepdims=True), NEG))
    q_hi = q_rows.astype(BF16).astype(F32)
    q_lo = q_rows - q_hi

    q2 = {(s, h): jnp.concatenate([group(q_hi, s, h), group(q_lo, s, h)], axis=0).astype(BF16)
          for s in range(ns) for h in range(C_HEADS)}

    def past_scores(s, h, n):
        s2 = jnp.dot(q2[s, h], head_block_t(k_pages[s], n, h).astype(BF16), preferred_element_type=F32)
        return s2[0:r8, :] + s2[r8:2 * r8, :]

    s_past = []
    gate = jnp.zeros((hr, GATE_COLS), F32)
    gate_col = lax.broadcasted_iota(jnp.int32, (hr, GATE_COLS), 1)
    for n in range(n_past):
        s_n = stack(lambda s, h: past_scores(s, h, n))
        s_past.append(s_n)
        gate = jnp.where(gate_col == n, jnp.sum(s_n, axis=1, keepdims=True) * (1.0 / MOBA_BLOCK), gate)
    sel = _top_blocks(gate, gate_col < n_past, 1)
    s_past = [jnp.where(sel[:, n:n + 1] > 0.0, s_past[n], NEG) for n in range(n_past)]
    m = s_own[0]
    for s_c in s_own[1:]:
        m = jnp.maximum(m, s_c)
    for s_n in s_past:
        m = jnp.maximum(m, jnp.max(s_n, axis=1, keepdims=True))
    l = jnp.zeros((hr, 1), F32)
    acc = jnp.zeros((hr, C_HD), F32)
    for c in range(dec_seq):
        v_c = stack(lambda s, h: jnp.broadcast_to(v[s][c:c + 1, head_lanes(h)], (r8, C_HD)))
        p = jnp.exp(s_own[c] - m)
        l = l + p
        acc = acc + p * v_c
    for n in range(n_past):
        p = jnp.exp(s_past[n] - m)
        l = l + jnp.sum(p, axis=1, keepdims=True)
        p = p.astype(BF16)
        acc = acc + stack(lambda s, h: lax.dot_general(group(p, s, h), head_block_t(v_pages[s], n, h).astype(BF16),
                                                       NT_DIMS, preferred_element_type=F32))
    acc = acc / l

    out_m = _mem_attend([(part(s, P_MQ),
                          lambda h, s=s: mk_ref[0, s, pl.ds(h, N_MEM, stride=M_HEADS), :],
                          lambda h, s=s: mv_ref[0, s, pl.ds(h, N_MEM, stride=M_HEADS), :]) for s in range(ns)])
    for s in range(ns):
        out_c = jnp.concatenate([group(acc, s, h) for h in range(C_HEADS)], axis=1)
        br_ref[seq_rows(s), 2 * BRANCH_W:3 * BRANCH_W] = new_rows(_silu(part(s, P_CG)) * out_c)
        br_ref[seq_rows(s), 3 * BRANCH_W:4 * BRANCH_W] = new_rows(_silu(part(s, P_MG)) * out_m[s])


def _sample_branch(layer, parts_s, tab, state_conv, cache_k, cache_v, cache_mem_k, cache_mem_v, page_table, lw,
                   dec_seq):
    bs = parts_s.shape[0] // dec_seq
    r8 = SAMPLE_ROWS
    ns = SAMPLE_SEQS
    page_spec = lambda s, p: pl.BlockSpec((1, 1, C_HEADS, C_HD, PAGE),
                                          lambda b, pt: (layer, pt[ns * b + s, p], 0, 0, 0))
    page_specs = [page_spec(s, p) for s in range(ns) for p in range(N_PAGES)]
    const = lambda shape: pl.BlockSpec(shape, lambda b, pt: (0,) * len(shape))
    assert (ns * dec_seq) % SAMPLE_ROWS == 0
    seq_spec = lambda rows, w: pl.BlockSpec((ns * rows, w), lambda b, pt: (b, 0))
    in_specs = [
        seq_spec(dec_seq, IN_W),
        const((r8, 384)),
        pl.BlockSpec((1, ns, CONV_W - 1, BRANCH_W), lambda b, pt: (layer, b, 0, 0)),
        pl.BlockSpec((1, ns, N_MEM * M_HEADS, M_HD), lambda b, pt: (layer, b, 0, 0)),
        pl.BlockSpec((1, ns, N_MEM * M_HEADS, M_HD), lambda b, pt: (layer, b, 0, 0)),
    ] + page_specs + page_specs + [
        const((1, BRANCH_W)), const((dec_seq, r8, BRANCH_W)), const((r8, BRANCH_W)), const((CONV_W, BRANCH_W)),
    ]
    out_specs = [seq_spec(dec_seq, BRANCH_W), seq_spec(dec_seq, BRANCH_W),
                 pl.BlockSpec((ns, 1, CONV_W - 1, BRANCH_W), lambda b, pt: (b, 0, 0, 0)),
                 seq_spec(dec_seq, BRANCH_W), seq_spec(dec_seq, 4 * BRANCH_W)]
    out_shape = [jax.ShapeDtypeStruct((bs * dec_seq, BRANCH_W), F32),
                 jax.ShapeDtypeStruct((bs * dec_seq, BRANCH_W), F32),
                 jax.ShapeDtypeStruct((bs, 1, CONV_W - 1, BRANCH_W), F32),
                 jax.ShapeDtypeStruct((bs * dec_seq, BRANCH_W), F32),
                 jax.ShapeDtypeStruct((bs * dec_seq, 4 * BRANCH_W), F32)]
    grid_spec = pltpu.PrefetchScalarGridSpec(
        num_scalar_prefetch=1, grid=(bs // ns,), in_specs=in_specs, out_specs=out_specs)
    return pl.pallas_call(
        functools.partial(_sample_branch_kernel, dec_seq=dec_seq),
        grid_spec=grid_spec,
        out_shape=out_shape,
        compiler_params=_cparams(("arbitrary",)),
        name="sample_branch",
    )(page_table, parts_s, tab, state_conv, cache_mem_k, cache_mem_v,
      *([cache_k] * (ns * N_PAGES)), *([cache_v] * (ns * N_PAGES)),
      lw["ln_v_gain"], lw["w_spatial_x"], lw["b_spatial_x"], lw["conv_w"])


def _sample_tail_kernel(x_ref, br_ref, gpre_ref, gpost_ref, wm_ref, bm_ref, wb_ref, wo_ref, y_ref):
    branches = tuple(br_ref[:, n * BRANCH_W:(n + 1) * BRANCH_W] for n in range(N_BRANCH))
    x = x_ref[...]
    y_ref[...] = _merge_out(x, _merge_gates(x, gpre_ref, wm_ref, bm_ref), branches, gpost_ref, wb_ref, wo_ref)


def _sample_tail(x, branches, lw):
    n = x.shape[0]
    tm = TAIL_TM
    return pl.pallas_call(
        _sample_tail_kernel,
        grid=(n // tm,),
        in_specs=[
            pl.BlockSpec((tm, D_MODEL), lambda i: (i, 0)),
            pl.BlockSpec((tm, 4 * BRANCH_W), lambda i: (i, 0)),
            _const_spec((1, D_MODEL)), _const_spec((1, D_MODEL)),
            _const_spec((D_MODEL, 4 * D_MODEL)), _const_spec((1, 4 * D_MODEL)),
            _const_spec((4, BRANCH_W, D_MODEL)), _const_spec((D_MODEL, D_MODEL)),
        ],
        out_specs=pl.BlockSpec((tm, D_MODEL), lambda i: (i, 0)),
        out_shape=jax.ShapeDtypeStruct((n, D_MODEL), F32),
        compiler_params=_cparams(("parallel",)),
        name="sample_tail",
    )(x, branches, lw["g_pre"], lw["g_post"], lw["w_merge"], lw["b_merge"], lw["w_branch"], lw["w_out"])


def kernel(x_prompt, x_sample, cache_k, cache_v, cache_mem_k, cache_mem_v, state_conv, page_table, mem_prompt,
           g_pre, g_post, w_in, ln_v_gain, w_spatial, b_spatial, conv_w, g_mem, w_mem_kv, w_merge, b_merge,
           w_branch, w_out):
    bp, tp, d = x_prompt.shape
    bs, ts, _ = x_sample.shape
    depth = w_in.shape[0]
    page = cache_k.shape[2]
    past_len = page_table.shape[1] * page
    assert (d, page, page_table.shape[1]) == (D_MODEL, PAGE, N_PAGES) and ts <= SAMPLE_ROWS
    assert past_len % MOBA_BLOCK == 0 and tp % KV_TM == 0 and ts >= CONV_W - 1
    r8 = SAMPLE_ROWS

    tab_p = _rope_table(jnp.arange(tp, dtype=jnp.int32))
    tab_s = _rope_table(past_len + jnp.arange(r8, dtype=jnp.int32))
    xp = x_prompt.reshape(bp * tp, d)
    xs = x_sample.reshape(bs * ts, d)
    mem = mem_prompt.reshape(bp * N_MEM, d)
    ck = jnp.transpose(cache_k, (0, 1, 3, 4, 2))
    cv = jnp.transpose(cache_v, (0, 1, 3, 4, 2))
    cmk = cache_mem_k.reshape(depth, bs, N_MEM * M_HEADS, M_HD)
    cmv = cache_mem_v.reshape(depth, bs, N_MEM * M_HEADS, M_HD)
    lane_group = jnp.arange(BRANCH_W) // (BRANCH_W // A_GROUPS)

    outs = {name: [] for name in ("cp", "mkp", "mvp", "ks", "vs", "cs", "vns")}
    kv_state = tuple(jnp.zeros((depth, bp, C_HEADS, C_HD, tp), F32) for _ in range(2))
    for l in range(depth):
        wsx = jnp.transpose(w_spatial[l][:, :r8, :ts], (2, 1, 0))[:, :, lane_group]
        bsx = jnp.transpose(b_spatial[l][:, :r8], (1, 0))[:, lane_group]
        lw = {
            "g_pre": g_pre[l].reshape(1, d), "g_post": g_post[l].reshape(1, d),
            "ln_v_gain": ln_v_gain[l].reshape(1, BRANCH_W),
            "w_spatial": w_spatial[l], "b_spatial_t": b_spatial[l].T,
            "w_spatial_x": wsx, "b_spatial_x": bsx,
            "conv_w": conv_w[l],
            "w_merge": w_merge[l].astype(BF16), "b_merge": b_merge[l].reshape(1, 4 * d),
            "w_branch": w_branch[l].astype(BF16), "w_out": w_out[l].astype(BF16),
        }
        w_in_b = w_in[l].astype(BF16)

        mkv = _inproj(mem, g_mem[l], w_mem_kv[l].astype(BF16), 1024, 1024)
        parts = _inproj(xp, g_pre[l], w_in_b, 1024, 3328)
        k_p, v_p, kb, vb, kmean = _kvpost(parts, tab_p, tp, l, depth, kv_state)
        kv_state = (k_p, v_p)
        kmean = kmean.reshape(bp, tp // MOBA_BLOCK, BRANCH_W)
        out_c = _moba_prompt(parts, tab_p, kb, vb, kmean, bp, tp)
        xp, ztail = _prompt_tail(xp, parts, out_c, mkv, lw, tp)
        outs["cp"].append(ztail.reshape(bp, tp // TAIL_TM, 8, BRANCH_W)[:, -1, 8 - (CONV_W - 1):, :])
        outs["mkp"].append(mkv[:, :BRANCH_W].reshape(bp, N_MEM, M_HEADS, M_HD))
        outs["mvp"].append(mkv[:, BRANCH_W:].reshape(bp, N_MEM, M_HEADS, M_HD))

        parts_s = _inproj(xs, g_pre[l], w_in_b, bs * ts, 1664)
        k_s, v_s, c_s, vn_s, br_s = _sample_branch(l, parts_s, tab_s, state_conv, ck, cv, cmk, cmv, page_table,
                                                   lw, ts)
        xs = _sample_tail(xs, br_s, lw)
        outs["ks"].append(k_s.reshape(bs, ts, C_HEADS, C_HD))
        outs["vs"].append(v_s.reshape(bs, ts, C_HEADS, C_HD))
        outs["cs"].append(c_s.reshape(bs, CONV_W - 1, BRANCH_W))
        outs["vns"].append(vn_s.reshape(bs, ts, BRANCH_W))

    st = lambda name: jnp.stack(outs[name])
    return (xp.reshape(bp, tp, d), xs.reshape(bs, ts, d),
            jnp.transpose(kv_state[0], (0, 1, 4, 2, 3)), jnp.transpose(kv_state[1], (0, 1, 4, 2, 3)),
            st("cp"), st("mkp"), st("mvp"),
            st("ks"), st("vs"), st("cs"), st("vns"))
```

```python
import functools

import jax
import jax.numpy as jnp
from jax import lax
from jax.experimental import pallas as pl
from jax.experimental.pallas import tpu as pltpu

D_MODEL = 1024
BRANCH_W = 512
N_PARTS = 13
IN_W = N_PARTS * BRANCH_W
CHUNK = 128
A_GROUPS = 4
CONV_W = 3
N_BRANCH = 4
C_HEADS = 8
C_HD = 64
ROPE_DIM = 16
ROPE_THETA = 500000.0
MOBA_BLOCK = 256
MOBA_TOPK = 3
M_HEADS = 4
M_HD = 128
N_MEM = 256
EPS = 1e-6

P_AU, P_AV, P_AG, P_BB, P_BC, P_BX, P_BG, P_CQ, P_CK, P_CV, P_CG, P_MQ, P_MG = range(13)

SAMPLE_ROWS = 8
NEG = -1e30
GATE_COLS = 128
VMEM_LIMIT = 56 * 1024 * 1024

F32 = jnp.float32
BF16 = jnp.bfloat16
NT_DIMS = (((1,), (1,)), ((), ()))


def _cparams(sem):
    return pltpu.CompilerParams(dimension_semantics=sem, vmem_limit_bytes=VMEM_LIMIT)


def _rms(x, g):
    r = lax.rsqrt(jnp.mean(x * x, axis=-1, keepdims=True) + EPS)
    return (x * r) * g


def _layernorm(x, g):
    mu = jnp.mean(x, axis=-1, keepdims=True)
    xc = x - mu
    r = lax.rsqrt(jnp.mean(xc * xc, axis=-1, keepdims=True) + EPS)
    return (xc * r) * g


def _sigmoid(x):
    return 0.5 * jnp.tanh(0.5 * x) + 0.5


def _silu(x):
    return x * _sigmoid(x)


def _rope(x, tab):
    c, s_lo, s_hi = tab[:, 0:128], tab[:, 128:256], tab[:, 256:384]
    half = ROPE_DIM // 2
    outs = []
    for g in range(BRANCH_W // 128):
        xg = x[:, g * 128:(g + 1) * 128]
        outs.append(xg * c + pltpu.roll(xg, 128 - half, axis=1) * s_lo + pltpu.roll(xg, half, axis=1) * s_hi)
    return jnp.concatenate(outs, axis=1)


def _rope_table(pos):
    half = ROPE_DIM // 2
    inv = jnp.power(jnp.float32(ROPE_THETA), -jnp.arange(half, dtype=F32) * (2.0 / ROPE_DIM))
    ang = pos.astype(F32)[:, None] * inv[None, :]
    cos, sin = jnp.cos(ang), jnp.sin(ang)
    d = jnp.arange(128) % C_HD
    idx = d % half
    cosl = jnp.where(d[None, :] < ROPE_DIM, cos[:, idx], 1.0)
    s_lo = jnp.where(d[None, :] < half, -sin[:, idx], 0.0)
    s_hi = jnp.where((d[None, :] >= half) & (d[None, :] < ROPE_DIM), sin[:, idx], 0.0)
    return jnp.concatenate([cosl, s_lo, s_hi], axis=1).astype(F32)


def _top_blocks(gate, valid, axis):
    idx = lax.broadcasted_iota(jnp.int32, gate.shape, axis).astype(F32)
    g = jnp.where(valid, gate, -jnp.inf)
    sel = jnp.zeros(gate.shape, F32)
    for _ in range(MOBA_TOPK):
        m = jnp.max(g, axis=axis, keepdims=True)
        cand = (g == m) & valid & (sel == 0.0)
        first = jnp.min(jnp.where(cand, idx, float(gate.shape[axis])), axis=axis, keepdims=True)
        pick = idx == first
        sel = jnp.where(pick, 1.0, sel)
        g = jnp.where(pick, -jnp.inf, g)
    return sel


def _inproj_kernel(x_ref, g_ref, w_ref, o_ref):
    h = _rms(x_ref[...], g_ref[...]).astype(BF16)
    o_ref[...] = jnp.dot(h, w_ref[...], preferred_element_type=F32)


def _inproj(x, g, w_bf16, tm, tn):
    n, d = x.shape
    n_out = w_bf16.shape[1]
    return pl.pallas_call(
        _inproj_kernel,
        grid=(n_out // tn, n // tm),
        in_specs=[
            pl.BlockSpec((tm, d), lambda j, i: (i, 0)),
            pl.BlockSpec((1, d), lambda j, i: (0, 0)),
            pl.BlockSpec((d, tn), lambda j, i: (0, j)),
        ],
        out_specs=pl.BlockSpec((tm, tn), lambda j, i: (i, j)),
        out_shape=jax.ShapeDtypeStruct((n, n_out), F32),
        compiler_params=_cparams(("parallel", "parallel")),
        name="inproj",
    )(x, g.reshape(1, d), w_bf16)


KV_TM = 1024


V_ROWS = 80
LOG2E = 1.4426950408889634


def _kvpost_kernel(k_ref, v_ref, tab_ref, *rest, tiles_per_seq):
    kt_ref, vt_ref, kb_ref, vtb_ref, km_ref = rest[-5:]
    k = _rope(k_ref[...], tab_ref[...])
    k_t = k.T
    v_t = v_ref[...].T
    kt_ref[0, 0] = k_t.reshape(C_HEADS, C_HD, KV_TM)
    vt_ref[0, 0] = v_t.reshape(C_HEADS, C_HD, KV_TM)
    lane = lax.broadcasted_iota(jnp.int32, (KV_TM, BRANCH_W), 1)
    row = lax.broadcasted_iota(jnp.int32, (KV_TM, BRANCH_W), 0)
    blk = (pl.program_id(0) % tiles_per_seq) * (KV_TM // MOBA_BLOCK) + row // MOBA_BLOCK
    onehot = jnp.where(lane % C_HD == blk, 1.0, 0.0)
    odd_head = (lane // C_HD) % 2 == 1
    kb_ref[0] = jnp.where(odd_head, onehot, k).astype(BF16)
    kb_ref[1] = jnp.where(odd_head, k, onehot).astype(BF16)
    extra = jnp.where(lax.broadcasted_iota(jnp.int32, (V_ROWS - C_HD, MOBA_BLOCK), 0) == 0, 1.0, 0.0)
    for i in range(KV_TM // MOBA_BLOCK):
        rows = slice(i * MOBA_BLOCK, (i + 1) * MOBA_BLOCK)
        pieces = []
        for h in range(C_HEADS):
            pieces += [v_t[h * C_HD:(h + 1) * C_HD, rows], extra]
        vtb_ref[i] = jnp.concatenate(pieces, axis=0).astype(BF16)
        km_ref[0, i:i + 1, :] = jnp.sum(k[rows, :], axis=0, keepdims=True) * (1.0 / MOBA_BLOCK)


def _kvpost(parts, tab, seq, layer, depth, state):
    n = parts.shape[0]
    tiles_per_seq = seq // KV_TM
    blocks_per_tile = KV_TM // MOBA_BLOCK
    assert seq // MOBA_BLOCK <= C_HD
    blk = lambda c: pl.BlockSpec((KV_TM, BRANCH_W), lambda i: (i, c))
    slab = pl.BlockSpec((1, 1, C_HEADS, C_HD, KV_TM),
                        lambda i: (layer, i // tiles_per_seq, 0, 0, i % tiles_per_seq))
    carried = list(state)
    return pl.pallas_call(
        functools.partial(_kvpost_kernel, tiles_per_seq=tiles_per_seq),
        grid=(n // KV_TM,),
        in_specs=[blk(P_CK), blk(P_CV),
                  pl.BlockSpec((KV_TM, 384), lambda i: (i % tiles_per_seq, 0))]
        + [pl.BlockSpec(memory_space=pl.ANY)] * len(carried),
        input_output_aliases={3 + i: i for i in range(len(carried))},
        out_specs=[slab, slab,
                   pl.BlockSpec((2, KV_TM, BRANCH_W), lambda i: (0, i, 0)),
                   pl.BlockSpec((blocks_per_tile, C_HEADS * V_ROWS, MOBA_BLOCK), lambda i: (i, 0, 0)),
                   pl.BlockSpec((1, blocks_per_tile, BRANCH_W), lambda i: (i, 0, 0))],
        out_shape=[jax.ShapeDtypeStruct((depth, n // seq, C_HEADS, C_HD, seq), F32),
                   jax.ShapeDtypeStruct((depth, n // seq, C_HEADS, C_HD, seq), F32),
                   jax.ShapeDtypeStruct((2, n, BRANCH_W), BF16),
                   jax.ShapeDtypeStruct((n // MOBA_BLOCK, C_HEADS * V_ROWS, MOBA_BLOCK), BF16),
                   jax.ShapeDtypeStruct((n // KV_TM, blocks_per_tile, BRANCH_W), F32)],
        compiler_params=_cparams(("parallel",)),
        name="kvpost",
    )(parts, parts, tab, *carried)


MOBA_HEADS_PER_LOOP = 4


def _moba_prompt_kernel(q_ref, tab_ref, kb_ref, vtb_ref, km_ref, o_ref, qt_ref, m_ref, acc_ref, sa_ref, sb_ref):
    j = pl.program_id(1)
    tq = MOBA_BLOCK
    nb = km_ref.shape[1]
    q = _rope(q_ref[...], tab_ref[...])
    km = km_ref[0]
    km_hi = km.astype(BF16).astype(F32)
    km2 = jnp.concatenate([km_hi, km - km_hi], axis=0).astype(BF16)
    blk_i = lax.broadcasted_iota(jnp.int32, (nb, tq), 0)
    dim_i = lax.broadcasted_iota(jnp.int32, (128, tq), 0)
    for g in range(C_HEADS // 2):
        lanes = slice(g * 128, (g + 1) * 128)
        qt = q[:, lanes].T
        for hh in range(2):
            h = 2 * g + hh
            qh = jnp.where((dim_i >= hh * C_HD) & (dim_i < (hh + 1) * C_HD), qt, 0.0)
            q_hi = qh.astype(BF16)
            q_lo = (qh - q_hi.astype(F32)).astype(BF16)
            gate2 = jnp.dot(km2[:, lanes], q_hi, preferred_element_type=F32)
            gate = (gate2[0:nb, :] + gate2[nb:2 * nb, :]
                    + jnp.dot(km2[0:nb, lanes], q_lo, preferred_element_type=F32))
            sel = _top_blocks(gate, blk_i < j, 0)
            pen = jnp.where((sel > 0.0) | (blk_i == j), 0.0, NEG)
            pen = jnp.concatenate([pen, jnp.zeros((C_HD - nb, tq), F32)], axis=0)
            qs = qh * (C_HD ** -0.5 * LOG2E)
            full = (jnp.concatenate([qs[0:C_HD, :], pen], axis=0) if hh == 0
                    else jnp.concatenate([pen, qs[C_HD:2 * C_HD, :]], axis=0))
            qt_ref[h] = full.astype(BF16)
    m_ref[...] = jnp.full(m_ref.shape, NEG, F32)
    acc_ref[...] = jnp.zeros(acc_ref.shape, F32)

    def scores(heads, n, dst_ref, mask_fn=None):
        r0 = pl.multiple_of(n * MOBA_BLOCK, MOBA_BLOCK)
        for i, h in enumerate(heads):
            kblk = kb_ref[h % 2, pl.ds(r0, MOBA_BLOCK), (h // 2) * 128:(h // 2 + 1) * 128]
            s = jnp.dot(kblk, qt_ref[h], preferred_element_type=F32)
            dst_ref[i] = s if mask_fn is None else mask_fn(s)

    def accumulate(heads, n, src_ref, weight=None):
        for i, h in enumerate(heads):
            rows = slice(h * V_ROWS, (h + 1) * V_ROWS)
            m = m_ref[h:h + 1, :]
            m_new = jnp.maximum(m, jnp.max(src_ref[i], axis=0, keepdims=True))
            a = jnp.exp2(m - m_new)
            p = jnp.exp2(src_ref[i] - m_new).astype(BF16)
            pv = jnp.dot(vtb_ref[n, rows, :], p, preferred_element_type=F32)
            if weight is not None:
                pv = pv * weight
            acc_ref[rows, :] = a * acc_ref[rows, :] + pv
            m_ref[h:h + 1, :] = m_new

    key_i = lax.broadcasted_iota(jnp.int32, (MOBA_BLOCK, tq), 0)
    qry_i = lax.broadcasted_iota(jnp.int32, (MOBA_BLOCK, tq), 1)
    causal = key_i <= qry_i
    trips = (j + 1) // 2
    last_valid = jnp.where(j % 2 == 1, 0.0, 1.0)
    finish_previous = None
    for g, h0 in enumerate(range(0, C_HEADS, MOBA_HEADS_PER_LOOP)):
        heads = tuple(range(h0, h0 + MOBA_HEADS_PER_LOOP))
        a_ref, b_ref = (sa_ref, sb_ref) if g % 2 == 0 else (sb_ref, sa_ref)
        scores(heads, j, a_ref, lambda s: jnp.where(causal, s, NEG))
        if finish_previous is not None:
            finish_previous()

        def body(i, carry, heads=heads, a_ref=a_ref, b_ref=b_ref):
            n0 = 2 * i
            n1 = n0 + 1
            scores(heads, n0, b_ref)
            accumulate(heads, jnp.where(i == 0, j, n0 - 1), a_ref)
            n1c = jnp.minimum(n1, j - 1)
            scores(heads, n1c, a_ref)
            accumulate(heads, n0, b_ref)
            return carry

        lax.fori_loop(0, trips, body, 0)
        finish_previous = functools.partial(accumulate, heads, jnp.where(j == 0, j, j - 1), a_ref,
                                            weight=last_valid)
    finish_previous()
    out_t = jnp.concatenate(
        [acc_ref[h * V_ROWS:h * V_ROWS + C_HD, :] / acc_ref[h * V_ROWS + C_HD:h * V_ROWS + C_HD + 1, :]
         for h in range(C_HEADS)], axis=0)
    o_ref[...] = out_t.T


def _moba_prompt(parts, tab, kb, vtb, kmean, batch, seq):
    n = parts.shape[0]
    nb = seq // MOBA_BLOCK
    return pl.pallas_call(
        _moba_prompt_kernel,
        grid=(batch, nb),
        in_specs=[
            pl.BlockSpec((MOBA_BLOCK, BRANCH_W), lambda b, j: (b * nb + j, P_CQ)),
            pl.BlockSpec((MOBA_BLOCK, 384), lambda b, j: (j, 0)),
            pl.BlockSpec((2, seq, BRANCH_W), lambda b, j: (0, b, 0)),
            pl.BlockSpec((nb, C_HEADS * V_ROWS, MOBA_BLOCK), lambda b, j: (b, 0, 0)),
            pl.BlockSpec((1, nb, BRANCH_W), lambda b, j: (b, 0, 0)),
        ],
        out_specs=pl.BlockSpec((MOBA_BLOCK, BRANCH_W), lambda b, j: (b * nb + j, 0)),
        out_shape=jax.ShapeDtypeStruct((n, BRANCH_W), F32),
        scratch_shapes=[pltpu.VMEM((C_HEADS, 128, MOBA_BLOCK), BF16),
                        pltpu.VMEM((C_HEADS, MOBA_BLOCK), F32),
                        pltpu.VMEM((C_HEADS * V_ROWS, MOBA_BLOCK), F32),
                        pltpu.VMEM((MOBA_HEADS_PER_LOOP, MOBA_BLOCK, MOBA_BLOCK), F32),
                        pltpu.VMEM((MOBA_HEADS_PER_LOOP, MOBA_BLOCK, MOBA_BLOCK), F32)],
        compiler_params=_cparams(("parallel", "arbitrary")),
        name="moba_prompt",
    )(parts, tab, kb, vtb, kmean)


def _merge_gates(x, gpre_ref, wm_ref, bm_ref):
    h = _rms(x, gpre_ref[...]).astype(BF16)
    cols = lambda n: slice(n * D_MODEL, (n + 1) * D_MODEL)
    return [_sigmoid(jnp.dot(h, wm_ref[:, cols(n)], preferred_element_type=F32) + bm_ref[:, cols(n)])
            for n in range(N_BRANCH)]


def _merge_out(x, gates, branches, gpost_ref, wb_ref, wo_ref):
    acc = None
    for n, (gate, br) in enumerate(zip(gates, branches)):
        proj = jnp.dot(br.astype(BF16), wb_ref[n], preferred_element_type=F32)
        acc = gate * proj if acc is None else acc + gate * proj
    y = jnp.dot(acc.astype(BF16), wo_ref[...], preferred_element_type=F32)
    return x + _rms(y, gpost_ref[...])


def _mem_scores(problems):
    lanes = lambda h: slice(h * M_HD, (h + 1) * M_HD)
    return [[lax.dot_general(mq[:, lanes(h)].astype(BF16), mk_head(h).astype(BF16), NT_DIMS,
                             preferred_element_type=F32) * (M_HD ** -0.5 * LOG2E) for h in range(M_HEADS)]
            for mq, mk_head, _ in problems]


def _mem_values(problems, scores):
    probs = [[jnp.exp2(s - jnp.max(s, axis=1, keepdims=True)) for s in per_head] for per_head in scores]
    return [jnp.concatenate(
        [jnp.dot(p.astype(BF16), mv_head(h).astype(BF16), preferred_element_type=F32)
         / jnp.sum(p, axis=1, keepdims=True) for h, p in enumerate(per_head)], axis=1)
        for (_, _, mv_head), per_head in zip(problems, probs)]


def _mem_attend(problems):
    return _mem_values(problems, _mem_scores(problems))


TAIL_TM = 512


def _prompt_tail_kernel(x_ref, au_ref, av_ref, ag_ref, bb_ref, bc_ref, bx_ref, bg_ref, cg_ref, mq_ref, mg_ref,
                        hc_ref, hx_ref, oc_ref, mk_ref, mv_ref,
                        gpre_ref, gpost_ref, lng_ref, ws_ref, bs_ref, cw_ref, wm_ref, bm_ref, wb_ref, wo_ref,
                        y_ref, zt_ref, *, tiles_per_seq):
    tm = TAIL_TM
    x = x_ref[...]
    vn = _layernorm(av_ref[...], lng_ref[...]).astype(BF16)
    t_i = lax.broadcasted_iota(jnp.int32, (CHUNK, CHUNK), 0)
    s_i = lax.broadcasted_iota(jnp.int32, (CHUNK, CHUNK), 1)
    sp_cols = []
    for g in range(A_GROUPS):
        lanes = slice(g * 128, (g + 1) * 128)
        ws = jnp.where(s_i <= t_i, ws_ref[g], 0.0).astype(BF16)
        bias = bs_ref[:, g:g + 1]
        rows = [jnp.dot(ws, vn[c * CHUNK:(c + 1) * CHUNK, lanes], preferred_element_type=F32) + bias
                for c in range(tm // CHUNK)]
        sp_cols.append(jnp.concatenate(rows, axis=0))
    br_a = _silu(ag_ref[...]) * (au_ref[...] * jnp.concatenate(sp_cols, axis=1))
    mem = [(mq_ref[...], lambda h: mk_ref[:, h * M_HD:(h + 1) * M_HD], lambda h: mv_ref[:, h * M_HD:(h + 1) * M_HD])]
    out_m, = _mem_attend(mem)
    br_m = _silu(mg_ref[...]) * out_m
    z = bc_ref[...] * bx_ref[...]
    first_of_seq = (pl.program_id(0) % tiles_per_seq) == 0
    halo = jnp.where(first_of_seq, 0.0, hc_ref[...] * hx_ref[...])
    rix = lax.broadcasted_iota(jnp.int32, (tm, BRANCH_W), 0)
    z1 = jnp.where(rix == 0, halo[7:8, :], pltpu.roll(z, 1, axis=0))
    z2 = pltpu.roll(z, 2, axis=0)
    z2 = jnp.where(rix == 0, halo[6:7, :], jnp.where(rix == 1, halo[7:8, :], z2))
    conv = cw_ref[0:1, :] * z2 + cw_ref[1:2, :] * z1 + cw_ref[2:3, :] * z
    br_b = _silu(bg_ref[...]) * (bb_ref[...] * conv)
    zt_ref[0] = z[tm - 8:tm, :]
    br_c = _silu(cg_ref[...]) * oc_ref[...]
    gates = _merge_gates(x, gpre_ref, wm_ref, bm_ref)
    y_ref[...] = _merge_out(x, gates, (br_a, br_b, br_c, br_m), gpost_ref, wb_ref, wo_ref)


def _const_spec(shape):
    zeros = (0,) * len(shape)
    return pl.BlockSpec(shape, lambda i: zeros, pipeline_mode=pl.Buffered(1))


def _prompt_tail(x, parts, out_c, mkv, lw, seq):
    n = x.shape[0]
    tm = TAIL_TM
    tiles_per_seq = seq // tm
    part = lambda c: pl.BlockSpec((tm, BRANCH_W), lambda i: (i, c))
    halo = lambda c: pl.BlockSpec((8, BRANCH_W), lambda i: (jnp.maximum(i * (tm // 8) - 1, 0), c))
    row512 = pl.BlockSpec((tm, BRANCH_W), lambda i: (i, 0))
    in_specs = [
        pl.BlockSpec((tm, D_MODEL), lambda i: (i, 0)),
        part(P_AU), part(P_AV), part(P_AG), part(P_BB), part(P_BC), part(P_BX), part(P_BG),
        part(P_CG), part(P_MQ), part(P_MG),
        halo(P_BC), halo(P_BX),
        row512,
        pl.BlockSpec((N_MEM, BRANCH_W), lambda i: (i // tiles_per_seq, 0)),
        pl.BlockSpec((N_MEM, BRANCH_W), lambda i: (i // tiles_per_seq, 1)),
        _const_spec((1, D_MODEL)), _const_spec((1, D_MODEL)), _const_spec((1, BRANCH_W)),
        _const_spec((A_GROUPS, CHUNK, CHUNK)), _const_spec((CHUNK, A_GROUPS)), _const_spec((CONV_W, BRANCH_W)),
        _const_spec((D_MODEL, 4 * D_MODEL)), _const_spec((1, 4 * D_MODEL)),
        _const_spec((4, BRANCH_W, D_MODEL)), _const_spec((D_MODEL, D_MODEL)),
    ]
    return pl.pallas_call(
        functools.partial(_prompt_tail_kernel, tiles_per_seq=tiles_per_seq),
        grid=(n // tm,),
        in_specs=in_specs,
        out_specs=[pl.BlockSpec((tm, D_MODEL), lambda i: (i, 0)),
                   pl.BlockSpec((1, 8, BRANCH_W), lambda i: (i, 0, 0))],
        out_shape=[jax.ShapeDtypeStruct((n, D_MODEL), F32),
                   jax.ShapeDtypeStruct((n // tm, 8, BRANCH_W), F32)],
        compiler_params=_cparams(("parallel",)),
        name="prompt_tail",
    )(x, *([parts] * 10), parts, parts, out_c, mkv, mkv,
      lw["g_pre"], lw["g_post"], lw["ln_v_gain"], lw["w_spatial"], lw["b_spatial_t"], lw["conv_w"],
      lw["w_merge"], lw["b_merge"], lw["w_branch"], lw["w_out"])


N_PAGES = 16
PAGE = 128
PAGES_PER_BLOCK = MOBA_BLOCK // PAGE


SAMPLE_SEQS = 2


def _sample_branch_kernel(pt_ref, parts_ref, tab_ref, conv_ref, mk_ref, mv_ref, *rest, dec_seq):
    ns = SAMPLE_SEQS
    k_pages = [rest[s * N_PAGES:(s + 1) * N_PAGES] for s in range(ns)]
    v_pages = [rest[(ns + s) * N_PAGES:(ns + s + 1) * N_PAGES] for s in range(ns)]
    lng_ref, wsx_ref, bsx_ref, cw_ref, ko_ref, vo_ref, co_ref, vn_ref, br_ref = rest[2 * ns * N_PAGES:]
    del pt_ref
    r8 = SAMPLE_ROWS
    pad = jnp.zeros((r8 - dec_seq, BRANCH_W), F32)
    seq_rows = lambda s: slice(s * dec_seq, (s + 1) * dec_seq)
    part = lambda s, c: jnp.concatenate([parts_ref[seq_rows(s), c * BRANCH_W:(c + 1) * BRANCH_W], pad], axis=0)
    new_rows = lambda x: x[0:dec_seq, :]
    rix = lax.broadcasted_iota(jnp.int32, (r8, BRANCH_W), 0)

    for s in range(ns):
        vn = _layernorm(part(s, P_AV), lng_ref[...])
        vn_ref[seq_rows(s), :] = new_rows(vn)
        sp = bsx_ref[...]
        for i in range(dec_seq):
            sp = sp + jnp.where(rix >= i, wsx_ref[i], 0.0) * vn[i:i + 1, :]
        br_ref[seq_rows(s), 0:BRANCH_W] = new_rows(_silu(part(s, P_AG)) * (part(s, P_AU) * sp))

        z = part(s, P_BC) * part(s, P_BX)
        prev = conv_ref[0, s]
        zrow = lambda i, prev=prev, z=z: (prev[i:i + 1, :] if i < CONV_W - 1
                                          else z[i - (CONV_W - 1):i - (CONV_W - 2), :])
        conv = jnp.zeros((r8, BRANCH_W), F32)
        for t in range(dec_seq):
            c_t = cw_ref[0:1, :] * zrow(t) + cw_ref[1:2, :] * zrow(t + 1) + cw_ref[2:3, :] * zrow(t + 2)
            conv = jnp.where(rix == t, c_t, conv)
        br_ref[seq_rows(s), BRANCH_W:2 * BRANCH_W] = new_rows(_silu(part(s, P_BG)) * (part(s, P_BB) * conv))
        co_ref[s, 0] = z[dec_seq - (CONV_W - 1):dec_seq, :]

    tab = tab_ref[...]
    q = [_rope(part(s, P_CQ), tab) for s in range(ns)]
    k = [_rope(part(s, P_CK), tab) for s in range(ns)]
    v = [part(s, P_CV) for s in range(ns)]
    for s in range(ns):
        ko_ref[seq_rows(s), :] = new_rows(k[s])
        vo_ref[seq_rows(s), :] = new_rows(v[s])
    n_past = N_PAGES // PAGES_PER_BLOCK
    scale = C_HD ** -0.5
    hr = ns * C_HEADS * r8
    stack = lambda f: jnp.concatenate([f(s, h) for s in range(ns) for h in range(C_HEADS)], axis=0)
    head_lanes = lambda h: slice(h * C_HD, (h + 1) * C_HD)
    group = lambda x, s, h: x[(s * C_HEADS + h) * r8:(s * C_HEADS + h + 1) * r8, :]
    q_rows = stack(lambda s, h: q[s][:, head_lanes(h)]) * scale
    t_q = lax.broadcasted_iota(jnp.int32, (hr, 1), 0) % r8

    def head_block_t(pages, n, h):
        return jnp.concatenate([pages[n * PAGES_PER_BLOCK + i][0, 0, h] for i in range(PAGES_PER_BLOCK)], axis=1)

    s_own = []
    for c in range(dec_seq):
        k_c = stack(lambda s, h: jnp.broadcast_to(k[s][c:c + 1, head_lanes(h)], (r8, C_HD)))
        s_own.append(jnp.where(t_q >= c, jnp.sum(q_rows * k_c, axis=1, keepdims=True), NEG))
    q_hi = q_rows.astype(BF16).astype(F32)
    q_lo = q_rows - q_hi

    q2 = {(s, h): jnp.concatenate([group(q_hi, s, h), group(q_lo, s, h)], axis=0).astype(BF16)
          for s in range(ns) for h in range(C_HEADS)}

    def past_scores(s, h, n):
        s2 = jnp.dot(q2[s, h], head_block_t(k_pages[s], n, h).astype(BF16), preferred_element_type=F32)
        return s2[0:r8, :] + s2[r8:2 * r8, :]

    s_past = []
    gate = jnp.zeros((hr, GATE_COLS), F32)
    gate_col = lax.broadcasted_iota(jnp.int32, (hr, GATE_COLS), 1)
    for n in range(n_past):
        s_n = stack(lambda s, h: past_scores(s, h, n))
        s_past.append(s_n)
        gate = jnp.where(gate_col == n, jnp.sum(s_n, axis=1, keepdims=True) * (1.0 / MOBA_BLOCK), gate)
    sel = _top_blocks(gate, gate_col < n_past, 1)
    s_past = [jnp.where(sel[:, n:n + 1] > 0.0, s_past[n], NEG) for n in range(n_past)]
    m = s_own[0]
    for s_c in s_own[1:]:
        m = jnp.maximum(m, s_c)
    for s_n in s_past:
        m = jnp.maximum(m, jnp.max(s_n, axis=1, keepdims=True))
    l = jnp.zeros((hr, 1), F32)
    acc = jnp.zeros((hr, C_HD), F32)
    for c in range(dec_seq):
        v_c = stack(lambda s, h: jnp.broadcast_to(v[s][c:c + 1, head_lanes(h)], (r8, C_HD)))
        p = jnp.exp(s_own[c] - m)
        l = l + p
        acc = acc + p * v_c
    for n in range(n_past):
        p = jnp.exp(s_past[n] - m)
        l = l + jnp.sum(p, axis=1, keepdims=True)
        p = p.astype(BF16)
        acc = acc + stack(lambda s, h: lax.dot_general(group(p, s, h), head_block_t(v_pages[s], n, h).astype(BF16),
                                                       NT_DIMS, preferred_element_type=F32))
    acc = acc / l

    out_m = _mem_attend([(part(s, P_MQ),
                          lambda h, s=s: mk_ref[0, s, pl.ds(h, N_MEM, stride=M_HEADS), :],
                          lambda h, s=s: mv_ref[0, s, pl.ds(h, N_MEM, stride=M_HEADS), :]) for s in range(ns)])
    for s in range(ns):
        out_c = jnp.concatenate([group(acc, s, h) for h in range(C_HEADS)], axis=1)
        br_ref[seq_rows(s), 2 * BRANCH_W:3 * BRANCH_W] = new_rows(_silu(part(s, P_CG)) * out_c)
        br_ref[seq_rows(s), 3 * BRANCH_W:4 * BRANCH_W] = new_rows(_silu(part(s, P_MG)) * out_m[s])


def _sample_branch(layer, parts_s, tab, state_conv, cache_k, cache_v, cache_mem_k, cache_mem_v, page_table, lw,
                   dec_seq):
    bs = parts_s.shape[0] // dec_seq
    r8 = SAMPLE_ROWS
    ns = SAMPLE_SEQS
    page_spec = lambda s, p: pl.BlockSpec((1, 1, C_HEADS, C_HD, PAGE),
                                          lambda b, pt: (layer, pt[ns * b + s, p], 0, 0, 0))
    page_specs = [page_spec(s, p) for s in range(ns) for p in range(N_PAGES)]
    const = lambda shape: pl.BlockSpec(shape, lambda b, pt: (0,) * len(shape))
    assert (ns * dec_seq) % SAMPLE_ROWS == 0
    seq_spec = lambda rows, w: pl.BlockSpec((ns * rows, w), lambda b, pt: (b, 0))
    in_specs = [
        seq_spec(dec_seq, IN_W),
        const((r8, 384)),
        pl.BlockSpec((1, ns, CONV_W - 1, BRANCH_W), lambda b, pt: (layer, b, 0, 0)),
        pl.BlockSpec((1, ns, N_MEM * M_HEADS, M_HD), lambda b, pt: (layer, b, 0, 0)),
        pl.BlockSpec((1, ns, N_MEM * M_HEADS, M_HD), lambda b, pt: (layer, b, 0, 0)),
    ] + page_specs + page_specs + [
        const((1, BRANCH_W)), const((dec_seq, r8, BRANCH_W)), const((r8, BRANCH_W)), const((CONV_W, BRANCH_W)),
    ]
    out_specs = [seq_spec(dec_seq, BRANCH_W), seq_spec(dec_seq, BRANCH_W),
                 pl.BlockSpec((ns, 1, CONV_W - 1, BRANCH_W), lambda b, pt: (b, 0, 0, 0)),
                 seq_spec(dec_seq, BRANCH_W), seq_spec(dec_seq, 4 * BRANCH_W)]
    out_shape = [jax.ShapeDtypeStruct((bs * dec_seq, BRANCH_W), F32),
                 jax.ShapeDtypeStruct((bs * dec_seq, BRANCH_W), F32),
                 jax.ShapeDtypeStruct((bs, 1, CONV_W - 1, BRANCH_W), F32),
                 jax.ShapeDtypeStruct((bs * dec_seq, BRANCH_W), F32),
                 jax.ShapeDtypeStruct((bs * dec_seq, 4 * BRANCH_W), F32)]
    grid_spec = pltpu.PrefetchScalarGridSpec(
        num_scalar_prefetch=1, grid=(bs // ns,), in_specs=in_specs, out_specs=out_specs)
    return pl.pallas_call(
        functools.partial(_sample_branch_kernel, dec_seq=dec_seq),
        grid_spec=grid_spec,
        out_shape=out_shape,
        compiler_params=_cparams(("arbitrary",)),
        name="sample_branch",
    )(page_table, parts_s, tab, state_conv, cache_mem_k, cache_mem_v,
      *([cache_k] * (ns * N_PAGES)), *([cache_v] * (ns * N_PAGES)),
      lw["ln_v_gain"], lw["w_spatial_x"], lw["b_spatial_x"], lw["conv_w"])


def _sample_tail_kernel(x_ref, br_ref, gpre_ref, gpost_ref, wm_ref, bm_ref, wb_ref, wo_ref, y_ref):
    branches = tuple(br_ref[:, n * BRANCH_W:(n + 1) * BRANCH_W] for n in range(N_BRANCH))
    x = x_ref[...]
    y_ref[...] = _merge_out(x, _merge_gates(x, gpre_ref, wm_ref, bm_ref), branches, gpost_ref, wb_ref, wo_ref)


def _sample_tail(x, branches, lw):
    n = x.shape[0]
    tm = TAIL_TM
    return pl.pallas_call(
        _sample_tail_kernel,
        grid=(n // tm,),
        in_specs=[
            pl.BlockSpec((tm, D_MODEL), lambda i: (i, 0)),
            pl.BlockSpec((tm, 4 * BRANCH_W), lambda i: (i, 0)),
            _const_spec((1, D_MODEL)), _const_spec((1, D_MODEL)),
            _const_spec((D_MODEL, 4 * D_MODEL)), _const_spec((1, 4 * D_MODEL)),
            _const_spec((4, BRANCH_W, D_MODEL)), _const_spec((D_MODEL, D_MODEL)),
        ],
        out_specs=pl.BlockSpec((tm, D_MODEL), lambda i: (i, 0)),
        out_shape=jax.ShapeDtypeStruct((n, D_MODEL), F32),
        compiler_params=_cparams(("parallel",)),
        name="sample_tail",
    )(x, branches, lw["g_pre"], lw["g_post"], lw["w_merge"], lw["b_merge"], lw["w_branch"], lw["w_out"])


def kernel(x_prompt, x_sample, cache_k, cache_v, cache_mem_k, cache_mem_v, state_conv, page_table, mem_prompt,
           g_pre, g_post, w_in, ln_v_gain, w_spatial, b_spatial, conv_w, g_mem, w_mem_kv, w_merge, b_merge,
           w_branch, w_out):
    bp, tp, d = x_prompt.shape
    bs, ts, _ = x_sample.shape
    depth = w_in.shape[0]
    page = cache_k.shape[2]
    past_len = page_table.shape[1] * page
    assert (d, page, page_table.shape[1]) == (D_MODEL, PAGE, N_PAGES) and ts <= SAMPLE_ROWS
    assert past_len % MOBA_BLOCK == 0 and tp % KV_TM == 0 and ts >= CONV_W - 1
    r8 = SAMPLE_ROWS

    tab_p = _rope_table(jnp.arange(tp, dtype=jnp.int32))
    tab_s = _rope_table(past_len + jnp.arange(r8, dtype=jnp.int32))
    xp = x_prompt.reshape(bp * tp, d)
    xs = x_sample.reshape(bs * ts, d)
    mem = mem_prompt.reshape(bp * N_MEM, d)
    ck = jnp.transpose(cache_k, (0, 1, 3, 4, 2))
    cv = jnp.transpose(cache_v, (0, 1, 3, 4, 2))
    cmk = cache_mem_k.reshape(depth, bs, N_MEM * M_HEADS, M_HD)
    cmv = cache_mem_v.reshape(depth, bs, N_MEM * M_HEADS, M_HD)
    lane_group = jnp.arange(BRANCH_W) // (BRANCH_W // A_GROUPS)

    outs = {name: [] for name in ("cp", "mkp", "mvp", "ks", "vs", "cs", "vns")}
    kv_state = tuple(jnp.zeros((depth, bp, C_HEADS, C_HD, tp), F32) for _ in range(2))
    for l in range(depth):
        wsx = jnp.transpose(w_spatial[l][:, :r8, :ts], (2, 1, 0))[:, :, lane_group]
        bsx = jnp.transpose(b_spatial[l][:, :r8], (1, 0))[:, lane_group]
        lw = {
            "g_pre": g_pre[l].reshape(1, d), "g_post": g_post[l].reshape(1, d),
            "ln_v_gain": ln_v_gain[l].reshape(1, BRANCH_W),
            "w_spatial": w_spatial[l], "b_spatial_t": b_spatial[l].T,
            "w_spatial_x": wsx, "b_spatial_x": bsx,
            "conv_w": conv_w[l],
            "w_merge": w_merge[l].astype(BF16), "b_merge": b_merge[l].reshape(1, 4 * d),
            "w_branch": w_branch[l].astype(BF16), "w_out": w_out[l].astype(BF16),
        }
        w_in_b = w_in[l].astype(BF16)

        mkv = _inproj(mem, g_mem[l], w_mem_kv[l].astype(BF16), 1024, 1024)
        parts = _inproj(xp, g_pre[l], w_in_b, 1024, 3328)
        k_p, v_p, kb, vb, kmean = _kvpost(parts, tab_p, tp, l, depth, kv_state)
        kv_state = (k_p, v_p)
        kmean = kmean.reshape(bp, tp // MOBA_BLOCK, BRANCH_W)
        out_c = _moba_prompt(parts, tab_p, kb, vb, kmean, bp, tp)
        xp, ztail = _prompt_tail(xp, parts, out_c, mkv, lw, tp)
        outs["cp"].append(ztail.reshape(bp, tp // TAIL_TM, 8, BRANCH_W)[:, -1, 8 - (CONV_W - 1):, :])
        outs["mkp"].append(mkv[:, :BRANCH_W].reshape(bp, N_MEM, M_HEADS, M_HD))
        outs["mvp"].append(mkv[:, BRANCH_W:].reshape(bp, N_MEM, M_HEADS, M_HD))

        parts_s = _inproj(xs, g_pre[l], w_in_b, bs * ts, 1664)
        k_s, v_s, c_s, vn_s, br_s = _sample_branch(l, parts_s, tab_s, state_conv, ck, cv, cmk, cmv, page_table,
                                                   lw, ts)
        xs = _sample_tail(xs, br_s, lw)
        outs["ks"].append(k_s.reshape(bs, ts, C_HEADS, C_HD))
        outs["vs"].append(v_s.reshape(bs, ts, C_HEADS, C_HD))
        outs["cs"].append(c_s.reshape(bs, CONV_W - 1, BRANCH_W))
        outs["vns"].append(vn_s.reshape(bs, ts, BRANCH_W))

    st = lambda name: jnp.stack(outs[name])
    return (xp.reshape(bp, tp, d), xs.reshape(bs, ts, d),
            jnp.transpose(kv_state[0], (0, 1, 4, 2, 3)), jnp.transpose(kv_state[1], (0, 1, 4, 2, 3)),
            st("cp"), st("mkp"), st("mvp"),
            st("ks"), st("vs"), st("cs"), st("vns"))
```

```python
import functools

import jax
import jax.numpy as jnp
from jax import lax
from jax.experimental import pallas as pl
from jax.experimental.pallas import tpu as pltpu

D_MODEL = 1024
BRANCH_W = 512
N_PARTS = 13
IN_W = N_PARTS * BRANCH_W
CHUNK = 128
A_GROUPS = 4
CONV_W = 3
N_BRANCH = 4
C_HEADS = 8
C_HD = 64
ROPE_DIM = 16
ROPE_THETA = 500000.0
MOBA_BLOCK = 256
MOBA_TOPK = 3
M_HEADS = 4
M_HD = 128
N_MEM = 256
EPS = 1e-6

P_AU, P_AV, P_AG, P_BB, P_BC, P_BX, P_BG, P_CQ, P_CK, P_CV, P_CG, P_MQ, P_MG = range(13)

SAMPLE_ROWS = 8
NEG = -1e30
LANES = 128
ROPE_TAB_W = 3 * LANES
GATE_COLS = LANES
VMEM_LIMIT = 56 * 1024 * 1024

F32 = jnp.float32
BF16 = jnp.bfloat16
NT_DIMS = (((1,), (1,)), ((), ()))


def _cparams(sem):
    return pltpu.CompilerParams(dimension_semantics=sem, vmem_limit_bytes=VMEM_LIMIT)


def _rms(x, g):
    r = lax.rsqrt(jnp.mean(x * x, axis=-1, keepdims=True) + EPS)
    return (x * r) * g


def _layernorm(x, g):
    mu = jnp.mean(x, axis=-1, keepdims=True)
    xc = x - mu
    r = lax.rsqrt(jnp.mean(xc * xc, axis=-1, keepdims=True) + EPS)
    return (xc * r) * g


def _sigmoid(x):
    return 0.5 * jnp.tanh(0.5 * x) + 0.5


def _silu(x):
    return x * _sigmoid(x)


def _rope(x, tab):
    c, s_lo, s_hi = tab[:, 0:LANES], tab[:, LANES:2 * LANES], tab[:, 2 * LANES:3 * LANES]
    half = ROPE_DIM // 2
    outs = []
    for g in range(BRANCH_W // LANES):
        xg = x[:, g * LANES:(g + 1) * LANES]
        outs.append(xg * c + pltpu.roll(xg, LANES - half, axis=1) * s_lo + pltpu.roll(xg, half, axis=1) * s_hi)
    return jnp.concatenate(outs, axis=1)


def _rope_table(pos):
    half = ROPE_DIM // 2
    inv = jnp.power(jnp.float32(ROPE_THETA), -jnp.arange(half, dtype=F32) * (2.0 / ROPE_DIM))
    ang = pos.astype(F32)[:, None] * inv[None, :]
    cos, sin = jnp.cos(ang), jnp.sin(ang)
    d = jnp.arange(LANES) % C_HD
    idx = d % half
    cosl = jnp.where(d[None, :] < ROPE_DIM, cos[:, idx], 1.0)
    s_lo = jnp.where(d[None, :] < half, -sin[:, idx], 0.0)
    s_hi = jnp.where((d[None, :] >= half) & (d[None, :] < ROPE_DIM), sin[:, idx], 0.0)
    return jnp.concatenate([cosl, s_lo, s_hi], axis=1).astype(F32)


def _top_blocks(gate, valid, axis):
    idx = lax.broadcasted_iota(jnp.int32, gate.shape, axis).astype(F32)
    g = jnp.where(valid, gate, -jnp.inf)
    sel = jnp.zeros(gate.shape, F32)
    for _ in range(MOBA_TOPK):
        m = jnp.max(g, axis=axis, keepdims=True)
        cand = (g == m) & valid & (sel == 0.0)
        first = jnp.min(jnp.where(cand, idx, float(gate.shape[axis])), axis=axis, keepdims=True)
        pick = idx == first
        sel = jnp.where(pick, 1.0, sel)
        g = jnp.where(pick, -jnp.inf, g)
    return sel


def _inproj_kernel(x_ref, g_ref, w_ref, o_ref):
    h = _rms(x_ref[...], g_ref[...]).astype(BF16)
    o_ref[...] = jnp.dot(h, w_ref[...], preferred_element_type=F32)


def _inproj(x, g, w_bf16, tm, tn):
    n, d = x.shape
    n_out = w_bf16.shape[1]
    return pl.pallas_call(
        _inproj_kernel,
        grid=(n_out // tn, n // tm),
        in_specs=[
            pl.BlockSpec((tm, d), lambda j, i: (i, 0)),
            pl.BlockSpec((1, d), lambda j, i: (0, 0)),
            pl.BlockSpec((d, tn), lambda j, i: (0, j)),
        ],
        out_specs=pl.BlockSpec((tm, tn), lambda j, i: (i, j)),
        out_shape=jax.ShapeDtypeStruct((n, n_out), F32),
        compiler_params=_cparams(("parallel", "parallel")),
        name="inproj",
    )(x, g.reshape(1, d), w_bf16)


KV_TM = 1024


V_ROWS = 80
LOG2E = 1.4426950408889634


def _kvpost_kernel(k_ref, v_ref, tab_ref, *rest, tiles_per_seq):
    kt_ref, vt_ref, kb_ref, vtb_ref, km_ref = rest[-5:]
    k = _rope(k_ref[...], tab_ref[...])
    k_t = k.T
    v_t = v_ref[...].T
    kt_ref[0, 0] = k_t.reshape(C_HEADS, C_HD, KV_TM)
    vt_ref[0, 0] = v_t.reshape(C_HEADS, C_HD, KV_TM)
    lane = lax.broadcasted_iota(jnp.int32, (KV_TM, BRANCH_W), 1)
    row = lax.broadcasted_iota(jnp.int32, (KV_TM, BRANCH_W), 0)
    blk = (pl.program_id(0) % tiles_per_seq) * (KV_TM // MOBA_BLOCK) + row // MOBA_BLOCK
    onehot = jnp.where(lane % C_HD == blk, 1.0, 0.0)
    odd_head = (lane // C_HD) % 2 == 1
    kb_ref[0] = jnp.where(odd_head, onehot, k).astype(BF16)
    kb_ref[1] = jnp.where(odd_head, k, onehot).astype(BF16)
    extra = jnp.where(lax.broadcasted_iota(jnp.int32, (V_ROWS - C_HD, MOBA_BLOCK), 0) == 0, 1.0, 0.0)
    for i in range(KV_TM // MOBA_BLOCK):
        rows = slice(i * MOBA_BLOCK, (i + 1) * MOBA_BLOCK)
        pieces = []
        for h in range(C_HEADS):
            pieces += [v_t[h * C_HD:(h + 1) * C_HD, rows], extra]
        vtb_ref[i] = jnp.concatenate(pieces, axis=0).astype(BF16)
        km_ref[0, i:i + 1, :] = jnp.sum(k[rows, :], axis=0, keepdims=True) * (1.0 / MOBA_BLOCK)


def _kvpost(parts, tab, seq, layer, depth, state):
    n = parts.shape[0]
    tiles_per_seq = seq // KV_TM
    blocks_per_tile = KV_TM // MOBA_BLOCK
    assert seq // MOBA_BLOCK <= C_HD
    blk = lambda c: pl.BlockSpec((KV_TM, BRANCH_W), lambda i: (i, c))
    slab = pl.BlockSpec((1, 1, C_HEADS, C_HD, KV_TM),
                        lambda i: (layer, i // tiles_per_seq, 0, 0, i % tiles_per_seq))
    carried = list(state)
    return pl.pallas_call(
        functools.partial(_kvpost_kernel, tiles_per_seq=tiles_per_seq),
        grid=(n // KV_TM,),
        in_specs=[blk(P_CK), blk(P_CV),
                  pl.BlockSpec((KV_TM, ROPE_TAB_W), lambda i: (i % tiles_per_seq, 0))]
        + [pl.BlockSpec(memory_space=pl.ANY)] * len(carried),
        input_output_aliases={3 + i: i for i in range(len(carried))},
        out_specs=[slab, slab,
                   pl.BlockSpec((2, KV_TM, BRANCH_W), lambda i: (0, i, 0)),
                   pl.BlockSpec((blocks_per_tile, C_HEADS * V_ROWS, MOBA_BLOCK), lambda i: (i, 0, 0)),
                   pl.BlockSpec((1, blocks_per_tile, BRANCH_W), lambda i: (i, 0, 0))],
        out_shape=[jax.ShapeDtypeStruct((depth, n // seq, C_HEADS, C_HD, seq), F32),
                   jax.ShapeDtypeStruct((depth, n // seq, C_HEADS, C_HD, seq), F32),
                   jax.ShapeDtypeStruct((2, n, BRANCH_W), BF16),
                   jax.ShapeDtypeStruct((n // MOBA_BLOCK, C_HEADS * V_ROWS, MOBA_BLOCK), BF16),
                   jax.ShapeDtypeStruct((n // KV_TM, blocks_per_tile, BRANCH_W), F32)],
        compiler_params=_cparams(("parallel",)),
        name="kvpost",
    )(parts, parts, tab, *carried)


MOBA_HEADS_PER_LOOP = 4


def _moba_prompt_kernel(q_ref, tab_ref, kb_ref, vtb_ref, km_ref, o_ref, qt_ref, m_ref, acc_ref, sa_ref, sb_ref):
    j = pl.program_id(1)
    tq = MOBA_BLOCK
    nb = km_ref.shape[1]
    q = _rope(q_ref[...], tab_ref[...])
    km = km_ref[0]
    km_hi = km.astype(BF16).astype(F32)
    km2 = jnp.concatenate([km_hi, km - km_hi], axis=0).astype(BF16)
    blk_i = lax.broadcasted_iota(jnp.int32, (nb, tq), 0)
    dim_i = lax.broadcasted_iota(jnp.int32, (LANES, tq), 0)
    for g in range(C_HEADS // 2):
        lanes = slice(g * LANES, (g + 1) * LANES)
        qt = q[:, lanes].T
        for hh in range(2):
            h = 2 * g + hh
            qh = jnp.where((dim_i >= hh * C_HD) & (dim_i < (hh + 1) * C_HD), qt, 0.0)
            q_hi = qh.astype(BF16)
            q_lo = (qh - q_hi.astype(F32)).astype(BF16)
            gate2 = jnp.dot(km2[:, lanes], q_hi, preferred_element_type=F32)
            gate = (gate2[0:nb, :] + gate2[nb:2 * nb, :]
                    + jnp.dot(km2[0:nb, lanes], q_lo, preferred_element_type=F32))
            sel = _top_blocks(gate, blk_i < j, 0)
            pen = jnp.where((sel > 0.0) | (blk_i == j), 0.0, NEG)
            pen = jnp.concatenate([pen, jnp.zeros((C_HD - nb, tq), F32)], axis=0)
            qs = qh * (C_HD ** -0.5 * LOG2E)
            full = (jnp.concatenate([qs[0:C_HD, :], pen], axis=0) if hh == 0
                    else jnp.concatenate([pen, qs[C_HD:2 * C_HD, :]], axis=0))
            qt_ref[h] = full.astype(BF16)
    m_ref[...] = jnp.full(m_ref.shape, NEG, F32)
    acc_ref[...] = jnp.zeros(acc_ref.shape, F32)

    def scores(heads, n, dst_ref, mask_fn=None):
        r0 = pl.multiple_of(n * MOBA_BLOCK, MOBA_BLOCK)
        for i, h in enumerate(heads):
            kblk = kb_ref[h % 2, pl.ds(r0, MOBA_BLOCK), (h // 2) * LANES:(h // 2 + 1) * LANES]
            s = jnp.dot(kblk, qt_ref[h], preferred_element_type=F32)
            dst_ref[i] = s if mask_fn is None else mask_fn(s)

    def accumulate(heads, n, src_ref, weight=None):
        for i, h in enumerate(heads):
            rows = slice(h * V_ROWS, (h + 1) * V_ROWS)
            m = m_ref[h:h + 1, :]
            m_new = jnp.maximum(m, jnp.max(src_ref[i], axis=0, keepdims=True))
            a = jnp.exp2(m - m_new)
            p = jnp.exp2(src_ref[i] - m_new).astype(BF16)
            pv = jnp.dot(vtb_ref[n, rows, :], p, preferred_element_type=F32)
            if weight is not None:
                pv = pv * weight
            acc_ref[rows, :] = a * acc_ref[rows, :] + pv
            m_ref[h:h + 1, :] = m_new

    key_i = lax.broadcasted_iota(jnp.int32, (MOBA_BLOCK, tq), 0)
    qry_i = lax.broadcasted_iota(jnp.int32, (MOBA_BLOCK, tq), 1)
    causal = key_i <= qry_i
    trips = (j + 1) // 2
    last_valid = jnp.where(j % 2 == 1, 0.0, 1.0)
    finish_previous = None
    for g, h0 in enumerate(range(0, C_HEADS, MOBA_HEADS_PER_LOOP)):
        heads = tuple(range(h0, h0 + MOBA_HEADS_PER_LOOP))
        a_ref, b_ref = (sa_ref, sb_ref) if g % 2 == 0 else (sb_ref, sa_ref)
        scores(heads, j, a_ref, lambda s: jnp.where(causal, s, NEG))
        if finish_previous is not None:
            finish_previous()

        def body(i, carry, heads=heads, a_ref=a_ref, b_ref=b_ref):
            n0 = 2 * i
            n1 = n0 + 1
            scores(heads, n0, b_ref)
            accumulate(heads, jnp.where(i == 0, j, n0 - 1), a_ref)
            n1c = jnp.minimum(n1, j - 1)
            scores(heads, n1c, a_ref)
            accumulate(heads, n0, b_ref)
            return carry

        lax.fori_loop(0, trips, body, 0)
        finish_previous = functools.partial(accumulate, heads, jnp.where(j == 0, j, j - 1), a_ref,
                                            weight=last_valid)
    finish_previous()
    out_t = jnp.concatenate(
        [acc_ref[h * V_ROWS:h * V_ROWS + C_HD, :] / acc_ref[h * V_ROWS + C_HD:h * V_ROWS + C_HD + 1, :]
         for h in range(C_HEADS)], axis=0)
    o_ref[...] = out_t.T


def _moba_prompt(parts, tab, kb, vtb, kmean, batch, seq):
    n = parts.shape[0]
    nb = seq // MOBA_BLOCK
    return pl.pallas_call(
        _moba_prompt_kernel,
        grid=(batch, nb),
        in_specs=[
            pl.BlockSpec((MOBA_BLOCK, BRANCH_W), lambda b, j: (b * nb + j, P_CQ)),
            pl.BlockSpec((MOBA_BLOCK, ROPE_TAB_W), lambda b, j: (j, 0)),
            pl.BlockSpec((2, seq, BRANCH_W), lambda b, j: (0, b, 0)),
            pl.BlockSpec((nb, C_HEADS * V_ROWS, MOBA_BLOCK), lambda b, j: (b, 0, 0)),
            pl.BlockSpec((1, nb, BRANCH_W), lambda b, j: (b, 0, 0)),
        ],
        out_specs=pl.BlockSpec((MOBA_BLOCK, BRANCH_W), lambda b, j: (b * nb + j, 0)),
        out_shape=jax.ShapeDtypeStruct((n, BRANCH_W), F32),
        scratch_shapes=[pltpu.VMEM((C_HEADS, LANES, MOBA_BLOCK), BF16),
                        pltpu.VMEM((C_HEADS, MOBA_BLOCK), F32),
                        pltpu.VMEM((C_HEADS * V_ROWS, MOBA_BLOCK), F32),
                        pltpu.VMEM((MOBA_HEADS_PER_LOOP, MOBA_BLOCK, MOBA_BLOCK), F32),
                        pltpu.VMEM((MOBA_HEADS_PER_LOOP, MOBA_BLOCK, MOBA_BLOCK), F32)],
        compiler_params=_cparams(("parallel", "arbitrary")),
        name="moba_prompt",
    )(parts, tab, kb, vtb, kmean)


def _merge_gates(x, gpre_ref, wm_ref, bm_ref):
    h = _rms(x, gpre_ref[...]).astype(BF16)
    cols = lambda n: slice(n * D_MODEL, (n + 1) * D_MODEL)
    return [_sigmoid(jnp.dot(h, wm_ref[:, cols(n)], preferred_element_type=F32) + bm_ref[:, cols(n)])
            for n in range(N_BRANCH)]


def _merge_out(x, gates, branches, gpost_ref, wb_ref, wo_ref):
    acc = None
    for n, (gate, br) in enumerate(zip(gates, branches)):
        proj = jnp.dot(br.astype(BF16), wb_ref[n], preferred_element_type=F32)
        acc = gate * proj if acc is None else acc + gate * proj
    y = jnp.dot(acc.astype(BF16), wo_ref[...], preferred_element_type=F32)
    return x + _rms(y, gpost_ref[...])


def _mem_scores(problems):
    lanes = lambda h: slice(h * M_HD, (h + 1) * M_HD)
    return [[lax.dot_general(mq[:, lanes(h)].astype(BF16), mk_head(h).astype(BF16), NT_DIMS,
                             preferred_element_type=F32) * (M_HD ** -0.5 * LOG2E) for h in range(M_HEADS)]
            for mq, mk_head, _ in problems]


def _mem_values(problems, scores):
    probs = [[jnp.exp2(s - jnp.max(s, axis=1, keepdims=True)) for s in per_head] for per_head in scores]
    return [jnp.concatenate(
        [jnp.dot(p.astype(BF16), mv_head(h).astype(BF16), preferred_element_type=F32)
         / jnp.sum(p, axis=1, keepdims=True) for h, p in enumerate(per_head)], axis=1)
        for (_, _, mv_head), per_head in zip(problems, probs)]


def _mem_attend(problems):
    return _mem_values(problems, _mem_scores(problems))


TAIL_TM = 512


def _prompt_tail_kernel(x_ref, au_ref, av_ref, ag_ref, bb_ref, bc_ref, bx_ref, bg_ref, cg_ref, mq_ref, mg_ref,
                        hc_ref, hx_ref, oc_ref, mk_ref, mv_ref,
                        gpre_ref, gpost_ref, lng_ref, ws_ref, bs_ref, cw_ref, wm_ref, bm_ref, wb_ref, wo_ref,
                        y_ref, zt_ref, *, tiles_per_seq):
    tm = TAIL_TM
    x = x_ref[...]
    vn = _layernorm(av_ref[...], lng_ref[...]).astype(BF16)
    t_i = lax.broadcasted_iota(jnp.int32, (CHUNK, CHUNK), 0)
    s_i = lax.broadcasted_iota(jnp.int32, (CHUNK, CHUNK), 1)
    sp_cols = []
    for g in range(A_GROUPS):
        lanes = slice(g * LANES, (g + 1) * LANES)
        ws = jnp.where(s_i <= t_i, ws_ref[g], 0.0).astype(BF16)
        bias = bs_ref[:, g:g + 1]
        rows = [jnp.dot(ws, vn[c * CHUNK:(c + 1) * CHUNK, lanes], preferred_element_type=F32) + bias
                for c in range(tm // CHUNK)]
        sp_cols.append(jnp.concatenate(rows, axis=0))
    br_a = _silu(ag_ref[...]) * (au_ref[...] * jnp.concatenate(sp_cols, axis=1))
    mem = [(mq_ref[...], lambda h: mk_ref[:, h * M_HD:(h + 1) * M_HD], lambda h: mv_ref[:, h * M_HD:(h + 1) * M_HD])]
    out_m, = _mem_attend(mem)
    br_m = _silu(mg_ref[...]) * out_m
    z = bc_ref[...] * bx_ref[...]
    first_of_seq = (pl.program_id(0) % tiles_per_seq) == 0
    halo = jnp.where(first_of_seq, 0.0, hc_ref[...] * hx_ref[...])
    rix = lax.broadcasted_iota(jnp.int32, (tm, BRANCH_W), 0)
    z1 = jnp.where(rix == 0, halo[7:8, :], pltpu.roll(z, 1, axis=0))
    z2 = pltpu.roll(z, 2, axis=0)
    z2 = jnp.where(rix == 0, halo[6:7, :], jnp.where(rix == 1, halo[7:8, :], z2))
    conv = cw_ref[0:1, :] * z2 + cw_ref[1:2, :] * z1 + cw_ref[2:3, :] * z
    br_b = _silu(bg_ref[...]) * (bb_ref[...] * conv)
    zt_ref[0] = z[tm - 8:tm, :]
    br_c = _silu(cg_ref[...]) * oc_ref[...]
    gates = _merge_gates(x, gpre_ref, wm_ref, bm_ref)
    y_ref[...] = _merge_out(x, gates, (br_a, br_b, br_c, br_m), gpost_ref, wb_ref, wo_ref)


def _const_spec(shape):
    zeros = (0,) * len(shape)
    return pl.BlockSpec(shape, lambda i: zeros, pipeline_mode=pl.Buffered(1))


def _prompt_tail(x, parts, out_c, mkv, lw, seq):
    n = x.shape[0]
    tm = TAIL_TM
    tiles_per_seq = seq // tm
    part = lambda c: pl.BlockSpec((tm, BRANCH_W), lambda i: (i, c))
    halo = lambda c: pl.BlockSpec((8, BRANCH_W), lambda i: (jnp.maximum(i * (tm // 8) - 1, 0), c))
    row512 = pl.BlockSpec((tm, BRANCH_W), lambda i: (i, 0))
    in_specs = [
        pl.BlockSpec((tm, D_MODEL), lambda i: (i, 0)),
        part(P_AU), part(P_AV), part(P_AG), part(P_BB), part(P_BC), part(P_BX), part(P_BG),
        part(P_CG), part(P_MQ), part(P_MG),
        halo(P_BC), halo(P_BX),
        row512,
        pl.BlockSpec((N_MEM, BRANCH_W), lambda i: (i // tiles_per_seq, 0)),
        pl.BlockSpec((N_MEM, BRANCH_W), lambda i: (i // tiles_per_seq, 1)),
        _const_spec((1, D_MODEL)), _const_spec((1, D_MODEL)), _const_spec((1, BRANCH_W)),
        _const_spec((A_GROUPS, CHUNK, CHUNK)), _const_spec((CHUNK, A_GROUPS)), _const_spec((CONV_W, BRANCH_W)),
        _const_spec((D_MODEL, 4 * D_MODEL)), _const_spec((1, 4 * D_MODEL)),
        _const_spec((4, BRANCH_W, D_MODEL)), _const_spec((D_MODEL, D_MODEL)),
    ]
    return pl.pallas_call(
        functools.partial(_prompt_tail_kernel, tiles_per_seq=tiles_per_seq),
        grid=(n // tm,),
        in_specs=in_specs,
        out_specs=[pl.BlockSpec((tm, D_MODEL), lambda i: (i, 0)),
                   pl.BlockSpec((1, 8, BRANCH_W), lambda i: (i, 0, 0))],
        out_shape=[jax.ShapeDtypeStruct((n, D_MODEL), F32),
                   jax.ShapeDtypeStruct((n // tm, 8, BRANCH_W), F32)],
        compiler_params=_cparams(("parallel",)),
        name="prompt_tail",
    )(x, *([parts] * 10), parts, parts, out_c, mkv, mkv,
      lw["g_pre"], lw["g_post"], lw["ln_v_gain"], lw["w_spatial"], lw["b_spatial_t"], lw["conv_w"],
      lw["w_merge"], lw["b_merge"], lw["w_branch"], lw["w_out"])


N_PAGES = 16
PAGE = 128
PAGES_PER_BLOCK = MOBA_BLOCK // PAGE


SAMPLE_SEQS = 2


def _sample_branch_kernel(pt_ref, parts_ref, tab_ref, conv_ref, mk_ref, mv_ref, *rest, dec_seq):
    ns = SAMPLE_SEQS
    k_pages = [rest[s * N_PAGES:(s + 1) * N_PAGES] for s in range(ns)]
    v_pages = [rest[(ns + s) * N_PAGES:(ns + s + 1) * N_PAGES] for s in range(ns)]
    lng_ref, wsx_ref, bsx_ref, cw_ref, ko_ref, vo_ref, co_ref, vn_ref, br_ref = rest[2 * ns * N_PAGES:]
    del pt_ref
    r8 = SAMPLE_ROWS
    pad = jnp.zeros((r8 - dec_seq, BRANCH_W), F32)
    seq_rows = lambda s: slice(s * dec_seq, (s + 1) * dec_seq)
    part = lambda s, c: jnp.concatenate([parts_ref[seq_rows(s), c * BRANCH_W:(c + 1) * BRANCH_W], pad], axis=0)
    new_rows = lambda x: x[0:dec_seq, :]
    rix = lax.broadcasted_iota(jnp.int32, (r8, BRANCH_W), 0)

    for s in range(ns):
        vn = _layernorm(part(s, P_AV), lng_ref[...])
        vn_ref[seq_rows(s), :] = new_rows(vn)
        sp = bsx_ref[...]
        for i in range(dec_seq):
            sp = sp + jnp.where(rix >= i, wsx_ref[i], 0.0) * vn[i:i + 1, :]
        br_ref[seq_rows(s), 0:BRANCH_W] = new_rows(_silu(part(s, P_AG)) * (part(s, P_AU) * sp))

        z = part(s, P_BC) * part(s, P_BX)
        prev = conv_ref[0, s]
        zrow = lambda i, prev=prev, z=z: (prev[i:i + 1, :] if i < CONV_W - 1
                                          else z[i - (CONV_W - 1):i - (CONV_W - 2), :])
        conv = jnp.zeros((r8, BRANCH_W), F32)
        for t in range(dec_seq):
            c_t = cw_ref[0:1, :] * zrow(t) + cw_ref[1:2, :] * zrow(t + 1) + cw_ref[2:3, :] * zrow(t + 2)
            conv = jnp.where(rix == t, c_t, conv)
        br_ref[seq_rows(s), BRANCH_W:2 * BRANCH_W] = new_rows(_silu(part(s, P_BG)) * (part(s, P_BB) * conv))
        co_ref[s, 0] = z[dec_seq - (CONV_W - 1):dec_seq, :]

    tab = tab_ref[...]
    q = [_rope(part(s, P_CQ), tab) for s in range(ns)]
    k = [_rope(part(s, P_CK), tab) for s in range(ns)]
    v = [part(s, P_CV) for s in range(ns)]
    for s in range(ns):
        ko_ref[seq_rows(s), :] = new_rows(k[s])
        vo_ref[seq_rows(s), :] = new_rows(v[s])
    n_past = N_PAGES // PAGES_PER_BLOCK
    scale = C_HD ** -0.5
    hr = ns * C_HEADS * r8
    stack = lambda f: jnp.concatenate([f(s, h) for s in range(ns) for h in range(C_HEADS)], axis=0)
    head_lanes = lambda h: slice(h * C_HD, (h + 1) * C_HD)
    group = lambda x, s, h: x[(s * C_HEADS + h) * r8:(s * C_HEADS + h + 1) * r8, :]
    q_rows = stack(lambda s, h: q[s][:, head_lanes(h)]) * scale
    t_q = lax.broadcasted_iota(jnp.int32, (hr, 1), 0) % r8

    def head_block_t(pages, n, h):
        return jnp.concatenate([pages[n * PAGES_PER_BLOCK + i][0, 0, h] for i in range(PAGES_PER_BLOCK)], axis=1)

    s_own = []
    for c in range(dec_seq):
        k_c = stack(lambda s, h: jnp.broadcast_to(k[s][c:c + 1, head_lanes(h)], (r8, C_HD)))
        s_own.append(jnp.where(t_q >= c, jnp.sum(q_rows * k_c, axis=1, keepdims=True), NEG))
    q_hi = q_rows.astype(BF16).astype(F32)
    q_lo = q_rows - q_hi

    q2 = {(s, h): jnp.concatenate([group(q_hi, s, h), group(q_lo, s, h)], axis=0).astype(BF16)
          for s in range(ns) for h in range(C_HEADS)}

    def past_scores(s, h, n):
        s2 = jnp.dot(q2[s, h], head_block_t(k_pages[s], n, h).astype(BF16), preferred_element_type=F32)
        return s2[0:r8, :] + s2[r8:2 * r8, :]

    s_past = []
    gate = jnp.zeros((hr, GATE_COLS), F32)
    gate_col = lax.broadcasted_iota(jnp.int32, (hr, GATE_COLS), 1)
    for n in range(n_past):
        s_n = stack(lambda s, h: past_scores(s, h, n))
        s_past.append(s_n)
        gate = jnp.where(gate_col == n, jnp.sum(s_n, axis=1, keepdims=True) * (1.0 / MOBA_BLOCK), gate)
    sel = _top_blocks(gate, gate_col < n_past, 1)
    s_past = [jnp.where(sel[:, n:n + 1] > 0.0, s_past[n], NEG) for n in range(n_past)]
    m = s_own[0]
    for s_c in s_own[1:]:
        m = jnp.maximum(m, s_c)
    for s_n in s_past:
        m = jnp.maximum(m, jnp.max(s_n, axis=1, keepdims=True))
    l = jnp.zeros((hr, 1), F32)
    acc = jnp.zeros((hr, C_HD), F32)
    for c in range(dec_seq):
        v_c = stack(lambda s, h: jnp.broadcast_to(v[s][c:c + 1, head_lanes(h)], (r8, C_HD)))
        p = jnp.exp(s_own[c] - m)
        l = l + p
        acc = acc + p * v_c
    for n in range(n_past):
        p = jnp.exp(s_past[n] - m)
        l = l + jnp.sum(p, axis=1, keepdims=True)
        p = p.astype(BF16)
        acc = acc + stack(lambda s, h: lax.dot_general(group(p, s, h), head_block_t(v_pages[s], n, h).astype(BF16),
                                                       NT_DIMS, preferred_element_type=F32))
    acc = acc / l

    out_m = _mem_attend([(part(s, P_MQ),
                          lambda h, s=s: mk_ref[0, s, pl.ds(h, N_MEM, stride=M_HEADS), :],
                          lambda h, s=s: mv_ref[0, s, pl.ds(h, N_MEM, stride=M_HEADS), :]) for s in range(ns)])
    for s in range(ns):
        out_c = jnp.concatenate([group(acc, s, h) for h in range(C_HEADS)], axis=1)
        br_ref[seq_rows(s), 2 * BRANCH_W:3 * BRANCH_W] = new_rows(_silu(part(s, P_CG)) * out_c)
        br_ref[seq_rows(s), 3 * BRANCH_W:4 * BRANCH_W] = new_rows(_silu(part(s, P_MG)) * out_m[s])


def _sample_branch(layer, parts_s, tab, state_conv, cache_k, cache_v, cache_mem_k, cache_mem_v, page_table, lw,
                   dec_seq):
    bs = parts_s.shape[0] // dec_seq
    r8 = SAMPLE_ROWS
    ns = SAMPLE_SEQS
    page_spec = lambda s, p: pl.BlockSpec((1, 1, C_HEADS, C_HD, PAGE),
                                          lambda b, pt: (layer, pt[ns * b + s, p], 0, 0, 0))
    page_specs = [page_spec(s, p) for s in range(ns) for p in range(N_PAGES)]
    const = lambda shape: pl.BlockSpec(shape, lambda b, pt: (0,) * len(shape))
    assert (ns * dec_seq) % SAMPLE_ROWS == 0
    seq_spec = lambda rows, w: pl.BlockSpec((ns * rows, w), lambda b, pt: (b, 0))
    in_specs = [
        seq_spec(dec_seq, IN_W),
        const((r8, ROPE_TAB_W)),
        pl.BlockSpec((1, ns, CONV_W - 1, BRANCH_W), lambda b, pt: (layer, b, 0, 0)),
        pl.BlockSpec((1, ns, N_MEM * M_HEADS, M_HD), lambda b, pt: (layer, b, 0, 0)),
        pl.BlockSpec((1, ns, N_MEM * M_HEADS, M_HD), lambda b, pt: (layer, b, 0, 0)),
    ] + page_specs + page_specs + [
        const((1, BRANCH_W)), const((dec_seq, r8, BRANCH_W)), const((r8, BRANCH_W)), const((CONV_W, BRANCH_W)),
    ]
    out_specs = [seq_spec(dec_seq, BRANCH_W), seq_spec(dec_seq, BRANCH_W),
                 pl.BlockSpec((ns, 1, CONV_W - 1, BRANCH_W), lambda b, pt: (b, 0, 0, 0)),
                 seq_spec(dec_seq, BRANCH_W), seq_spec(dec_seq, 4 * BRANCH_W)]
    out_shape = [jax.ShapeDtypeStruct((bs * dec_seq, BRANCH_W), F32),
                 jax.ShapeDtypeStruct((bs * dec_seq, BRANCH_W), F32),
                 jax.ShapeDtypeStruct((bs, 1, CONV_W - 1, BRANCH_W), F32),
                 jax.ShapeDtypeStruct((bs * dec_seq, BRANCH_W), F32),
                 jax.ShapeDtypeStruct((bs * dec_seq, 4 * BRANCH_W), F32)]
    grid_spec = pltpu.PrefetchScalarGridSpec(
        num_scalar_prefetch=1, grid=(bs // ns,), in_specs=in_specs, out_specs=out_specs)
    return pl.pallas_call(
        functools.partial(_sample_branch_kernel, dec_seq=dec_seq),
        grid_spec=grid_spec,
        out_shape=out_shape,
        compiler_params=_cparams(("arbitrary",)),
        name="sample_branch",
    )(page_table, parts_s, tab, state_conv, cache_mem_k, cache_mem_v,
      *([cache_k] * (ns * N_PAGES)), *([cache_v] * (ns * N_PAGES)),
      lw["ln_v_gain"], lw["w_spatial_x"], lw["b_spatial_x"], lw["conv_w"])


def _sample_tail_kernel(x_ref, br_ref, gpre_ref, gpost_ref, wm_ref, bm_ref, wb_ref, wo_ref, y_ref):
    branches = tuple(br_ref[:, n * BRANCH_W:(n + 1) * BRANCH_W] for n in range(N_BRANCH))
    x = x_ref[...]
    y_ref[...] = _merge_out(x, _merge_gates(x, gpre_ref, wm_ref, bm_ref), branches, gpost_ref, wb_ref, wo_ref)


def _sample_tail(x, branches, lw):
    n = x.shape[0]
    tm = TAIL_TM
    return pl.pallas_call(
        _sample_tail_kernel,
        grid=(n // tm,),
        in_specs=[
            pl.BlockSpec((tm, D_MODEL), lambda i: (i, 0)),
            pl.BlockSpec((tm, 4 * BRANCH_W), lambda i: (i, 0)),
            _const_spec((1, D_MODEL)), _const_spec((1, D_MODEL)),
            _const_spec((D_MODEL, 4 * D_MODEL)), _const_spec((1, 4 * D_MODEL)),
            _const_spec((4, BRANCH_W, D_MODEL)), _const_spec((D_MODEL, D_MODEL)),
        ],
        out_specs=pl.BlockSpec((tm, D_MODEL), lambda i: (i, 0)),
        out_shape=jax.ShapeDtypeStruct((n, D_MODEL), F32),
        compiler_params=_cparams(("parallel",)),
        name="sample_tail",
    )(x, branches, lw["g_pre"], lw["g_post"], lw["w_merge"], lw["b_merge"], lw["w_branch"], lw["w_out"])


def kernel(x_prompt, x_sample, cache_k, cache_v, cache_mem_k, cache_mem_v, state_conv, page_table, mem_prompt,
           g_pre, g_post, w_in, ln_v_gain, w_spatial, b_spatial, conv_w, g_mem, w_mem_kv, w_merge, b_merge,
           w_branch, w_out):
    bp, tp, d = x_prompt.shape
    bs, ts, _ = x_sample.shape
    depth = w_in.shape[0]
    page = cache_k.shape[2]
    past_len = page_table.shape[1] * page
    assert (d, page, page_table.shape[1]) == (D_MODEL, PAGE, N_PAGES) and ts <= SAMPLE_ROWS
    assert past_len % MOBA_BLOCK == 0 and tp % KV_TM == 0 and ts >= CONV_W - 1
    r8 = SAMPLE_ROWS

    tab_p = _rope_table(jnp.arange(tp, dtype=jnp.int32))
    tab_s = _rope_table(past_len + jnp.arange(r8, dtype=jnp.int32))
    xp = x_prompt.reshape(bp * tp, d)
    xs = x_sample.reshape(bs * ts, d)
    mem = mem_prompt.reshape(bp * N_MEM, d)
    ck = jnp.transpose(cache_k, (0, 1, 3, 4, 2))
    cv = jnp.transpose(cache_v, (0, 1, 3, 4, 2))
    cmk = cache_mem_k.reshape(depth, bs, N_MEM * M_HEADS, M_HD)
    cmv = cache_mem_v.reshape(depth, bs, N_MEM * M_HEADS, M_HD)
    lane_group = jnp.arange(BRANCH_W) // (BRANCH_W // A_GROUPS)

    outs = {name: [] for name in ("cp", "mkp", "mvp", "ks", "vs", "cs", "vns")}
    kv_state = tuple(jnp.zeros((depth, bp, C_HEADS, C_HD, tp), F32) for _ in range(2))
    for l in range(depth):
        wsx = jnp.transpose(w_spatial[l][:, :r8, :ts], (2, 1, 0))[:, :, lane_group]
        bsx = jnp.transpose(b_spatial[l][:, :r8], (1, 0))[:, lane_group]
        lw = {
            "g_pre": g_pre[l].reshape(1, d), "g_post": g_post[l].reshape(1, d),
            "ln_v_gain": ln_v_gain[l].reshape(1, BRANCH_W),
            "w_spatial": w_spatial[l], "b_spatial_t": b_spatial[l].T,
            "w_spatial_x": wsx, "b_spatial_x": bsx,
            "conv_w": conv_w[l],
            "w_merge": w_merge[l].astype(BF16), "b_merge": b_merge[l].reshape(1, 4 * d),
            "w_branch": w_branch[l].astype(BF16), "w_out": w_out[l].astype(BF16),
        }
        w_in_b = w_in[l].astype(BF16)

        mkv = _inproj(mem, g_mem[l], w_mem_kv[l].astype(BF16), 1024, 1024)
        parts = _inproj(xp, g_pre[l], w_in_b, 1024, 3328)
        k_p, v_p, kb, vtb, kmean = _kvpost(parts, tab_p, tp, l, depth, kv_state)
        kv_state = (k_p, v_p)
        kmean = kmean.reshape(bp, tp // MOBA_BLOCK, BRANCH_W)
        out_c = _moba_prompt(parts, tab_p, kb, vtb, kmean, bp, tp)
        xp, ztail = _prompt_tail(xp, parts, out_c, mkv, lw, tp)
        outs["cp"].append(ztail.reshape(bp, tp // TAIL_TM, 8, BRANCH_W)[:, -1, 8 - (CONV_W - 1):, :])
        outs["mkp"].append(mkv[:, :BRANCH_W].reshape(bp, N_MEM, M_HEADS, M_HD))
        outs["mvp"].append(mkv[:, BRANCH_W:].reshape(bp, N_MEM, M_HEADS, M_HD))

        parts_s = _inproj(xs, g_pre[l], w_in_b, bs * ts, 1664)
        k_s, v_s, c_s, vn_s, br_s = _sample_branch(l, parts_s, tab_s, state_conv, ck, cv, cmk, cmv, page_table,
                                                   lw, ts)
        xs = _sample_tail(xs, br_s, lw)
        outs["ks"].append(k_s.reshape(bs, ts, C_HEADS, C_HD))
        outs["vs"].append(v_s.reshape(bs, ts, C_HEADS, C_HD))
        outs["cs"].append(c_s.reshape(bs, CONV_W - 1, BRANCH_W))
        outs["vns"].append(vn_s.reshape(bs, ts, BRANCH_W))

    st = lambda name: jnp.stack(outs[name])
    return (xp.reshape(bp, tp, d), xs.reshape(bs, ts, d),
            jnp.transpose(kv_state[0], (0, 1, 4, 2, 3)), jnp.transpose(kv_state[1], (0, 1, 4, 2, 3)),
            st("cp"), st("mkp"), st("mvp"),
            st("ks"), st("vs"), st("cs"), st("vns"))
```
